```python
import math
import jax, jax.numpy as jnp
from jax import lax
import numpy as np

D_MODEL = 1024
BATCH = 8
SEQ = 2048
DEPTH = 2

GRID_W = 64
HEAD_DIM = 64
NA_HEADS = D_MODEL // 128
NA_WIDTH = NA_HEADS * HEAD_DIM
NA_KH_MAX = 8
NA_KW = 16
GQA_HEADS = D_MODEL // 128
GQA_KV_HEADS = max(1, GQA_HEADS // 4)
GQA_WIDTH = GQA_HEADS * HEAD_DIM
GQA_KV_WIDTH = GQA_KV_HEADS * HEAD_DIM
Q_BLOCK = 128
ROPE_THETA = 10000.0
ROPE_AXIS_DIM = HEAD_DIM // 2
SG_WIDTH = D_MODEL // 2
SG_GROUPS = SG_WIDTH // HEAD_DIM
SG_CHUNK = 128
N_BRANCH = 3
LN_EPS = 1e-5
RMS_EPS = 1e-6
DEEPNORM_ALPHA = (2.0 * DEPTH) ** 0.25
DEEPNORM_BETA = (8.0 * DEPTH) ** -0.25
IN_SPLITS = (NA_WIDTH, NA_WIDTH, NA_WIDTH, NA_WIDTH,
             GQA_WIDTH, GQA_KV_WIDTH, GQA_KV_WIDTH, GQA_WIDTH,
             SG_WIDTH, SG_WIDTH, SG_WIDTH,
             N_BRANCH * D_MODEL)
IN_WIDTH = 4 * NA_WIDTH + 2 * GQA_WIDTH + 2 * GQA_KV_WIDTH + 3 * SG_WIDTH + N_BRANCH * D_MODEL

kernel_name = "hybrid_na_gqa_sgmlp_deepnorm_encoder"


def _split_points():
    pts, acc = [], 0
    for w in IN_SPLITS[:-1]:
        acc += w
        pts.append(acc)
    return pts


def layer_norm(x, g, b):
    xf = x.astype(jnp.float32)
    mu = jnp.mean(xf, axis=-1, keepdims=True)
    var = jnp.mean(jnp.square(xf - mu), axis=-1, keepdims=True)
    return ((xf - mu) * lax.rsqrt(var + LN_EPS)).astype(x.dtype) * g + b


def rms_norm(x, g):
    xf = x.astype(jnp.float32)
    ms = jnp.mean(jnp.square(xf), axis=-1, keepdims=True)
    return (xf * lax.rsqrt(ms + RMS_EPS)).astype(x.dtype) * g


def axial_rope_tables(s):
    t = jnp.arange(s)
    row = (t // GRID_W).astype(jnp.float32)
    col = (t % GRID_W).astype(jnp.float32)
    freqs = ROPE_THETA ** (-jnp.arange(0, ROPE_AXIS_DIM, 2, dtype=jnp.float32) / ROPE_AXIS_DIM)
    ang = jnp.concatenate([row[:, None] * freqs, col[:, None] * freqs], axis=-1)
    return jnp.cos(ang), jnp.sin(ang)


def apply_rope(x, cos, sin):
    xf = x.astype(jnp.float32).reshape(*x.shape[:-1], HEAD_DIM // 2, 2)
    x0, x1 = xf[..., 0], xf[..., 1]
    c = cos[None, :, None, :]
    s = sin[None, :, None, :]
    out = jnp.stack([x0 * c - x1 * s, x0 * s + x1 * c], axis=-1)
    return out.reshape(x.shape).astype(x.dtype)


def neighbourhood_attention(q, k, v, rpb):
    b, s, h, dh = q.shape
    rows = s // GRID_W
    kh = min(NA_KH_MAX, rows)
    r = jnp.arange(rows)
    row_start = jnp.clip(r - kh // 2, 0, rows - kh)
    key_rows = row_start[:, None] + jnp.arange(kh)[None, :]
    c = jnp.arange(GRID_W)
    col_start = jnp.clip(c - NA_KW // 2, 0, GRID_W - NA_KW)
    col_valid = (c[None, :] >= col_start[:, None]) & (c[None, :] < col_start[:, None] + NA_KW)
    qg = q.reshape(b, rows, GRID_W, h, dh)
    kg = k.reshape(b, rows, GRID_W, h, dh)[:, key_rows]
    vg = v.reshape(b, rows, GRID_W, h, dh)[:, key_rows]
    scores = jnp.einsum('brqhd,brjkhd->bhrqjk', qg, kg).astype(jnp.float32) * (dh ** -0.5)
    row_off = key_rows - r[:, None] + (NA_KH_MAX - 1)
    col_off = jnp.clip(c[None, :] - c[:, None] + (NA_KW - 1), 0, 2 * NA_KW - 2)
    bias = rpb[:, row_off[:, None, :, None], col_off[None, :, None, :]]
    scores = scores + bias.astype(jnp.float32)
    scores = jnp.where(col_valid[:, None, :], scores, jnp.float32(-1e30))
    shp = scores.shape
    p = jax.nn.softmax(scores.reshape(b, h, rows, GRID_W, kh * GRID_W), axis=-1)
    p = p.reshape(shp).astype(v.dtype)
    out = jnp.einsum('bhrqjk,brjkhd->brqhd', p, vg)
    return out.reshape(b, s, h * dh)


def gqa_attention(q, k, v):
    b, s, hq, dh = q.shape
    hkv = k.shape[2]
    grp = hq // hkv
    nblk = s // Q_BLOCK
    qb = q.reshape(b, nblk, Q_BLOCK, hkv, grp, dh).transpose(1, 0, 2, 3, 4, 5)
    scale = dh ** -0.5

    def block(qi):
        sc = jnp.einsum('bqkgd,bskd->bkgqs', qi, k).astype(jnp.float32) * scale
        p = jax.nn.softmax(sc, axis=-1).astype(v.dtype)
        return jnp.einsum('bkgqs,bskd->bqkgd', p, v)

    out = lax.map(block, qb)
    return out.transpose(1, 0, 2, 3, 4, 5).reshape(b, s, hq * dh)


def spatial_gating(u, v, ln_g, ln_b, w_s, b_s):
    b, s, wc = u.shape
    vn = layer_norm(v, ln_g, ln_b)
    vc = vn.reshape(b, s // SG_CHUNK, SG_CHUNK, SG_GROUPS, wc // SG_GROUPS)
    mixed = jnp.einsum('gmn,bcngd->bcmgd', w_s, vc) + b_s.T[None, None, :, :, None]
    return u * mixed.reshape(b, s, wc)


def hybrid_layer(x, w_in, b_in, na_rpb, q_norm_g, k_norm_g, sg_ln_g, sg_ln_b, sg_w, sg_b,
                 w_br_a, w_br_b, w_br_c, w_out, b_out, ln_g, ln_b, rope_cos, rope_sin):
    b, s, d = x.shape
    hcat = x @ w_in + b_in
    (na_q, na_k, na_v, na_z, gq_q, gq_k, gq_v, gq_z,
     sg_u, sg_v, sg_z, gates) = jnp.split(hcat, _split_points(), axis=-1)

    y_a = neighbourhood_attention(na_q.reshape(b, s, NA_HEADS, HEAD_DIM),
                                  na_k.reshape(b, s, NA_HEADS, HEAD_DIM),
                                  na_v.reshape(b, s, NA_HEADS, HEAD_DIM), na_rpb)
    p_a = (y_a * jax.nn.silu(na_z)) @ w_br_a

    q = apply_rope(rms_norm(gq_q.reshape(b, s, GQA_HEADS, HEAD_DIM), q_norm_g), rope_cos, rope_sin)
    k = apply_rope(rms_norm(gq_k.reshape(b, s, GQA_KV_HEADS, HEAD_DIM), k_norm_g), rope_cos, rope_sin)
    y_b = gqa_attention(q, k, gq_v.reshape(b, s, GQA_KV_HEADS, HEAD_DIM))
    p_b = (y_b * jax.nn.silu(gq_z)) @ w_br_b

    y_c = spatial_gating(sg_u, sg_v, sg_ln_g, sg_ln_b, sg_w, sg_b)
    p_c = (y_c * jax.nn.silu(sg_z)) @ w_br_c

    g = jax.nn.sigmoid(gates.reshape(b, s, N_BRANCH, d))
    merged = g[:, :, 0] * p_a + g[:, :, 1] * p_b + g[:, :, 2] * p_c
    sub = merged @ w_out + b_out
    return layer_norm(DEEPNORM_ALPHA * x + sub, ln_g, ln_b)


def setup_inputs(seed: int = 0) -> dict:
    key = jax.random.key(seed)
    ks = jax.random.split(key, 20)
    L, D = DEPTH, D_MODEL

    def nrm(k, shape, scale):
        return jax.random.normal(k, shape, jnp.float32) * scale

    return {
        "x": nrm(ks[0], (BATCH, SEQ, D), 1.0),
        "ln_in_g": 1.0 + nrm(ks[1], (D,), 0.02),
        "ln_in_b": nrm(ks[2], (D,), 0.02),
        "w_in": nrm(ks[3], (L, D, IN_WIDTH), D ** -0.5),
        "b_in": nrm(ks[4], (L, IN_WIDTH), 0.02),
        "na_rpb": nrm(ks[5], (L, NA_HEADS, 2 * NA_KH_MAX - 1, 2 * NA_KW - 1), 0.1),
        "q_norm_g": 1.0 + nrm(ks[6], (L, HEAD_DIM), 0.02),
        "k_norm_g": 1.0 + nrm(ks[7], (L, HEAD_DIM), 0.02),
        "sg_ln_g": 1.0 + nrm(ks[8], (L, SG_WIDTH), 0.02),
        "sg_ln_b": nrm(ks[9], (L, SG_WIDTH), 0.02),
        "sg_w": nrm(ks[10], (L, SG_GROUPS, SG_CHUNK, SG_CHUNK), SG_CHUNK ** -0.5),
        "sg_b": 1.0 + nrm(ks[11], (L, SG_GROUPS, SG_CHUNK), 0.02),
        "w_br_a": nrm(ks[12], (L, NA_WIDTH, D), NA_WIDTH ** -0.5),
        "w_br_b": nrm(ks[13], (L, GQA_WIDTH, D), GQA_WIDTH ** -0.5),
        "w_br_c": nrm(ks[14], (L, SG_WIDTH, D), SG_WIDTH ** -0.5),
        "w_out": nrm(ks[15], (L, D, D), (D ** -0.5) * DEEPNORM_BETA),
        "b_out": nrm(ks[16], (L, D), 0.02),
        "ln_post_g": 1.0 + nrm(ks[17], (L, D), 0.02),
        "ln_post_b": nrm(ks[18], (L, D), 0.02),
    }


def reference(x, ln_in_g, ln_in_b, w_in, b_in, na_rpb, q_norm_g, k_norm_g, sg_ln_g, sg_ln_b,
              sg_w, sg_b, w_br_a, w_br_b, w_br_c, w_out, b_out, ln_post_g, ln_post_b):
    s = x.shape[1]
    rope_cos, rope_sin = axial_rope_tables(s)
    h = layer_norm(x, ln_in_g, ln_in_b)
    for l in range(DEPTH):
        h = hybrid_layer(h, w_in[l], b_in[l], na_rpb[l], q_norm_g[l], k_norm_g[l],
                         sg_ln_g[l], sg_ln_b[l], sg_w[l], sg_b[l],
                         w_br_a[l], w_br_b[l], w_br_c[l], w_out[l], b_out[l],
                         ln_post_g[l], ln_post_b[l], rope_cos, rope_sin)
    return h
```

```python
import functools

import numpy as np
import jax
import jax.numpy as jnp
from jax import lax
from jax.experimental import pallas as pl
from jax.experimental.pallas import tpu as pltpu

F32 = jnp.float32
BF16 = jnp.bfloat16

D_MODEL = 1024
BATCH = 8
SEQ = 2048
DEPTH = 2
GRID_W = 64
ROWS = SEQ // GRID_W
HEAD_DIM = 64
NA_HEADS = 8
NA_WIDTH = 512
NA_KH = 8
NA_KW = 16
GQA_HEADS = 8
GQA_KV_HEADS = 2
GQA_WIDTH = 512
GQA_KV_WIDTH = 128
ROPE_THETA = 10000.0
ROPE_AXIS_DIM = HEAD_DIM // 2
SG_WIDTH = 512
SG_GROUPS = 8
SG_CHUNK = 128
N_BRANCH = 3
LN_EPS = 1e-5
RMS_EPS = 1e-6
DEEPNORM_ALPHA = (2.0 * DEPTH) ** 0.25
ATTN_SCALE = HEAD_DIM ** -0.5
MASK_VALUE = -1e30

TOKENS = BATCH * SEQ
LANES = 128
VMEM_LIMIT = 56 * 1024 * 1024

_SPLITS = (NA_WIDTH,) * 4 + (GQA_WIDTH, GQA_KV_WIDTH, GQA_KV_WIDTH, GQA_WIDTH) + (SG_WIDTH,) * 3 + (N_BRANCH * D_MODEL,)
_OFFS = np.concatenate([[0], np.cumsum(_SPLITS)])
(O_NAQ, O_NAK, O_NAV, O_NAZ, O_GQQ, O_GQK, O_GQV, O_GQZ, O_SGU, O_SGV, O_SGZ, O_GATE, O_END) = [int(v) for v in _OFFS]

PROJ_TM = 512
NA_ROWS = 2
NA_TQ = NA_ROWS * GRID_W
NA_WIN_ROWS = NA_KH + 2
NA_WIN = NA_WIN_ROWS * GRID_W
NA_STEPS = ROWS // NA_ROWS
NA_TYPES = 5
GQA_TQ = 256
MERGE_TM = 256

_DPERM = np.concatenate([np.arange(0, HEAD_DIM, 2), np.arange(1, HEAD_DIM, 2)])
_GQ_HEAD_ORDER = np.array([p + 4 * half for p in range(4) for half in range(2)])
_GQQ_PERM = np.concatenate([h * HEAD_DIM + _DPERM for h in _GQ_HEAD_ORDER])
_GQK_PERM = np.concatenate([h * HEAD_DIM + _DPERM for h in range(GQA_KV_HEADS)])
_GQY_PERM = np.concatenate([h * HEAD_DIM + np.arange(HEAD_DIM) for h in _GQ_HEAD_ORDER])

P_NAQ, P_NAK, P_NAV, P_GQQ, P_GQK, P_GQV, P_SGV, P_END = 0, 512, 1024, 1536, 2048, 2176, 2304, 2816


def _cparams(n_axes):
    return pltpu.CompilerParams(dimension_semantics=("arbitrary",) * n_axes, vmem_limit_bytes=VMEM_LIMIT)


def _const_spec(shape):
    nd = len(shape)
    return pl.BlockSpec(shape, lambda *_: (0,) * nd, pipeline_mode=pl.Buffered(1))


def _layer_norm(x, g, b):
    mu = jnp.mean(x, axis=-1, keepdims=True)
    xc = x - mu
    var = jnp.mean(xc * xc, axis=-1, keepdims=True)
    return xc * lax.rsqrt(var + LN_EPS) * g + b


def _lane_is_first_head(shape):
    return lax.broadcasted_iota(jnp.int32, shape, len(shape) - 1) % LANES < HEAD_DIM


def _toeplitz_body(rpb_ref, out_ref):
    n = out_ref.shape[0]
    flat = lax.broadcasted_iota(jnp.int32, (1, GRID_W * GRID_W), 1)
    d = jnp.clip(flat % GRID_W - flat // GRID_W + (NA_KW - 1), 0, 2 * NA_KW - 2)
    acc = jnp.zeros((n, GRID_W * GRID_W), F32)
    for c in range(2 * NA_KW - 1):
        acc = jnp.where(d == c, rpb_ref[:, c:c + 1], acc)
    out_ref[...] = acc


def _toeplitz(rpb2d):
    n = rpb2d.shape[0]
    return pl.pallas_call(
        _toeplitz_body,
        out_shape=jax.ShapeDtypeStruct((n, GRID_W * GRID_W), F32),
        name="rpb_toeplitz",
    )(rpb2d)


def _na_window_start(step):
    return min(max(NA_ROWS * step - NA_KH // 2, 0), ROWS - NA_WIN_ROWS)


def _na_step_type(step):
    return min(step, 2) + max(step - (NA_STEPS - 3), 0)


def _na_bias_tables(tz):
    c = np.arange(GRID_W)
    col_start = np.clip(c - NA_KW // 2, 0, GRID_W - NA_KW)
    col_valid = (c[None, :] >= col_start[:, None]) & (c[None, :] < col_start[:, None] + NA_KW)
    masked = jnp.full((NA_HEADS, GRID_W, GRID_W), MASK_VALUE, F32)
    tables = []
    type_steps = {}
    for step in range(NA_STEPS):
        type_steps.setdefault(_na_step_type(step), step)
    for t in range(NA_TYPES):
        step = type_steps[t]
        w0 = _na_window_start(step)
        rows = []
        for dq in range(NA_ROWS):
            r = NA_ROWS * step + dq
            row_start = min(max(r - NA_KH // 2, 0), ROWS - NA_KH)
            tiles = []
            for j in range(NA_WIN_ROWS):
                key_row = w0 + j
                if row_start <= key_row < row_start + NA_KH:
                    ro = key_row - r + (NA_KH - 1)
                    tiles.append(jnp.where(col_valid[None], tz[:, ro], MASK_VALUE))
                else:
                    tiles.append(masked)
            rows.append(jnp.concatenate(tiles, axis=-1))
        tab = jnp.concatenate(rows, axis=1)
        tables.append(tab.reshape(NA_HEADS // 2, 2 * NA_TQ, NA_WIN))
    return jnp.stack(tables)


def _proj_body(first, *refs):
    if first:
        (x_ref, lng_ref, lnb_ref, w_ref, b_ref, gq_ref, gk_ref, bd_ref, cos_ref, s1_ref, s2_ref,
         sgg_ref, sgb_ref, h_ref, naq_ref, nakt_ref, nav_ref, gqq_ref, gqkt_ref, gqv_ref, vn_ref) = refs
    else:
        (x_ref, w_ref, b_ref, gq_ref, gk_ref, bd_ref, cos_ref, s1_ref, s2_ref,
         sgg_ref, sgb_ref, naq_ref, nakt_ref, nav_ref, gqq_ref, gqkt_ref, gqv_ref, vn_ref) = refs
    x = x_ref[...]
    if first:
        x = _layer_norm(x, lng_ref[...], lnb_ref[...])
        h_ref[...] = x
    xb = x.astype(BF16)

    def proj(a, b):
        return jnp.dot(xb, w_ref[:, a:b], preferred_element_type=F32) + b_ref[:, a:b]

    def store_head_masked(dst_ref, val, col):
        first_head = _lane_is_first_head(val.shape)
        dst_ref[:, col:col + LANES] = jnp.where(first_head, val, 0.0).astype(BF16)
        dst_ref[:, NA_WIDTH + col:NA_WIDTH + col + LANES] = jnp.where(first_head, 0.0, val).astype(BF16)

    def rms_rope(val, gain):
        sq = val * val
        hi = sq.astype(BF16)
        lo = (sq - hi.astype(F32)).astype(BF16)
        ssum = (jnp.dot(hi, bd_ref[...], preferred_element_type=F32)
                + jnp.dot(lo, bd_ref[...], preferred_element_type=F32))
        xn = val * lax.rsqrt(ssum * (1.0 / HEAD_DIM) + RMS_EPS) * gain
        return (xn * cos_ref[...] + pltpu.roll(xn, LANES - ROPE_AXIS_DIM, 1) * s1_ref[...]
                + pltpu.roll(xn, ROPE_AXIS_DIM, 1) * s2_ref[...])

    q = proj(P_NAQ, P_NAQ + NA_WIDTH)
    for p in range(NA_WIDTH // LANES):
        store_head_masked(naq_ref, q[:, p * LANES:(p + 1) * LANES], p * LANES)
    k = proj(P_NAK, P_NAK + NA_WIDTH)
    for c in range(PROJ_TM // LANES):
        nakt_ref[0, c] = k[c * LANES:(c + 1) * LANES, :].T.astype(BF16)
    nav_ref[...] = proj(P_NAV, P_NAV + NA_WIDTH).astype(BF16)

    q = proj(P_GQQ, P_GQQ + GQA_WIDTH)
    for p in range(GQA_WIDTH // LANES):
        store_head_masked(gqq_ref, rms_rope(q[:, p * LANES:(p + 1) * LANES], gq_ref[...]), p * LANES)
    k = rms_rope(proj(P_GQK, P_GQK + GQA_KV_WIDTH), gk_ref[...])
    gqkt_ref[0] = k.T.astype(BF16)
    gqv_ref[...] = proj(P_GQV, P_GQV + GQA_KV_WIDTH).astype(BF16)

    vn_ref[...] = _layer_norm(proj(P_SGV, P_SGV + SG_WIDTH), sgg_ref[...], sgb_ref[...]).astype(BF16)


def _proj(first, x, ln_g, ln_b, w, b, gq, gk, bd, cos_t, s1_t, s2_t, sgg, sgb):
    tm = PROJ_TM
    tiles_per_seq = SEQ // tm
    row = lambda i: (i, 0)
    pos = lambda i: (i % tiles_per_seq, 0)
    in_specs = [pl.BlockSpec((tm, D_MODEL), row)]
    args = [x]
    if first:
        in_specs += [_const_spec((1, D_MODEL)), _const_spec((1, D_MODEL))]
        args += [ln_g, ln_b]
    in_specs += [_const_spec((D_MODEL, P_END)), _const_spec((1, P_END)), _const_spec((1, LANES)),
                 _const_spec((1, LANES)), _const_spec((LANES, LANES)),
                 pl.BlockSpec((tm, LANES), pos), pl.BlockSpec((tm, LANES), pos), pl.BlockSpec((tm, LANES), pos),
                 _const_spec((1, SG_WIDTH)), _const_spec((1, SG_WIDTH))]
    args += [w, b, gq, gk, bd, cos_t, s1_t, s2_t, sgg, sgb]
    out_shape, out_specs = [], []
    if first:
        out_shape.append(jax.ShapeDtypeStruct((TOKENS, D_MODEL), F32))
        out_specs.append(pl.BlockSpec((tm, D_MODEL), row))
    out_shape += [
        jax.ShapeDtypeStruct((TOKENS, 2 * NA_WIDTH), BF16),
        jax.ShapeDtypeStruct((BATCH, SEQ // LANES, NA_WIDTH, LANES), BF16),
        jax.ShapeDtypeStruct((TOKENS, NA_WIDTH), BF16),
        jax.ShapeDtypeStruct((TOKENS, 2 * GQA_WIDTH), BF16),
        jax.ShapeDtypeStruct((BATCH, GQA_KV_WIDTH, SEQ), BF16),
        jax.ShapeDtypeStruct((TOKENS, GQA_KV_WIDTH), BF16),
        jax.ShapeDtypeStruct((TOKENS, SG_WIDTH), BF16),
    ]
    out_specs += [
        pl.BlockSpec((tm, 2 * NA_WIDTH), row),
        pl.BlockSpec((1, tm // LANES, NA_WIDTH, LANES), lambda i: (i // tiles_per_seq, i % tiles_per_seq, 0, 0)),
        pl.BlockSpec((tm, NA_WIDTH), row),
        pl.BlockSpec((tm, 2 * GQA_WIDTH), row),
        pl.BlockSpec((1, GQA_KV_WIDTH, tm), lambda i: (i // tiles_per_seq, 0, i % tiles_per_seq)),
        pl.BlockSpec((tm, GQA_KV_WIDTH), row),
        pl.BlockSpec((tm, SG_WIDTH), row),
    ]
    return pl.pallas_call(
        functools.partial(_proj_body, first),
        grid=(TOKENS // tm,),
        in_specs=in_specs,
        out_specs=out_specs,
        out_shape=out_shape,
        compiler_params=_cparams(1),
        name="proj_first" if first else "proj",
    )(*args)


def _silu_gate(xb, wz_ref, bz_ref, col):
    z = jnp.dot(xb, wz_ref[:, col:col + LANES], preferred_element_type=F32) + bz_ref[:, col:col + LANES]
    return z * jax.nn.sigmoid(z)


def _pair_softmax_pv(s, v, n_q):
    m = jnp.max(s, axis=-1, keepdims=True)
    e = jnp.exp(s - m)
    l = jnp.sum(e, axis=-1, keepdims=True)
    o = jnp.dot(e.astype(BF16), v, preferred_element_type=F32) / l
    return jnp.where(_lane_is_first_head((n_q, LANES)), o[:n_q], o[n_q:])


def _na_body(q_ref, kt_ref, v_ref, bias_ref, x_ref, wz_ref, bz_ref, out_ref):
    step = pl.program_id(1)
    w0 = jnp.clip(NA_ROWS * step - NA_KH // 2, 0, ROWS - NA_WIN_ROWS)
    blk0 = w0 // 2
    tok0 = pl.multiple_of(w0 * GRID_W, LANES)
    xb = x_ref[...].astype(BF16)
    for p in range(NA_WIDTH // LANES):
        col = p * LANES
        qs = jnp.concatenate([q_ref[:, col:col + LANES], q_ref[:, NA_WIDTH + col:NA_WIDTH + col + LANES]], axis=0)
        kwin = jnp.concatenate([kt_ref[0, blk0 + i, col:col + LANES, :] for i in range(NA_WIN // LANES)], axis=1)
        s = jnp.dot(qs, kwin, preferred_element_type=F32) + bias_ref[0, p]
        y = _pair_softmax_pv(s, v_ref[pl.ds(tok0, NA_WIN), col:col + LANES], NA_TQ)
        out_ref[:, col:col + LANES] = (y * _silu_gate(xb, wz_ref, bz_ref, col)).astype(BF16)


def _na(q2, kt, v, bias, x, wz, bz):
    def bias_idx(b, s):
        return (jnp.minimum(s, 2) + jnp.maximum(s - (NA_STEPS - 3), 0), 0, 0, 0)

    row = lambda b, s: (b * NA_STEPS + s, 0)
    return pl.pallas_call(
        _na_body,
        grid=(BATCH, NA_STEPS),
        in_specs=[
            pl.BlockSpec((NA_TQ, 2 * NA_WIDTH), row),
            pl.BlockSpec((1, SEQ // LANES, NA_WIDTH, LANES), lambda b, s: (b, 0, 0, 0)),
            pl.BlockSpec((SEQ, NA_WIDTH), lambda b, s: (b, 0)),
            pl.BlockSpec((1, NA_HEADS // 2, 2 * NA_TQ, NA_WIN), bias_idx),
            pl.BlockSpec((NA_TQ, D_MODEL), row),
            _const_spec((D_MODEL, NA_WIDTH)),
            _const_spec((1, NA_WIDTH)),
        ],
        out_specs=pl.BlockSpec((NA_TQ, NA_WIDTH), row),
        out_shape=jax.ShapeDtypeStruct((TOKENS, NA_WIDTH), BF16),
        compiler_params=_cparams(2),
        name="na",
    )(q2, kt, v, bias, x, wz, bz)


def _gqa_body(q_ref, kt_ref, v_ref, x_ref, wz_ref, bz_ref, out_ref):
    xb = x_ref[...].astype(BF16)
    for p in range(GQA_WIDTH // LANES):
        col = p * LANES
        qs = jnp.concatenate([q_ref[:, col:col + LANES], q_ref[:, GQA_WIDTH + col:GQA_WIDTH + col + LANES]], axis=0)
        s = jnp.dot(qs, kt_ref[0], preferred_element_type=F32)
        y = _pair_softmax_pv(s, v_ref[...], GQA_TQ)
        out_ref[:, col:col + LANES] = (y * _silu_gate(xb, wz_ref, bz_ref, col)).astype(BF16)


def _gqa(q2, kt, v, x, wz, bz):
    steps = SEQ // GQA_TQ
    row = lambda b, s: (b * steps + s, 0)
    return pl.pallas_call(
        _gqa_body,
        grid=(BATCH, steps),
        in_specs=[
            pl.BlockSpec((GQA_TQ, 2 * GQA_WIDTH), row),
            pl.BlockSpec((1, GQA_KV_WIDTH, SEQ), lambda b, s: (b, 0, 0)),
            pl.BlockSpec((SEQ, GQA_KV_WIDTH), lambda b, s: (b, 0)),
            pl.BlockSpec((GQA_TQ, D_MODEL), row),
            _const_spec((D_MODEL, GQA_WIDTH)),
            _const_spec((1, GQA_WIDTH)),
        ],
        out_specs=pl.BlockSpec((GQA_TQ, GQA_WIDTH), row),
        out_shape=jax.ShapeDtypeStruct((TOKENS, GQA_WIDTH), BF16),
        compiler_params=_cparams(2),
        name="gqa",
    )(q2, kt, v, x, wz, bz)


def _merge_body(x_ref, ya_ref, yb_ref, vn_ref, wu_ref, bu_ref, wz_ref, bz_ref, ws_ref, bs_ref,
                wg_ref, bg_ref, wa_ref, wb_ref, wc_ref, wo_ref, bo_ref, lng_ref, lnb_ref, out_ref, mixed_ref):
    tm = MERGE_TM
    n_chunks = tm // SG_CHUNK
    x = x_ref[...]
    xb = x.astype(BF16)

    first_head = _lane_is_first_head((SG_CHUNK, LANES))
    for pp in range(SG_WIDTH // LANES):
        col = pp * LANES
        rhs = jnp.concatenate([vn_ref[c * SG_CHUNK:(c + 1) * SG_CHUNK, col:col + LANES] for c in range(n_chunks)], axis=1)
        res = jnp.dot(ws_ref[pp], rhs, preferred_element_type=F32)
        for c in range(n_chunks):
            blk = jnp.where(first_head, res[:SG_CHUNK, c * LANES:(c + 1) * LANES], res[SG_CHUNK:, c * LANES:(c + 1) * LANES])
            mixed_ref[c * SG_CHUNK:(c + 1) * SG_CHUNK, col:col + LANES] = blk + bs_ref[:, col:col + LANES]
    u = jnp.dot(xb, wu_ref[...], preferred_element_type=F32) + bu_ref[...]
    z = jnp.dot(xb, wz_ref[...], preferred_element_type=F32) + bz_ref[...]
    tc = (u * mixed_ref[...] * (z * jax.nn.sigmoid(z))).astype(BF16)

    def gate(i):
        lo = i * D_MODEL
        return jax.nn.sigmoid(jnp.dot(xb, wg_ref[:, lo:lo + D_MODEL], preferred_element_type=F32) + bg_ref[:, lo:lo + D_MODEL])

    merged = gate(0) * jnp.dot(ya_ref[...], wa_ref[...], preferred_element_type=F32)
    merged = merged + gate(1) * jnp.dot(yb_ref[...], wb_ref[...], preferred_element_type=F32)
    merged = merged + gate(2) * jnp.dot(tc, wc_ref[...], preferred_element_type=F32)
    sub = jnp.dot(merged.astype(BF16), wo_ref[...], preferred_element_type=F32) + bo_ref[...]
    out_ref[...] = _layer_norm(DEEPNORM_ALPHA * x + sub, lng_ref[...], lnb_ref[...])


def _merge(x, ya, yb, vn, wu, bu, wz, bz, ws, bs, wg, bg, wa, wb, wc, wo, bo, lng, lnb):
    tm = MERGE_TM
    row = lambda i: (i, 0)
    return pl.pallas_call(
        _merge_body,
        grid=(TOKENS // tm,),
        in_specs=[
            pl.BlockSpec((tm, D_MODEL), row),
            pl.BlockSpec((tm, NA_WIDTH), row),
            pl.BlockSpec((tm, GQA_WIDTH), row),
            pl.BlockSpec((tm, SG_WIDTH), row),
            _const_spec((D_MODEL, SG_WIDTH)), _const_spec((1, SG_WIDTH)),
            _const_spec((D_MODEL, SG_WIDTH)), _const_spec((1, SG_WIDTH)),
            _const_spec((SG_GROUPS // 2, 2 * SG_CHUNK, SG_CHUNK)), _const_spec((SG_CHUNK, SG_WIDTH)),
            _const_spec((D_MODEL, N_BRANCH * D_MODEL)), _const_spec((1, N_BRANCH * D_MODEL)),
            _const_spec((NA_WIDTH, D_MODEL)), _const_spec((GQA_WIDTH, D_MODEL)), _const_spec((SG_WIDTH, D_MODEL)),
            _const_spec((D_MODEL, D_MODEL)), _const_spec((1, D_MODEL)),
            _const_spec((1, D_MODEL)), _const_spec((1, D_MODEL)),
        ],
        out_specs=pl.BlockSpec((tm, D_MODEL), row),
        out_shape=jax.ShapeDtypeStruct((TOKENS, D_MODEL), F32),
        scratch_shapes=[pltpu.VMEM((tm, SG_WIDTH), F32)],
        compiler_params=_cparams(1),
        name="merge",
    )(x, ya, yb, vn, wu, bu, wz, bz, ws, bs, wg, bg, wa, wb, wc, wo, bo, lng, lnb)


def _rope_tables():
    t = jnp.arange(SEQ)
    row = (t // GRID_W).astype(F32)
    col = (t % GRID_W).astype(F32)
    freqs = ROPE_THETA ** (-jnp.arange(0, ROPE_AXIS_DIM, 2, dtype=F32) / ROPE_AXIS_DIM)
    ang = jnp.concatenate([row[:, None] * freqs, col[:, None] * freqs], axis=-1)
    cos, sin = jnp.cos(ang), jnp.sin(ang)
    zero = jnp.zeros_like(sin)
    cos_t = jnp.tile(jnp.concatenate([cos, cos], axis=-1), (1, 2))
    s1_t = jnp.tile(jnp.concatenate([-sin, zero], axis=-1), (1, 2))
    s2_t = jnp.tile(jnp.concatenate([zero, sin], axis=-1), (1, 2))
    return cos_t, s1_t, s2_t


def kernel(x, ln_in_g, ln_in_b, w_in, b_in, na_rpb, q_norm_g, k_norm_g, sg_ln_g, sg_ln_b, sg_w, sg_b,
           w_br_a, w_br_b, w_br_c, w_out, b_out, ln_post_g, ln_post_b):
    assert x.shape == (BATCH, SEQ, D_MODEL) and w_in.shape == (DEPTH, D_MODEL, O_END)
    cos_t, s1_t, s2_t = _rope_tables()
    bd = jnp.asarray(np.kron(np.eye(2), np.ones((HEAD_DIM, HEAD_DIM))), BF16)
    tz = _toeplitz(na_rpb.reshape(DEPTH * NA_HEADS * (2 * NA_KH - 1), 2 * NA_KW - 1))
    tz = tz.reshape(DEPTH, NA_HEADS, 2 * NA_KH - 1, GRID_W, GRID_W)

    row2 = lambda v: v.reshape(1, -1)
    h = x.reshape(TOKENS, D_MODEL)
    for l in range(DEPTH):
        w, b = w_in[l], b_in[l]
        sl = lambda a, e: (w[:, a:e], b[a:e])
        naq_w, naq_b = sl(O_NAQ, O_NAK)
        gqq_w, gqq_b = sl(O_GQQ, O_GQK)
        gqk_w, gqk_b = sl(O_GQK, O_GQV)
        cols = [(naq_w * ATTN_SCALE, naq_b * ATTN_SCALE), sl(O_NAK, O_NAV), sl(O_NAV, O_NAZ),
                (gqq_w[:, _GQQ_PERM], gqq_b[_GQQ_PERM]), (gqk_w[:, _GQK_PERM], gqk_b[_GQK_PERM]),
                sl(O_GQV, O_GQZ), sl(O_SGV, O_SGZ)]
        w1 = jnp.concatenate([c[0] for c in cols], axis=1).astype(BF16)
        b1 = row2(jnp.concatenate([c[1] for c in cols]))
        gq = row2(jnp.tile(q_norm_g[l][_DPERM] * ATTN_SCALE, 2))
        gk = row2(jnp.tile(k_norm_g[l][_DPERM], 2))
        proj_args = (w1, b1, gq, gk, bd, cos_t, s1_t, s2_t, row2(sg_ln_g[l]), row2(sg_ln_b[l]))
        if l == 0:
            h, naq, nakt, nav, gqq, gqkt, gqv, vn = _proj(True, h, row2(ln_in_g), row2(ln_in_b), *proj_args)
        else:
            naq, nakt, nav, gqq, gqkt, gqv, vn = _proj(False, h, None, None, *proj_args)

        naz_w, naz_b = sl(O_NAZ, O_GQQ)
        ya = _na(naq, nakt, nav, _na_bias_tables(tz[l]), h, naz_w.astype(BF16), row2(naz_b))

        gqz_w, gqz_b = sl(O_GQZ, O_SGU)
        yb = _gqa(gqq, gqkt, gqv, h, gqz_w[:, _GQY_PERM].astype(BF16), row2(gqz_b[_GQY_PERM]))

        sgu_w, sgu_b = sl(O_SGU, O_SGV)
        sgz_w, sgz_b = sl(O_SGZ, O_GATE)
        gate_w, gate_b = sl(O_GATE, O_END)
        ws = sg_w[l].reshape(SG_GROUPS // 2, 2 * SG_CHUNK, SG_CHUNK).astype(BF16)
        bs = jnp.repeat(sg_b[l].T, HEAD_DIM, axis=1)
        h = _merge(h, ya, yb, vn, sgu_w.astype(BF16), row2(sgu_b), sgz_w.astype(BF16), row2(sgz_b), ws, bs,
                   gate_w.astype(BF16), row2(gate_b), w_br_a[l].astype(BF16), w_br_b[l][_GQY_PERM].astype(BF16),
                   w_br_c[l].astype(BF16), w_out[l].astype(BF16), row2(b_out[l]),
                   row2(ln_post_g[l]), row2(ln_post_b[l]))
    return h.reshape(BATCH, SEQ, D_MODEL)
```

```python
import functools

import numpy as np
import jax
import jax.numpy as jnp
from jax import lax
from jax.experimental import pallas as pl
from jax.experimental.pallas import tpu as pltpu

F32 = jnp.float32
BF16 = jnp.bfloat16

D_MODEL = 1024
BATCH = 8
SEQ = 2048
DEPTH = 2
GRID_W = 64
ROWS = SEQ // GRID_W
HEAD_DIM = 64
NA_HEADS = 8
NA_WIDTH = 512
NA_KH = 8
NA_KW = 16
RPB_ROWS = 2 * NA_KH - 1
RPB_COLS = 2 * NA_KW - 1
GQA_HEADS = 8
GQA_KV_HEADS = 2
GQA_WIDTH = 512
GQA_KV_WIDTH = 128
ROPE_THETA = 10000.0
ROPE_AXIS_DIM = HEAD_DIM // 2
SG_WIDTH = 512
SG_GROUPS = 8
SG_CHUNK = 128
N_BRANCH = 3
LN_EPS = 1e-5
RMS_EPS = 1e-6
DEEPNORM_ALPHA = (2.0 * DEPTH) ** 0.25
ATTN_SCALE = HEAD_DIM ** -0.5
MASK_VALUE = -1e30

TOKENS = BATCH * SEQ
LANES = 128
VMEM_LIMIT = 56 * 1024 * 1024

_SPLITS = (NA_WIDTH,) * 4 + (GQA_WIDTH, GQA_KV_WIDTH, GQA_KV_WIDTH, GQA_WIDTH) + (SG_WIDTH,) * 3 + (N_BRANCH * D_MODEL,)
_OFFS = [int(v) for v in np.concatenate([[0], np.cumsum(_SPLITS)])]
(O_NAQ, O_NAK, O_NAV, O_NAZ, O_GQQ, O_GQK, O_GQV, O_GQZ, O_SGU, O_SGV, O_SGZ, O_GATE, O_END) = _OFFS

W_PROJ = 3072
W_Z = 512
_W_SEGMENTS = ((O_NAQ, O_NAZ), (O_GQQ, O_GQZ), (O_SGV, O_SGZ), None, (O_GATE, O_END),
               (O_NAZ, O_GQQ), (O_GQZ, O_SGU), (O_SGU, O_SGV), (O_SGZ, O_GATE))
W_PAD = 256
W_TOTAL = 8192
GATE_BLOCK = 1
NAZ_BLOCK, GQZ_BLOCK, SGU_BLOCK, SGZ_BLOCK = 12, 13, 14, 15
P_NAQ, P_NAK, P_NAV, P_GQQ, P_GQK, P_GQV, P_SGV = 0, 512, 1024, 1536, 2048, 2176, 2304

PROJ_TM = 512
NA_ROWS = 4
NA_TQ = NA_ROWS * GRID_W
NA_WIN_ROWS = 12
NA_WIN = NA_WIN_ROWS * GRID_W
NA_STEPS = ROWS // NA_ROWS
NA_TYPES = 3
GQA_TQ = 256
MERGE_TM = 256


def _cparams(n_axes):
    return pltpu.CompilerParams(dimension_semantics=("arbitrary",) * n_axes, vmem_limit_bytes=VMEM_LIMIT)


def _fixed_spec(block_shape, index):
    return pl.BlockSpec(block_shape, lambda *_: index, pipeline_mode=pl.Buffered(1))


def _layer_norm(x, g, b):
    mu = jnp.mean(x, axis=-1, keepdims=True)
    xc = x - mu
    var = jnp.mean(xc * xc, axis=-1, keepdims=True)
    return xc * lax.rsqrt(var + LN_EPS) * g + b


def _lane_is_first_head(shape):
    return lax.broadcasted_iota(jnp.int32, shape, len(shape) - 1) % LANES < HEAD_DIM


def _na_window_start(step):
    return min(max(NA_ROWS * step - NA_KH // 2, 0), ROWS - NA_WIN_ROWS)


_NA_TYPE_STEPS = (0, 1, NA_STEPS - 1)


def _na_tile_plan():
    plan = {}
    for t, step in enumerate(_NA_TYPE_STEPS):
        w0 = _na_window_start(step)
        for dq in range(NA_ROWS):
            r = NA_ROWS * step + dq
            row_start = min(max(r - NA_KH // 2, 0), ROWS - NA_KH)
            for j in range(NA_WIN_ROWS):
                key_row = w0 + j
                inside = row_start <= key_row < row_start + NA_KH
                plan[(t, dq, j)] = key_row - r + (NA_KH - 1) if inside else None
    return plan


def _na_bias_body(rpb_ref, out_ref):
    layer, pair = pl.program_id(0), pl.program_id(1)
    qi = lax.broadcasted_iota(jnp.int32, (GRID_W, GRID_W), 0)
    ki = lax.broadcasted_iota(jnp.int32, (GRID_W, GRID_W), 1)
    d = jnp.clip(ki - qi + (NA_KW - 1), 0, RPB_COLS - 1)
    col_start = jnp.clip(qi - NA_KW // 2, 0, GRID_W - NA_KW)
    col_valid = (ki >= col_start) & (ki < col_start + NA_KW)
    masked = jnp.full((GRID_W, GRID_W), MASK_VALUE, F32)
    plan = _na_tile_plan()
    for hsel in range(2):
        base = ((layer * NA_HEADS + 2 * pair + hsel) * RPB_ROWS) * RPB_COLS
        for ro in range(RPB_ROWS):
            users = [key for key, val in plan.items() if val == ro]
            if not users:
                continue
            tile = jnp.zeros((GRID_W, GRID_W), F32)
            for c in range(RPB_COLS):
                tile = jnp.where(d == c, rpb_ref[base + ro * RPB_COLS + c], tile)
            tile = jnp.where(col_valid, tile, MASK_VALUE)
            for (t, dq, j) in users:
                row0 = hsel * NA_TQ + dq * GRID_W
                out_ref[0, t, 0, row0:row0 + GRID_W, j * GRID_W:(j + 1) * GRID_W] = tile
        for (t, dq, j), val in plan.items():
            if val is None:
                row0 = hsel * NA_TQ + dq * GRID_W
                out_ref[0, t, 0, row0:row0 + GRID_W, j * GRID_W:(j + 1) * GRID_W] = masked


def _na_bias(rpb_flat):
    return pl.pallas_call(
        _na_bias_body,
        grid=(DEPTH, NA_HEADS // 2),
        in_specs=[pl.BlockSpec(memory_space=pltpu.SMEM)],
        out_specs=pl.BlockSpec((1, NA_TYPES, 1, 2 * NA_TQ, NA_WIN), lambda l, p: (l, 0, p, 0, 0)),
        out_shape=jax.ShapeDtypeStruct((DEPTH, NA_TYPES, NA_HEADS // 2, 2 * NA_TQ, NA_WIN), F32),
        compiler_params=_cparams(2),
        name="na_bias",
    )(rpb_flat)


def _proj_body(first, *refs):
    if first:
        (x_ref, lng_ref, lnb_ref, w_ref, b_ref, gq_ref, gk_ref, bd_ref, cos_ref, s1_ref, s2_ref,
         sgg_ref, sgb_ref, h_ref, naq_ref, nak_ref, nav_ref, gqq_ref, gqkt_ref, gqv_ref, vn_ref) = refs
    else:
        (x_ref, w_ref, b_ref, gq_ref, gk_ref, bd_ref, cos_ref, s1_ref, s2_ref,
         sgg_ref, sgb_ref, naq_ref, nak_ref, nav_ref, gqq_ref, gqkt_ref, gqv_ref, vn_ref) = refs
    x = x_ref[...]
    if first:
        x = _layer_norm(x, lng_ref[...], lnb_ref[...])
        h_ref[...] = x
    xb = x.astype(BF16)

    def proj(a, b):
        return jnp.dot(xb, w_ref[:, a:b], preferred_element_type=F32) + b_ref[:, a:b]

    def store_head_masked(dst_ref, val, col):
        first_head = _lane_is_first_head(val.shape)
        dst_ref[:, col:col + LANES] = jnp.where(first_head, val, 0.0).astype(BF16)
        dst_ref[:, NA_WIDTH + col:NA_WIDTH + col + LANES] = jnp.where(first_head, 0.0, val).astype(BF16)

    def rms_rope(val, gain):
        sq = val * val
        hi = sq.astype(BF16)
        lo = (sq - hi.astype(F32)).astype(BF16)
        ssum = (jnp.dot(hi, bd_ref[...], preferred_element_type=F32)
                + jnp.dot(lo, bd_ref[...], preferred_element_type=F32))
        xn = val * lax.rsqrt(ssum * (1.0 / HEAD_DIM) + RMS_EPS) * gain
        return (xn * cos_ref[...] + pltpu.roll(xn, LANES - 1, 1) * s1_ref[...]
                + pltpu.roll(xn, 1, 1) * s2_ref[...])

    q = proj(P_NAQ, P_NAQ + NA_WIDTH) * ATTN_SCALE
    for p in range(NA_WIDTH // LANES):
        store_head_masked(naq_ref, q[:, p * LANES:(p + 1) * LANES], p * LANES)
    nak_ref[...] = proj(P_NAK, P_NAK + NA_WIDTH).astype(BF16)
    nav_ref[...] = proj(P_NAV, P_NAV + NA_WIDTH).astype(BF16)

    q = proj(P_GQQ, P_GQQ + GQA_WIDTH)
    for p in range(GQA_WIDTH // LANES):
        store_head_masked(gqq_ref, rms_rope(q[:, p * LANES:(p + 1) * LANES], gq_ref[...]), p * LANES)
    k = rms_rope(proj(P_GQK, P_GQK + GQA_KV_WIDTH), gk_ref[...])
    v = proj(P_GQV, P_GQV + GQA_KV_WIDTH)
    first_head = _lane_is_first_head(k.shape)
    k_sw = pltpu.roll(k, HEAD_DIM, 1)
    v_sw = pltpu.roll(v, HEAD_DIM, 1)
    gqkt_ref[0, 0] = jnp.where(first_head, k, k_sw).T.astype(BF16)
    gqkt_ref[0, 1] = jnp.where(first_head, k_sw, k).T.astype(BF16)
    gqv_ref[:, :LANES] = jnp.where(first_head, v, v_sw).astype(BF16)
    gqv_ref[:, LANES:] = jnp.where(first_head, v_sw, v).astype(BF16)

    vn_ref[...] = _layer_norm(proj(P_SGV, P_SGV + SG_WIDTH), sgg_ref[...], sgb_ref[...]).astype(BF16)


def _proj(layer, first, x, ln_g, ln_b, wb, bb, gq, gk, bd, cos_t, s1_t, s2_t, sgg, sgb):
    tm = PROJ_TM
    tiles_per_seq = SEQ // tm
    row = lambda i: (i, 0)
    pos = lambda i: (i % tiles_per_seq, 0)
    in_specs = [pl.BlockSpec((tm, D_MODEL), row)]
    args = [x]
    if first:
        in_specs += [_fixed_spec((1, D_MODEL), (0, 0)), _fixed_spec((1, D_MODEL), (0, 0))]
        args += [ln_g, ln_b]
    in_specs += [_fixed_spec((None, D_MODEL, W_PROJ), (layer, 0, 0)), _fixed_spec((None, 1, W_PROJ), (layer, 0, 0)),
                 _fixed_spec((None, 1, LANES), (layer, 0, 0)), _fixed_spec((None, 1, LANES), (layer, 0, 0)),
                 _fixed_spec((LANES, LANES), (0, 0)),
                 pl.BlockSpec((tm, LANES), pos), pl.BlockSpec((tm, LANES), pos), pl.BlockSpec((tm, LANES), pos),
                 _fixed_spec((None, 1, SG_WIDTH), (layer, 0, 0)), _fixed_spec((None, 1, SG_WIDTH), (layer, 0, 0))]
    args += [wb, bb, gq, gk, bd, cos_t, s1_t, s2_t, sgg, sgb]
    out_shape, out_specs = [], []
    if first:
        out_shape.append(jax.ShapeDtypeStruct((TOKENS, D_MODEL), F32))
        out_specs.append(pl.BlockSpec((tm, D_MODEL), row))
    out_shape += [
        jax.ShapeDtypeStruct((TOKENS, 2 * NA_WIDTH), BF16),
        jax.ShapeDtypeStruct((TOKENS, NA_WIDTH), BF16),
        jax.ShapeDtypeStruct((TOKENS, NA_WIDTH), BF16),
        jax.ShapeDtypeStruct((TOKENS, 2 * GQA_WIDTH), BF16),
        jax.ShapeDtypeStruct((BATCH, GQA_KV_HEADS, LANES, SEQ), BF16),
        jax.ShapeDtypeStruct((TOKENS, GQA_KV_HEADS * LANES), BF16),
        jax.ShapeDtypeStruct((TOKENS, SG_WIDTH), BF16),
    ]
    out_specs += [
        pl.BlockSpec((tm, 2 * NA_WIDTH), row),
        pl.BlockSpec((tm, NA_WIDTH), row),
        pl.BlockSpec((tm, NA_WIDTH), row),
        pl.BlockSpec((tm, 2 * GQA_WIDTH), row),
        pl.BlockSpec((1, GQA_KV_HEADS, LANES, tm), lambda i: (i // tiles_per_seq, 0, 0, i % tiles_per_seq)),
        pl.BlockSpec((tm, GQA_KV_HEADS * LANES), row),
        pl.BlockSpec((tm, SG_WIDTH), row),
    ]
    return pl.pallas_call(
        functools.partial(_proj_body, first),
        grid=(TOKENS // tm,),
        in_specs=in_specs,
        out_specs=out_specs,
        out_shape=out_shape,
        compiler_params=_cparams(1),
        name="proj_first" if first else "proj",
    )(*args)


def _silu_gate(xb, wz_ref, bz_ref, col):
    z = jnp.dot(xb, wz_ref[:, col:col + LANES], preferred_element_type=F32) + bz_ref[:, col:col + LANES]
    return z * jax.nn.sigmoid(z)


def _pair_softmax_pv(s, v, n_q):
    m = jnp.max(s, axis=-1, keepdims=True)
    e = jnp.exp(s - m)
    l = jnp.sum(e, axis=-1, keepdims=True)
    o = jnp.dot(e.astype(BF16), v, preferred_element_type=F32) / l
    return jnp.where(_lane_is_first_head((n_q, LANES)), o[:n_q], o[n_q:])


def _na_body(q_ref, k_ref, v_ref, bias_ref, x_ref, wz_ref, bz_ref, out_ref):
    step = pl.program_id(1)
    w0 = jnp.clip(NA_ROWS * step - NA_KH // 2, 0, ROWS - NA_WIN_ROWS)
    tok0 = pl.multiple_of(w0 * GRID_W, 2 * LANES)
    xb = x_ref[...].astype(BF16)
    for p in range(NA_WIDTH // LANES):
        col = p * LANES
        qs = jnp.concatenate([q_ref[:, col:col + LANES], q_ref[:, NA_WIDTH + col:NA_WIDTH + col + LANES]], axis=0)
        kwin = k_ref[pl.ds(tok0, NA_WIN), col:col + LANES]
        s = lax.dot_general(qs, kwin, (((1,), (1,)), ((), ())), preferred_element_type=F32) + bias_ref[0, 0, p]
        y = _pair_softmax_pv(s, v_ref[pl.ds(tok0, NA_WIN), col:col + LANES], NA_TQ)
        out_ref[:, col:col + LANES] = (y * _silu_gate(xb, wz_ref, bz_ref, col)).astype(BF16)


def _na(layer, q2, k, v, bias, x, wb, bb):
    def bias_idx(b, s):
        return (layer, jnp.minimum(s, 1) + jnp.maximum(s - (NA_STEPS - 2), 0), 0, 0, 0)

    row = lambda b, s: (b * NA_STEPS + s, 0)
    return pl.pallas_call(
        _na_body,
        grid=(BATCH, NA_STEPS),
        in_specs=[
            pl.BlockSpec((NA_TQ, 2 * NA_WIDTH), row),
            pl.BlockSpec((SEQ, NA_WIDTH), lambda b, s: (b, 0)),
            pl.BlockSpec((SEQ, NA_WIDTH), lambda b, s: (b, 0)),
            pl.BlockSpec((1, 1, NA_HEADS // 2, 2 * NA_TQ, NA_WIN), bias_idx),
            pl.BlockSpec((NA_TQ, D_MODEL), row),
            _fixed_spec((None, D_MODEL, W_Z), (layer, 0, NAZ_BLOCK)),
            _fixed_spec((None, 1, W_Z), (layer, 0, NAZ_BLOCK)),
        ],
        out_specs=pl.BlockSpec((NA_TQ, NA_WIDTH), row),
        out_shape=jax.ShapeDtypeStruct((TOKENS, NA_WIDTH), BF16),
        compiler_params=_cparams(2),
        name="na",
    )(q2, k, v, bias, x, wb, bb)


def _gqa_body(q_ref, kt_ref, v_ref, x_ref, wz_ref, bz_ref, out_ref):
    xb = x_ref[...].astype(BF16)
    tiles_per_kv = GQA_WIDTH // LANES // GQA_KV_HEADS
    for p in range(GQA_WIDTH // LANES):
        col = p * LANES
        kv = p // tiles_per_kv
        qs = jnp.concatenate([q_ref[:, col:col + LANES], q_ref[:, GQA_WIDTH + col:GQA_WIDTH + col + LANES]], axis=0)
        s = jnp.dot(qs, kt_ref[0, kv], preferred_element_type=F32)
        y = _pair_softmax_pv(s, v_ref[:, kv * LANES:(kv + 1) * LANES], GQA_TQ)
        out_ref[:, col:col + LANES] = (y * _silu_gate(xb, wz_ref, bz_ref, col)).astype(BF16)


def _gqa(layer, q2, kt, v, x, wb, bb):
    steps = SEQ // GQA_TQ
    row = lambda b, s: (b * steps + s, 0)
    return pl.pallas_call(
        _gqa_body,
        grid=(BATCH, steps),
        in_specs=[
            pl.BlockSpec((GQA_TQ, 2 * GQA_WIDTH), row),
            pl.BlockSpec((1, GQA_KV_HEADS, LANES, SEQ), lambda b, s: (b, 0, 0, 0)),
            pl.BlockSpec((SEQ, GQA_KV_HEADS * LANES), lambda b, s: (b, 0)),
            pl.BlockSpec((GQA_TQ, D_MODEL), row),
            _fixed_spec((None, D_MODEL, W_Z), (layer, 0, GQZ_BLOCK)),
            _fixed_spec((None, 1, W_Z), (layer, 0, GQZ_BLOCK)),
        ],
        out_specs=pl.BlockSpec((GQA_TQ, GQA_WIDTH), row),
        out_shape=jax.ShapeDtypeStruct((TOKENS, GQA_WIDTH), BF16),
        compiler_params=_cparams(2),
        name="gqa",
    )(q2, kt, v, x, wb, bb)


def _merge_body(x_ref, ya_ref, yb_ref, vn_ref, wu_ref, bu_ref, wz_ref, bz_ref, ws_ref, bs_ref,
                wg_ref, bg_ref, wo_ref, wa_ref, wb_ref, wc_ref, bo_ref, lng_ref, lnb_ref, out_ref, mixed_ref):
    tm = MERGE_TM
    n_chunks = tm // SG_CHUNK
    x = x_ref[...]
    xb = x.astype(BF16)

    first_head = _lane_is_first_head((SG_CHUNK, LANES))
    for pp in range(SG_WIDTH // LANES):
        col = pp * LANES
        rhs = jnp.concatenate([vn_ref[c * SG_CHUNK:(c + 1) * SG_CHUNK, col:col + LANES] for c in range(n_chunks)], axis=1)
        res = jnp.dot(ws_ref[pp], rhs, preferred_element_type=F32)
        for c in range(n_chunks):
            blk = jnp.where(first_head, res[:SG_CHUNK, c * LANES:(c + 1) * LANES], res[SG_CHUNK:, c * LANES:(c + 1) * LANES])
            mixed_ref[c * SG_CHUNK:(c + 1) * SG_CHUNK, col:col + LANES] = blk + bs_ref[:, col:col + LANES]
    u = jnp.dot(xb, wu_ref[...], preferred_element_type=F32) + bu_ref[...]
    z = jnp.dot(xb, wz_ref[...], preferred_element_type=F32) + bz_ref[...]
    tc = (u * mixed_ref[...] * (z * jax.nn.sigmoid(z))).astype(BF16)

    def gate(i):
        lo = i * D_MODEL
        return jax.nn.sigmoid(jnp.dot(xb, wg_ref[:, lo:lo + D_MODEL], preferred_element_type=F32) + bg_ref[:, lo:lo + D_MODEL])

    merged = gate(0) * jnp.dot(ya_ref[...], wa_ref[...], preferred_element_type=F32)
    merged = merged + gate(1) * jnp.dot(yb_ref[...], wb_ref[...], preferred_element_type=F32)
    merged = merged + gate(2) * jnp.dot(tc, wc_ref[...], preferred_element_type=F32)
    sub = jnp.dot(merged.astype(BF16), wo_ref[...], preferred_element_type=F32) + bo_ref[...]
    out_ref[...] = _layer_norm(DEEPNORM_ALPHA * x + sub, lng_ref[...], lnb_ref[...])


def _merge(layer, x, ya, yb, vn, wb, bb, ws, bs, wbr, bo, lng, lnb):
    tm = MERGE_TM
    row = lambda i: (i, 0)
    vec = lambda width: _fixed_spec((None, 1, width), (layer, 0, 0))
    return pl.pallas_call(
        _merge_body,
        grid=(TOKENS // tm,),
        in_specs=[
            pl.BlockSpec((tm, D_MODEL), row),
            pl.BlockSpec((tm, NA_WIDTH), row),
            pl.BlockSpec((tm, GQA_WIDTH), row),
            pl.BlockSpec((tm, SG_WIDTH), row),
            _fixed_spec((None, D_MODEL, W_Z), (layer, 0, SGU_BLOCK)), _fixed_spec((None, 1, W_Z), (layer, 0, SGU_BLOCK)),
            _fixed_spec((None, D_MODEL, W_Z), (layer, 0, SGZ_BLOCK)), _fixed_spec((None, 1, W_Z), (layer, 0, SGZ_BLOCK)),
            _fixed_spec((None, SG_GROUPS // 2, 2 * SG_CHUNK, SG_CHUNK), (layer, 0, 0, 0)),
            _fixed_spec((None, SG_CHUNK, SG_WIDTH), (layer, 0, 0)),
            _fixed_spec((None, D_MODEL, N_BRANCH * D_MODEL), (layer, 0, GATE_BLOCK)),
            _fixed_spec((None, 1, N_BRANCH * D_MODEL), (layer, 0, GATE_BLOCK)),
            _fixed_spec((None, D_MODEL, D_MODEL), (layer, 0, 0)),
            _fixed_spec((None, NA_WIDTH, D_MODEL), (layer, 2, 0)),
            _fixed_spec((None, GQA_WIDTH, D_MODEL), (layer, 3, 0)),
            _fixed_spec((None, SG_WIDTH, D_MODEL), (layer, 4, 0)),
            vec(D_MODEL), vec(D_MODEL), vec(D_MODEL),
        ],
        out_specs=pl.BlockSpec((tm, D_MODEL), row),
        out_shape=jax.ShapeDtypeStruct((TOKENS, D_MODEL), F32),
        scratch_shapes=[pltpu.VMEM((tm, SG_WIDTH), F32)],
        compiler_params=_cparams(1),
        name="merge",
    )(x, ya, yb, vn, wb, bb, wb, bb, ws, bs, wb, bb, wbr, wbr, wbr, wbr, bo, lng, lnb)


def _rope_tables():
    t = np.arange(SEQ)
    row = (t // GRID_W).astype(np.float32)
    col = (t % GRID_W).astype(np.float32)
    freqs = np.float32(ROPE_THETA) ** (-np.arange(0, ROPE_AXIS_DIM, 2, dtype=np.float32) / np.float32(ROPE_AXIS_DIM))
    ang = np.concatenate([row[:, None] * freqs, col[:, None] * freqs], axis=-1).astype(np.float32)
    cos = np.cos(ang.astype(np.float64)).astype(np.float32)
    sin = np.sin(ang.astype(np.float64)).astype(np.float32)
    pair = (np.arange(LANES) % HEAD_DIM) // 2
    even = (np.arange(LANES) % 2 == 0)[None, :]
    cos_t = cos[:, pair]
    s1_t = np.where(even, -sin[:, pair], 0.0).astype(np.float32)
    s2_t = np.where(even, 0.0, sin[:, pair]).astype(np.float32)
    return jnp.asarray(cos_t), jnp.asarray(s1_t), jnp.asarray(s2_t)


def _reorder_columns(a, pad_value_shape):
    parts = []
    for seg in _W_SEGMENTS:
        if seg is None:
            parts.append(jnp.zeros(pad_value_shape + (W_PAD,), a.dtype))
        else:
            parts.append(a[..., seg[0]:seg[1]])
    return jnp.concatenate(parts, axis=-1)


def kernel(x, ln_in_g, ln_in_b, w_in, b_in, na_rpb, q_norm_g, k_norm_g, sg_ln_g, sg_ln_b, sg_w, sg_b,
           w_br_a, w_br_b, w_br_c, w_out, b_out, ln_post_g, ln_post_b):
    assert x.shape == (BATCH, SEQ, D_MODEL) and w_in.shape == (DEPTH, D_MODEL, O_END)
    cos_t, s1_t, s2_t = _rope_tables()
    bd = jnp.asarray(np.kron(np.eye(2), np.ones((HEAD_DIM, HEAD_DIM))), BF16)
    vec3 = lambda v: v.reshape(DEPTH, 1, -1)

    wb = _reorder_columns(w_in, (DEPTH, D_MODEL)).astype(BF16)
    bb = vec3(_reorder_columns(b_in, (DEPTH,)))
    wbr = jnp.concatenate([w_out, w_br_a, w_br_b, w_br_c], axis=1).astype(BF16)
    ws = sg_w.reshape(DEPTH, SG_GROUPS // 2, 2 * SG_CHUNK, SG_CHUNK).astype(BF16)
    bs = jnp.repeat(jnp.swapaxes(sg_b, 1, 2), HEAD_DIM, axis=2)
    gq = vec3(jnp.tile(q_norm_g * ATTN_SCALE, (1, 2)))
    gk = vec3(jnp.tile(k_norm_g, (1, 2)))
    sgg, sgb = vec3(sg_ln_g), vec3(sg_ln_b)
    bo, lng, lnb = vec3(b_out), vec3(ln_post_g), vec3(ln_post_b)
    bias = _na_bias(na_rpb.reshape(-1))

    h = x.reshape(TOKENS, D_MODEL)
    for l in range(DEPTH):
        proj_args = (wb, bb, gq, gk, bd, cos_t, s1_t, s2_t, sgg, sgb)
        if l == 0:
            h, naq, nak, nav, gqq, gqkt, gqv, vn = _proj(l, True, h, ln_in_g.reshape(1, -1), ln_in_b.reshape(1, -1), *proj_args)
        else:
            naq, nak, nav, gqq, gqkt, gqv, vn = _proj(l, False, h, None, None, *proj_args)
        ya = _na(l, naq, nak, nav, bias, h, wb, bb)
        yb = _gqa(l, gqq, gqkt, gqv, h, wb, bb)
        h = _merge(l, h, ya, yb, vn, wb, bb, ws, bs, wbr, bo, lng, lnb)
    return h.reshape(BATCH, SEQ, D_MODEL)
```

```python
import functools

import numpy as np
import jax
import jax.numpy as jnp
from jax import lax
from jax.experimental import pallas as pl
from jax.experimental.pallas import tpu as pltpu

F32 = jnp.float32
BF16 = jnp.bfloat16

D_MODEL = 1024
BATCH = 8
SEQ = 2048
DEPTH = 2
GRID_W = 64
ROWS = SEQ // GRID_W
HEAD_DIM = 64
NA_HEADS = 8
NA_WIDTH = 512
NA_KH = 8
NA_KW = 16
RPB_ROWS = 2 * NA_KH - 1
RPB_COLS = 2 * NA_KW - 1
GQA_HEADS = 8
GQA_KV_HEADS = 2
GQA_WIDTH = 512
GQA_KV_WIDTH = 128
ROPE_THETA = 10000.0
ROPE_AXIS_DIM = HEAD_DIM // 2
SG_WIDTH = 512
SG_GROUPS = 8
SG_CHUNK = 128
N_BRANCH = 3
LN_EPS = 1e-5
RMS_EPS = 1e-6
DEEPNORM_ALPHA = (2.0 * DEPTH) ** 0.25
ATTN_SCALE = HEAD_DIM ** -0.5
MASK_VALUE = -1e30

TOKENS = BATCH * SEQ
LANES = 128
VMEM_LIMIT = 56 * 1024 * 1024

_SPLITS = (NA_WIDTH,) * 4 + (GQA_WIDTH, GQA_KV_WIDTH, GQA_KV_WIDTH, GQA_WIDTH) + (SG_WIDTH,) * 3 + (N_BRANCH * D_MODEL,)
_OFFS = [int(v) for v in np.concatenate([[0], np.cumsum(_SPLITS)])]
(O_NAQ, O_NAK, O_NAV, O_NAZ, O_GQQ, O_GQK, O_GQV, O_GQZ, O_SGU, O_SGV, O_SGZ, O_GATE, O_END) = _OFFS

W_PROJ = 3072
W_Z = 512
_W_SEGMENTS = ((O_NAQ, O_NAZ), (O_GQQ, O_GQZ), (O_SGV, O_SGZ), None, (O_GATE, O_END),
               (O_NAZ, O_GQQ), (O_GQZ, O_SGU), (O_SGU, O_SGV), (O_SGZ, O_GATE))
W_PAD = 256
W_TOTAL = 8192
GATE_BLOCK = 1
NAZ_BLOCK, GQZ_BLOCK, SGU_BLOCK, SGZ_BLOCK = 12, 13, 14, 15
P_NAQ, P_NAK, P_NAV, P_GQQ, P_GQK, P_GQV, P_SGV = 0, 512, 1024, 1536, 2048, 2176, 2304

PROJ_TM = 512
NA_ROWS = 4
NA_TQ = NA_ROWS * GRID_W
NA_WIN_ROWS = 12
NA_WIN = NA_WIN_ROWS * GRID_W
NA_GROUP_COUNT = ROWS // NA_ROWS
NA_GROUPS_PER_STEP = 2
NA_TYPES = 3
NA_SOFTMAX_ROWS = 32
GQA_TQ = 512
GQA_STAGE_ROWS = 256
GQA_SOFTMAX_ROWS = 16
MERGE_TM = 256


def _cparams(n_axes):
    return pltpu.CompilerParams(dimension_semantics=("arbitrary",) * n_axes, vmem_limit_bytes=VMEM_LIMIT)


def _fixed_spec(block_shape, index):
    return pl.BlockSpec(block_shape, lambda *_: index, pipeline_mode=pl.Buffered(1))


def _layer_norm(x, g, b):
    mu = jnp.mean(x, axis=-1, keepdims=True)
    xc = x - mu
    var = jnp.mean(xc * xc, axis=-1, keepdims=True)
    return xc * lax.rsqrt(var + LN_EPS) * g + b


def _lane_is_first_head(shape):
    return lax.broadcasted_iota(jnp.int32, shape, len(shape) - 1) % LANES < HEAD_DIM


def _na_window_start(step):
    return min(max(NA_ROWS * step - NA_KH // 2, 0), ROWS - NA_WIN_ROWS)


_NA_TYPE_STEPS = (0, 1, NA_GROUP_COUNT - 1)


def _na_tile_plan():
    plan = {}
    for t, step in enumerate(_NA_TYPE_STEPS):
        w0 = _na_window_start(step)
        for dq in range(NA_ROWS):
            r = NA_ROWS * step + dq
            row_start = min(max(r - NA_KH // 2, 0), ROWS - NA_KH)
            for j in range(NA_WIN_ROWS):
                key_row = w0 + j
                inside = row_start <= key_row < row_start + NA_KH
                plan[(t, dq, j)] = key_row - r + (NA_KH - 1) if inside else None
    return plan


def _na_bias_body(rpb_ref, out_ref):
    layer, pair = pl.program_id(0), pl.program_id(1)
    qi = lax.broadcasted_iota(jnp.int32, (GRID_W, GRID_W), 0)
    ki = lax.broadcasted_iota(jnp.int32, (GRID_W, GRID_W), 1)
    d = jnp.clip(ki - qi + (NA_KW - 1), 0, RPB_COLS - 1)
    col_start = jnp.clip(qi - NA_KW // 2, 0, GRID_W - NA_KW)
    col_valid = (ki >= col_start) & (ki < col_start + NA_KW)
    masked = jnp.full((GRID_W, GRID_W), MASK_VALUE, F32)
    plan = _na_tile_plan()
    for hsel in range(2):
        base = ((layer * NA_HEADS + 2 * pair + hsel) * RPB_ROWS) * RPB_COLS
        for ro in range(RPB_ROWS):
            users = [key for key, val in plan.items() if val == ro]
            if not users:
                continue
            tile = jnp.zeros((GRID_W, GRID_W), F32)
            for c in range(RPB_COLS):
                tile = jnp.where(d == c, rpb_ref[base + ro * RPB_COLS + c], tile)
            tile = jnp.where(col_valid, tile, MASK_VALUE)
            for (t, dq, j) in users:
                row0 = hsel * NA_TQ + dq * GRID_W
                out_ref[0, t, 0, row0:row0 + GRID_W, j * GRID_W:(j + 1) * GRID_W] = tile
        for (t, dq, j), val in plan.items():
            if val is None:
                row0 = hsel * NA_TQ + dq * GRID_W
                out_ref[0, t, 0, row0:row0 + GRID_W, j * GRID_W:(j + 1) * GRID_W] = masked


def _na_bias(rpb_flat):
    return pl.pallas_call(
        _na_bias_body,
        grid=(DEPTH, NA_HEADS // 2),
        in_specs=[pl.BlockSpec(memory_space=pltpu.SMEM)],
        out_specs=pl.BlockSpec((1, NA_TYPES, 1, 2 * NA_TQ, NA_WIN), lambda l, p: (l, 0, p, 0, 0)),
        out_shape=jax.ShapeDtypeStruct((DEPTH, NA_TYPES, NA_HEADS // 2, 2 * NA_TQ, NA_WIN), F32),
        compiler_params=_cparams(2),
        name="na_bias",
    )(rpb_flat)


def _proj_body(first, *refs):
    if first:
        (x_ref, lng_ref, lnb_ref, w_ref, b_ref, gq_ref, gk_ref, bd_ref, cos_ref, s1_ref, s2_ref,
         sgg_ref, sgb_ref, h_ref, naq_ref, nak_ref, nav_ref, gqq_ref, gqkt_ref, gqv_ref, vn_ref) = refs
    else:
        (x_ref, w_ref, b_ref, gq_ref, gk_ref, bd_ref, cos_ref, s1_ref, s2_ref,
         sgg_ref, sgb_ref, naq_ref, nak_ref, nav_ref, gqq_ref, gqkt_ref, gqv_ref, vn_ref) = refs
    x = x_ref[...]
    if first:
        x = _layer_norm(x, lng_ref[...], lnb_ref[...])
        h_ref[...] = x
    xb = x.astype(BF16)

    def proj(a, b):
        return jnp.dot(xb, w_ref[:, a:b], preferred_element_type=F32) + b_ref[:, a:b]

    def store_head_masked(dst_ref, val, col):
        first_head = _lane_is_first_head(val.shape)
        dst_ref[:, col:col + LANES] = jnp.where(first_head, val, 0.0).astype(BF16)
        dst_ref[:, NA_WIDTH + col:NA_WIDTH + col + LANES] = jnp.where(first_head, 0.0, val).astype(BF16)

    def rms_rope(val, gain):
        sq = val * val
        hi = sq.astype(BF16)
        lo = (sq - hi.astype(F32)).astype(BF16)
        ssum = (jnp.dot(hi, bd_ref[...], preferred_element_type=F32)
                + jnp.dot(lo, bd_ref[...], preferred_element_type=F32))
        xn = val * lax.rsqrt(ssum * (1.0 / HEAD_DIM) + RMS_EPS) * gain
        return (xn * cos_ref[...] + pltpu.roll(xn, LANES - 1, 1) * s1_ref[...]
                + pltpu.roll(xn, 1, 1) * s2_ref[...])

    q = proj(P_NAQ, P_NAQ + NA_WIDTH) * ATTN_SCALE
    for p in range(NA_WIDTH // LANES):
        store_head_masked(naq_ref, q[:, p * LANES:(p + 1) * LANES], p * LANES)
    nak_ref[...] = proj(P_NAK, P_NAK + NA_WIDTH).astype(BF16)
    nav_ref[...] = proj(P_NAV, P_NAV + NA_WIDTH).astype(BF16)

    q = proj(P_GQQ, P_GQQ + GQA_WIDTH)
    for p in range(GQA_WIDTH // LANES):
        store_head_masked(gqq_ref, rms_rope(q[:, p * LANES:(p + 1) * LANES], gq_ref[...]), p * LANES)
    k = rms_rope(proj(P_GQK, P_GQK + GQA_KV_WIDTH), gk_ref[...])
    v = proj(P_GQV, P_GQV + GQA_KV_WIDTH)
    first_head = _lane_is_first_head(k.shape)
    k_sw = pltpu.roll(k, HEAD_DIM, 1)
    v_sw = pltpu.roll(v, HEAD_DIM, 1)
    gqkt_ref[0, 0] = jnp.where(first_head, k, k_sw).T.astype(BF16)
    gqkt_ref[0, 1] = jnp.where(first_head, k_sw, k).T.astype(BF16)
    gqv_ref[:, :LANES] = jnp.where(first_head, v, v_sw).astype(BF16)
    gqv_ref[:, LANES:] = jnp.where(first_head, v_sw, v).astype(BF16)

    vn_ref[...] = _layer_norm(proj(P_SGV, P_SGV + SG_WIDTH), sgg_ref[...], sgb_ref[...]).astype(BF16)


def _proj(layer, first, x, ln_g, ln_b, wb, bb, gq, gk, bd, cos_t, s1_t, s2_t, sgg, sgb):
    tm = PROJ_TM
    tiles_per_seq = SEQ // tm
    row = lambda i: (i, 0)
    pos = lambda i: (i % tiles_per_seq, 0)
    in_specs = [pl.BlockSpec((tm, D_MODEL), row)]
    args = [x]
    if first:
        in_specs += [_fixed_spec((1, D_MODEL), (0, 0)), _fixed_spec((1, D_MODEL), (0, 0))]
        args += [ln_g, ln_b]
    in_specs += [_fixed_spec((None, D_MODEL, W_PROJ), (layer, 0, 0)), _fixed_spec((None, 1, W_PROJ), (layer, 0, 0)),
                 _fixed_spec((None, 1, LANES), (layer, 0, 0)), _fixed_spec((None, 1, LANES), (layer, 0, 0)),
                 _fixed_spec((LANES, LANES), (0, 0)),
                 pl.BlockSpec((tm, LANES), pos), pl.BlockSpec((tm, LANES), pos), pl.BlockSpec((tm, LANES), pos),
                 _fixed_spec((None, 1, SG_WIDTH), (layer, 0, 0)), _fixed_spec((None, 1, SG_WIDTH), (layer, 0, 0))]
    args += [wb, bb, gq, gk, bd, cos_t, s1_t, s2_t, sgg, sgb]
    out_shape, out_specs = [], []
    if first:
        out_shape.append(jax.ShapeDtypeStruct((TOKENS, D_MODEL), F32))
        out_specs.append(pl.BlockSpec((tm, D_MODEL), row))
    out_shape += [
        jax.ShapeDtypeStruct((TOKENS, 2 * NA_WIDTH), BF16),
        jax.ShapeDtypeStruct((TOKENS, NA_WIDTH), BF16),
        jax.ShapeDtypeStruct((TOKENS, NA_WIDTH), BF16),
        jax.ShapeDtypeStruct((TOKENS, 2 * GQA_WIDTH), BF16),
        jax.ShapeDtypeStruct((BATCH, GQA_KV_HEADS, LANES, SEQ), BF16),
        jax.ShapeDtypeStruct((TOKENS, GQA_KV_HEADS * LANES), BF16),
        jax.ShapeDtypeStruct((TOKENS, SG_WIDTH), BF16),
    ]
    out_specs += [
        pl.BlockSpec((tm, 2 * NA_WIDTH), row),
        pl.BlockSpec((tm, NA_WIDTH), row),
        pl.BlockSpec((tm, NA_WIDTH), row),
        pl.BlockSpec((tm, 2 * GQA_WIDTH), row),
        pl.BlockSpec((1, GQA_KV_HEADS, LANES, tm), lambda i: (i // tiles_per_seq, 0, 0, i % tiles_per_seq)),
        pl.BlockSpec((tm, GQA_KV_HEADS * LANES), row),
        pl.BlockSpec((tm, SG_WIDTH), row),
    ]
    return pl.pallas_call(
        functools.partial(_proj_body, first),
        grid=(TOKENS // tm,),
        in_specs=in_specs,
        out_specs=out_specs,
        out_shape=out_shape,
        compiler_params=_cparams(1),
        name="proj_first" if first else "proj",
    )(*args)


def _silu_gate(x_ref, wz_ref, bz_ref):
    z = jnp.dot(x_ref[...].astype(BF16), wz_ref[...], preferred_element_type=F32) + bz_ref[...]
    return z * jax.nn.sigmoid(z)


def _softmax_rows(s_ref, e_ref, l_ref, n_rows, block_rows):
    for r in range(0, n_rows, block_rows):
        blk = s_ref[r:r + block_rows, :]
        e = jnp.exp(blk - jnp.max(blk, axis=-1, keepdims=True))
        e_ref[r:r + block_rows, :] = e.astype(BF16)
        l_ref[r:r + block_rows, :] = jnp.broadcast_to(jnp.sum(e, axis=-1, keepdims=True), (block_rows, LANES))


def _run_stages(n_stages, scores, finish, before_last):
    scores(0)
    for i in range(n_stages):
        if i + 1 < n_stages:
            scores(i + 1)
        else:
            before_last()
        finish(i)


def _na_body(q_ref, k_ref, v_ref, bias0_ref, bias1_ref, x_ref, wz_ref, bz_ref, out_ref,
             s_scr, e_scr, l_scr, y_scr, g_scr):
    step = pl.program_id(1)
    bias_refs = (bias0_ref, bias1_ref)
    first_head = _lane_is_first_head((NA_TQ, LANES))
    stages = [(g, p) for g in range(NA_GROUPS_PER_STEP) for p in range(NA_WIDTH // LANES)]

    def window(g):
        w0 = jnp.clip(NA_ROWS * (NA_GROUPS_PER_STEP * step + g) - NA_KH // 2, 0, ROWS - NA_WIN_ROWS)
        return pl.ds(pl.multiple_of(w0 * GRID_W, 2 * LANES), NA_WIN)

    def scores(i):
        g, p = stages[i]
        rows, col = slice(g * NA_TQ, (g + 1) * NA_TQ), p * LANES
        qs = jnp.concatenate([q_ref[rows, col:col + LANES], q_ref[rows, NA_WIDTH + col:NA_WIDTH + col + LANES]], axis=0)
        s = lax.dot_general(qs, k_ref[window(g), col:col + LANES], (((1,), (1,)), ((), ())), preferred_element_type=F32)
        s_scr[i % 2] = s + bias_refs[g][0, 0, p]

    def finish(i):
        g, p = stages[i]
        rows, col, slot = slice(g * NA_TQ, (g + 1) * NA_TQ), p * LANES, i % 2
        _softmax_rows(s_scr.at[slot], e_scr.at[slot], l_scr.at[slot], 2 * NA_TQ, NA_SOFTMAX_ROWS)
        o = jnp.dot(e_scr[slot], v_ref[window(g), col:col + LANES], preferred_element_type=F32) / l_scr[slot]
        y_scr[rows, col:col + LANES] = jnp.where(first_head, o[:NA_TQ], o[NA_TQ:])

    def gate():
        g_scr[...] = _silu_gate(x_ref, wz_ref, bz_ref)

    _run_stages(len(stages), scores, finish, gate)
    out_ref[...] = (y_scr[...] * g_scr[...]).astype(BF16)


def _na(layer, q2, k, v, bias, x, wb, bb):
    steps = NA_GROUP_COUNT // NA_GROUPS_PER_STEP
    tq = NA_GROUPS_PER_STEP * NA_TQ

    def bias_spec(g):
        def index(b, s):
            group = NA_GROUPS_PER_STEP * s + g
            return (layer, jnp.minimum(group, 1) + jnp.maximum(group - (NA_GROUP_COUNT - 2), 0), 0, 0, 0)
        return pl.BlockSpec((1, 1, NA_HEADS // 2, 2 * NA_TQ, NA_WIN), index)

    row = lambda b, s: (b * steps + s, 0)
    return pl.pallas_call(
        _na_body,
        grid=(BATCH, steps),
        in_specs=[
            pl.BlockSpec((tq, 2 * NA_WIDTH), row),
            pl.BlockSpec((SEQ, NA_WIDTH), lambda b, s: (b, 0)),
            pl.BlockSpec((SEQ, NA_WIDTH), lambda b, s: (b, 0)),
            bias_spec(0), bias_spec(1),
            pl.BlockSpec((tq, D_MODEL), row),
            _fixed_spec((None, D_MODEL, W_Z), (layer, 0, NAZ_BLOCK)),
            _fixed_spec((None, 1, W_Z), (layer, 0, NAZ_BLOCK)),
        ],
        out_specs=pl.BlockSpec((tq, NA_WIDTH), row),
        out_shape=jax.ShapeDtypeStruct((TOKENS, NA_WIDTH), BF16),
        scratch_shapes=[pltpu.VMEM((2, 2 * NA_TQ, NA_WIN), F32), pltpu.VMEM((2, 2 * NA_TQ, NA_WIN), BF16),
                        pltpu.VMEM((2, 2 * NA_TQ, LANES), F32), pltpu.VMEM((tq, NA_WIDTH), F32),
                        pltpu.VMEM((tq, NA_WIDTH), F32)],
        compiler_params=_cparams(2),
        name="na",
    )(q2, k, v, bias, bias, x, wb, bb)


def _gqa_body(q_ref, kt_ref, v_ref, x_ref, wz_ref, bz_ref, out_ref, s_scr, e_scr, l_scr, y_scr, g_scr):
    m = GQA_STAGE_ROWS
    tiles_per_kv = GQA_WIDTH // LANES // GQA_KV_HEADS
    first_head = _lane_is_first_head((m, LANES))
    stages = [(p, rb, h) for p in range(GQA_WIDTH // LANES) for rb in range(GQA_TQ // m) for h in range(2)]

    def scores(i):
        p, rb, h = stages[i]
        col = h * GQA_WIDTH + p * LANES
        s_scr[i % 2] = jnp.dot(q_ref[rb * m:(rb + 1) * m, col:col + LANES], kt_ref[0, p // tiles_per_kv],
                               preferred_element_type=F32)

    def finish(i):
        p, rb, h = stages[i]
        rows, col, slot, kv = slice(rb * m, (rb + 1) * m), p * LANES, i % 2, p // tiles_per_kv
        _softmax_rows(s_scr.at[slot], e_scr.at[slot], l_scr.at[slot], m, GQA_SOFTMAX_ROWS)
        o = jnp.dot(e_scr[slot], v_ref[:, kv * LANES:(kv + 1) * LANES], preferred_element_type=F32) / l_scr[slot]
        if h == 0:
            y_scr[rows, col:col + LANES] = o
        else:
            y_scr[rows, col:col + LANES] = jnp.where(first_head, y_scr[rows, col:col + LANES], o)

    def gate():
        g_scr[...] = _silu_gate(x_ref, wz_ref, bz_ref)

    _run_stages(len(stages), scores, finish, gate)
    out_ref[...] = (y_scr[...] * g_scr[...]).astype(BF16)


def _gqa(layer, q2, kt, v, x, wb, bb):
    steps = SEQ // GQA_TQ
    row = lambda b, s: (b * steps + s, 0)
    return pl.pallas_call(
        _gqa_body,
        grid=(BATCH, steps),
        in_specs=[
            pl.BlockSpec((GQA_TQ, 2 * GQA_WIDTH), row),
            pl.BlockSpec((1, GQA_KV_HEADS, LANES, SEQ), lambda b, s: (b, 0, 0, 0)),
            pl.BlockSpec((SEQ, GQA_KV_HEADS * LANES), lambda b, s: (b, 0)),
            pl.BlockSpec((GQA_TQ, D_MODEL), row),
            _fixed_spec((None, D_MODEL, W_Z), (layer, 0, GQZ_BLOCK)),
            _fixed_spec((None, 1, W_Z), (layer, 0, GQZ_BLOCK)),
        ],
        out_specs=pl.BlockSpec((GQA_TQ, GQA_WIDTH), row),
        out_shape=jax.ShapeDtypeStruct((TOKENS, GQA_WIDTH), BF16),
        scratch_shapes=[pltpu.VMEM((2, GQA_STAGE_ROWS, SEQ), F32), pltpu.VMEM((2, GQA_STAGE_ROWS, SEQ), BF16),
                        pltpu.VMEM((2, GQA_STAGE_ROWS, LANES), F32), pltpu.VMEM((GQA_TQ, GQA_WIDTH), F32),
                        pltpu.VMEM((GQA_TQ, GQA_WIDTH), F32)],
        compiler_params=_cparams(2),
        name="gqa",
    )(q2, kt, v, x, wb, bb)


def _merge_body(x_ref, ya_ref, yb_ref, vn_ref, wu_ref, bu_ref, wz_ref, bz_ref, ws_ref, bs_ref,
                wg_ref, bg_ref, wo_ref, wa_ref, wb_ref, wc_ref, bo_ref, lng_ref, lnb_ref, out_ref, mixed_ref):
    tm = MERGE_TM
    n_chunks = tm // SG_CHUNK
    x = x_ref[...]
    xb = x.astype(BF16)

    first_head = _lane_is_first_head((SG_CHUNK, LANES))
    for pp in range(SG_WIDTH // LANES):
        col = pp * LANES
        rhs = jnp.concatenate([vn_ref[c * SG_CHUNK:(c + 1) * SG_CHUNK, col:col + LANES] for c in range(n_chunks)], axis=1)
        res = jnp.dot(ws_ref[pp], rhs, preferred_element_type=F32)
        for c in range(n_chunks):
            blk = jnp.where(first_head, res[:SG_CHUNK, c * LANES:(c + 1) * LANES], res[SG_CHUNK:, c * LANES:(c + 1) * LANES])
            mixed_ref[c * SG_CHUNK:(c + 1) * SG_CHUNK, col:col + LANES] = blk + bs_ref[:, col:col + LANES]
    u = jnp.dot(xb, wu_ref[...], preferred_element_type=F32) + bu_ref[...]
    z = jnp.dot(xb, wz_ref[...], preferred_element_type=F32) + bz_ref[...]
    tc = (u * mixed_ref[...] * (z * jax.nn.sigmoid(z))).astype(BF16)

    def gate(i):
        lo = i * D_MODEL
        return jax.nn.sigmoid(jnp.dot(xb, wg_ref[:, lo:lo + D_MODEL], preferred_element_type=F32) + bg_ref[:, lo:lo + D_MODEL])

    merged = gate(0) * jnp.dot(ya_ref[...], wa_ref[...], preferred_element_type=F32)
    merged = merged + gate(1) * jnp.dot(yb_ref[...], wb_ref[...], preferred_element_type=F32)
    merged = merged + gate(2) * jnp.dot(tc, wc_ref[...], preferred_element_type=F32)
    sub = jnp.dot(merged.astype(BF16), wo_ref[...], preferred_element_type=F32) + bo_ref[...]
    out_ref[...] = _layer_norm(DEEPNORM_ALPHA * x + sub, lng_ref[...], lnb_ref[...])


def _merge(layer, x, ya, yb, vn, wb, bb, ws, bs, wbr, bo, lng, lnb):
    tm = MERGE_TM
    row = lambda i: (i, 0)
    vec = lambda width: _fixed_spec((None, 1, width), (layer, 0, 0))
    return pl.pallas_call(
        _merge_body,
        grid=(TOKENS // tm,),
        in_specs=[
            pl.BlockSpec((tm, D_MODEL), row),
            pl.BlockSpec((tm, NA_WIDTH), row),
            pl.BlockSpec((tm, GQA_WIDTH), row),
            pl.BlockSpec((tm, SG_WIDTH), row),
            _fixed_spec((None, D_MODEL, W_Z), (layer, 0, SGU_BLOCK)), _fixed_spec((None, 1, W_Z), (layer, 0, SGU_BLOCK)),
            _fixed_spec((None, D_MODEL, W_Z), (layer, 0, SGZ_BLOCK)), _fixed_spec((None, 1, W_Z), (layer, 0, SGZ_BLOCK)),
            _fixed_spec((None, SG_GROUPS // 2, 2 * SG_CHUNK, SG_CHUNK), (layer, 0, 0, 0)),
            _fixed_spec((None, SG_CHUNK, SG_WIDTH), (layer, 0, 0)),
            _fixed_spec((None, D_MODEL, N_BRANCH * D_MODEL), (layer, 0, GATE_BLOCK)),
            _fixed_spec((None, 1, N_BRANCH * D_MODEL), (layer, 0, GATE_BLOCK)),
            _fixed_spec((None, D_MODEL, D_MODEL), (layer, 0, 0)),
            _fixed_spec((None, NA_WIDTH, D_MODEL), (layer, 2, 0)),
            _fixed_spec((None, GQA_WIDTH, D_MODEL), (layer, 3, 0)),
            _fixed_spec((None, SG_WIDTH, D_MODEL), (layer, 4, 0)),
            vec(D_MODEL), vec(D_MODEL), vec(D_MODEL),
        ],
        out_specs=pl.BlockSpec((tm, D_MODEL), row),
        out_shape=jax.ShapeDtypeStruct((TOKENS, D_MODEL), F32),
        scratch_shapes=[pltpu.VMEM((tm, SG_WIDTH), F32)],
        compiler_params=_cparams(1),
        name="merge",
    )(x, ya, yb, vn, wb, bb, wb, bb, ws, bs, wb, bb, wbr, wbr, wbr, wbr, bo, lng, lnb)


def _rope_tables():
    t = np.arange(SEQ)
    row = (t // GRID_W).astype(np.float32)
    col = (t % GRID_W).astype(np.float32)
    freqs = np.float32(ROPE_THETA) ** (-np.arange(0, ROPE_AXIS_DIM, 2, dtype=np.float32) / np.float32(ROPE_AXIS_DIM))
    ang = np.concatenate([row[:, None] * freqs, col[:, None] * freqs], axis=-1).astype(np.float32)
    cos = np.cos(ang.astype(np.float64)).astype(np.float32)
    sin = np.sin(ang.astype(np.float64)).astype(np.float32)
    pair = (np.arange(LANES) % HEAD_DIM) // 2
    even = (np.arange(LANES) % 2 == 0)[None, :]
    cos_t = cos[:, pair]
    s1_t = np.where(even, -sin[:, pair], 0.0).astype(np.float32)
    s2_t = np.where(even, 0.0, sin[:, pair]).astype(np.float32)
    return jnp.asarray(cos_t), jnp.asarray(s1_t), jnp.asarray(s2_t)


def _reorder_columns(a, pad_value_shape):
    parts = []
    for seg in _W_SEGMENTS:
        if seg is None:
            parts.append(jnp.zeros(pad_value_shape + (W_PAD,), a.dtype))
        else:
            parts.append(a[..., seg[0]:seg[1]])
    return jnp.concatenate(parts, axis=-1)


def kernel(x, ln_in_g, ln_in_b, w_in, b_in, na_rpb, q_norm_g, k_norm_g, sg_ln_g, sg_ln_b, sg_w, sg_b,
           w_br_a, w_br_b, w_br_c, w_out, b_out, ln_post_g, ln_post_b):
    assert x.shape == (BATCH, SEQ, D_MODEL) and w_in.shape == (DEPTH, D_MODEL, O_END)
    cos_t, s1_t, s2_t = _rope_tables()
    bd = jnp.asarray(np.kron(np.eye(2), np.ones((HEAD_DIM, HEAD_DIM))), BF16)
    vec3 = lambda v: v.reshape(DEPTH, 1, -1)

    wb = _reorder_columns(w_in, (DEPTH, D_MODEL)).astype(BF16)
    bb = vec3(_reorder_columns(b_in, (DEPTH,)))
    wbr = jnp.concatenate([w_out, w_br_a, w_br_b, w_br_c], axis=1).astype(BF16)
    ws = sg_w.reshape(DEPTH, SG_GROUPS // 2, 2 * SG_CHUNK, SG_CHUNK).astype(BF16)
    bs = jnp.repeat(jnp.swapaxes(sg_b, 1, 2), HEAD_DIM, axis=2)
    gq = vec3(jnp.tile(q_norm_g * ATTN_SCALE, (1, 2)))
    gk = vec3(jnp.tile(k_norm_g, (1, 2)))
    sgg, sgb = vec3(sg_ln_g), vec3(sg_ln_b)
    bo, lng, lnb = vec3(b_out), vec3(ln_post_g), vec3(ln_post_b)
    bias = _na_bias(na_rpb.reshape(-1))

    h = x.reshape(TOKENS, D_MODEL)
    for l in range(DEPTH):
        proj_args = (wb, bb, gq, gk, bd, cos_t, s1_t, s2_t, sgg, sgb)
        if l == 0:
            h, naq, nak, nav, gqq, gqkt, gqv, vn = _proj(l, True, h, ln_in_g.reshape(1, -1), ln_in_b.reshape(1, -1), *proj_args)
        else:
            naq, nak, nav, gqq, gqkt, gqv, vn = _proj(l, False, h, None, None, *proj_args)
        ya = _na(l, naq, nak, nav, bias, h, wb, bb)
        yb = _gqa(l, gqq, gqkt, gqv, h, wb, bb)
        h = _merge(l, h, ya, yb, vn, wb, bb, ws, bs, wbr, bo, lng, lnb)
    return h.reshape(BATCH, SEQ, D_MODEL)
```

```python
import functools

import numpy as np
import jax
import jax.numpy as jnp
from jax import lax
from jax.experimental import pallas as pl
from jax.experimental.pallas import tpu as pltpu

F32 = jnp.float32
BF16 = jnp.bfloat16

D_MODEL = 1024
BATCH = 8
SEQ = 2048
DEPTH = 2
GRID_W = 64
ROWS = SEQ // GRID_W
HEAD_DIM = 64
NA_HEADS = 8
NA_WIDTH = 512
NA_KH = 8
NA_KW = 16
RPB_ROWS = 2 * NA_KH - 1
RPB_COLS = 2 * NA_KW - 1
GQA_HEADS = 8
GQA_KV_HEADS = 2
GQA_WIDTH = 512
GQA_KV_WIDTH = 128
ROPE_THETA = 10000.0
ROPE_AXIS_DIM = HEAD_DIM // 2
SG_WIDTH = 512
SG_GROUPS = 8
SG_CHUNK = 128
N_BRANCH = 3
LN_EPS = 1e-5
RMS_EPS = 1e-6
DEEPNORM_ALPHA = (2.0 * DEPTH) ** 0.25
ATTN_SCALE = HEAD_DIM ** -0.5
LOG2E = 1.4426950408889634
SCORE_SCALE = ATTN_SCALE * LOG2E
MASK_VALUE = -1e30

TOKENS = BATCH * SEQ
LANES = 128
VMEM_LIMIT = 56 * 1024 * 1024

_SPLITS = (NA_WIDTH,) * 4 + (GQA_WIDTH, GQA_KV_WIDTH, GQA_KV_WIDTH, GQA_WIDTH) + (SG_WIDTH,) * 3 + (N_BRANCH * D_MODEL,)
_OFFS = [int(v) for v in np.concatenate([[0], np.cumsum(_SPLITS)])]
(O_NAQ, O_NAK, O_NAV, O_NAZ, O_GQQ, O_GQK, O_GQV, O_GQZ, O_SGU, O_SGV, O_SGZ, O_GATE, O_END) = _OFFS

W_PROJ = 3072
W_Z = 512
_W_SEGMENTS = ((O_NAQ, O_NAZ), (O_GQQ, O_GQZ), (O_SGV, O_SGZ), None, (O_GATE, O_END),
               (O_NAZ, O_GQQ), (O_GQZ, O_SGU), (O_SGU, O_SGV), (O_SGZ, O_GATE))
W_PAD = 256
W_TOTAL = 8192
GATE_BLOCK = 1
NAZ_BLOCK, GQZ_BLOCK, SGU_BLOCK, SGZ_BLOCK = 12, 13, 14, 15
P_NAQ, P_NAK, P_NAV, P_GQQ, P_GQK, P_GQV, P_SGV = 0, 512, 1024, 1536, 2048, 2176, 2304

PROJ_TM = 1024
PROJ_SUB = 512
NA_ROWS = 4
NA_TQ = NA_ROWS * GRID_W
NA_WIN_ROWS = 12
NA_WIN = NA_WIN_ROWS * GRID_W
NA_GROUP_COUNT = ROWS // NA_ROWS
NA_GROUPS_PER_STEP = 2
NA_TYPES = 3
NA_SOFTMAX_ROWS = 32
GQA_TQ = 512
GQA_STAGE_ROWS = 256
GQA_SOFTMAX_ROWS = 16
MERGE_TM = 512
MERGE_SUB = 256


def _cparams(n_axes):
    return pltpu.CompilerParams(dimension_semantics=("arbitrary",) * n_axes, vmem_limit_bytes=VMEM_LIMIT)


def _fixed_spec(block_shape, index):
    return pl.BlockSpec(block_shape, lambda *_: index, pipeline_mode=pl.Buffered(1))


def _layer_norm(x, g, b):
    mu = jnp.mean(x, axis=-1, keepdims=True)
    xc = x - mu
    var = jnp.mean(xc * xc, axis=-1, keepdims=True)
    return xc * lax.rsqrt(var + LN_EPS) * g + b


def _lane_is_first_head(shape):
    return lax.broadcasted_iota(jnp.int32, shape, len(shape) - 1) % LANES < HEAD_DIM


def _na_window_start(step):
    return min(max(NA_ROWS * step - NA_KH // 2, 0), ROWS - NA_WIN_ROWS)


_NA_TYPE_STEPS = (0, 1, NA_GROUP_COUNT - 1)


def _na_tile_plan():
    plan = {}
    for t, step in enumerate(_NA_TYPE_STEPS):
        w0 = _na_window_start(step)
        for dq in range(NA_ROWS):
            r = NA_ROWS * step + dq
            row_start = min(max(r - NA_KH // 2, 0), ROWS - NA_KH)
            for j in range(NA_WIN_ROWS):
                key_row = w0 + j
                inside = row_start <= key_row < row_start + NA_KH
                plan[(t, dq, j)] = key_row - r + (NA_KH - 1) if inside else None
    return plan


def _na_bias_body(rpb_ref, out_ref):
    layer, pair = pl.program_id(0), pl.program_id(1)
    qi = lax.broadcasted_iota(jnp.int32, (GRID_W, GRID_W), 0)
    ki = lax.broadcasted_iota(jnp.int32, (GRID_W, GRID_W), 1)
    d = jnp.clip(ki - qi + (NA_KW - 1), 0, RPB_COLS - 1)
    col_start = jnp.clip(qi - NA_KW // 2, 0, GRID_W - NA_KW)
    col_valid = (ki >= col_start) & (ki < col_start + NA_KW)
    masked = jnp.full((GRID_W, GRID_W), MASK_VALUE, F32)
    plan = _na_tile_plan()
    for hsel in range(2):
        base = ((layer * NA_HEADS + 2 * pair + hsel) * RPB_ROWS) * RPB_COLS
        for ro in range(RPB_ROWS):
            users = [key for key, val in plan.items() if val == ro]
            if not users:
                continue
            tile = jnp.zeros((GRID_W, GRID_W), F32)
            for c in range(RPB_COLS):
                tile = jnp.where(d == c, rpb_ref[base + ro * RPB_COLS + c], tile)
            tile = jnp.where(col_valid, tile * LOG2E, MASK_VALUE)
            for (t, dq, j) in users:
                row0 = hsel * NA_TQ + dq * GRID_W
                out_ref[0, t, 0, row0:row0 + GRID_W, j * GRID_W:(j + 1) * GRID_W] = tile
        for (t, dq, j), val in plan.items():
            if val is None:
                row0 = hsel * NA_TQ + dq * GRID_W
                out_ref[0, t, 0, row0:row0 + GRID_W, j * GRID_W:(j + 1) * GRID_W] = masked


def _na_bias(rpb_flat):
    return pl.pallas_call(
        _na_bias_body,
        grid=(DEPTH, NA_HEADS // 2),
        in_specs=[pl.BlockSpec(memory_space=pltpu.SMEM)],
        out_specs=pl.BlockSpec((1, NA_TYPES, 1, 2 * NA_TQ, NA_WIN), lambda l, p: (l, 0, p, 0, 0)),
        out_shape=jax.ShapeDtypeStruct((DEPTH, NA_TYPES, NA_HEADS // 2, 2 * NA_TQ, NA_WIN), F32),
        compiler_params=_cparams(2),
        name="na_bias",
    )(rpb_flat)


def _proj_body(first, *refs):
    if first:
        (x_ref, lng_ref, lnb_ref, w_ref, b_ref, gq_ref, gk_ref, bd_ref, cos_ref, s1_ref, s2_ref,
         sgg_ref, sgb_ref, h_ref, naq_ref, nak_ref, nav_ref, gqq_ref, gqkt_ref, gqv_ref, vn_ref) = refs
    else:
        (x_ref, w_ref, b_ref, gq_ref, gk_ref, bd_ref, cos_ref, s1_ref, s2_ref,
         sgg_ref, sgb_ref, naq_ref, nak_ref, nav_ref, gqq_ref, gqkt_ref, gqv_ref, vn_ref) = refs
    for r0 in range(0, PROJ_TM, PROJ_SUB):
        _proj_rows(first, slice(r0, r0 + PROJ_SUB), x_ref, lng_ref if first else None, lnb_ref if first else None,
                   w_ref, b_ref, gq_ref, gk_ref, bd_ref, cos_ref, s1_ref, s2_ref, sgg_ref, sgb_ref,
                   h_ref if first else None, naq_ref, nak_ref, nav_ref, gqq_ref, gqkt_ref, gqv_ref, vn_ref)


def _proj_rows(first, rows, x_ref, lng_ref, lnb_ref, w_ref, b_ref, gq_ref, gk_ref, bd_ref, cos_ref, s1_ref, s2_ref,
               sgg_ref, sgb_ref, h_ref, naq_ref, nak_ref, nav_ref, gqq_ref, gqkt_ref, gqv_ref, vn_ref):
    x = x_ref[rows, :]
    if first:
        x = _layer_norm(x, lng_ref[...], lnb_ref[...])
        h_ref[rows, :] = x
    xb = x.astype(BF16)

    def proj(a, b):
        return jnp.dot(xb, w_ref[:, a:b], preferred_element_type=F32) + b_ref[:, a:b]

    def store_head_masked(dst_ref, val, col):
        first_head = _lane_is_first_head(val.shape)
        dst_ref[rows, col:col + LANES] = jnp.where(first_head, val, 0.0).astype(BF16)
        dst_ref[rows, NA_WIDTH + col:NA_WIDTH + col + LANES] = jnp.where(first_head, 0.0, val).astype(BF16)

    def store_with_ones(dst_ref, val, col, second_copy):
        first_head = _lane_is_first_head(val.shape)
        dst_ref[rows, col:col + LANES] = jnp.where(first_head, val, 1.0).astype(BF16)
        dst_ref[rows, second_copy + col:second_copy + col + LANES] = jnp.where(first_head, 1.0, val).astype(BF16)

    def rms_rope(val, gain):
        sq = val * val
        hi = sq.astype(BF16)
        lo = (sq - hi.astype(F32)).astype(BF16)
        ssum = (jnp.dot(hi, bd_ref[...], preferred_element_type=F32)
                + jnp.dot(lo, bd_ref[...], preferred_element_type=F32))
        xn = val * lax.rsqrt(ssum * (1.0 / HEAD_DIM) + RMS_EPS) * gain
        return (xn * cos_ref[rows, :] + pltpu.roll(xn, LANES - 1, 1) * s1_ref[rows, :]
                + pltpu.roll(xn, 1, 1) * s2_ref[rows, :])

    q = proj(P_NAQ, P_NAQ + NA_WIDTH) * SCORE_SCALE
    for p in range(NA_WIDTH // LANES):
        store_head_masked(naq_ref, q[:, p * LANES:(p + 1) * LANES], p * LANES)
    nak_ref[rows, :] = proj(P_NAK, P_NAK + NA_WIDTH).astype(BF16)
    nav_ref[rows, :] = proj(P_NAV, P_NAV + NA_WIDTH).astype(BF16)

    q = proj(P_GQQ, P_GQQ + GQA_WIDTH)
    for p in range(GQA_WIDTH // LANES):
        store_head_masked(gqq_ref, rms_rope(q[:, p * LANES:(p + 1) * LANES], gq_ref[...]), p * LANES)
    k = rms_rope(proj(P_GQK, P_GQK + GQA_KV_WIDTH), gk_ref[...])
    v = proj(P_GQV, P_GQV + GQA_KV_WIDTH)
    first_head = _lane_is_first_head(k.shape)
    k_sw = pltpu.roll(k, HEAD_DIM, 1)
    v_sw = pltpu.roll(v, HEAD_DIM, 1)
    gqkt_ref[0, 0, :, rows] = jnp.where(first_head, k, k_sw).T.astype(BF16)
    gqkt_ref[0, 1, :, rows] = jnp.where(first_head, k_sw, k).T.astype(BF16)
    store_with_ones(gqv_ref, jnp.where(first_head, v, v_sw), 0, LANES)
    store_with_ones(gqv_ref, jnp.where(first_head, v_sw, v), 2 * LANES, LANES)

    vn_ref[rows, :] = _layer_norm(proj(P_SGV, P_SGV + SG_WIDTH), sgg_ref[...], sgb_ref[...]).astype(BF16)


def _proj(layer, first, x, ln_g, ln_b, wb, bb, gq, gk, bd, cos_t, s1_t, s2_t, sgg, sgb):
    tm = PROJ_TM
    tiles_per_seq = SEQ // tm
    row = lambda i: (i, 0)
    pos = lambda i: (i % tiles_per_seq, 0)
    in_specs = [pl.BlockSpec((tm, D_MODEL), row)]
    args = [x]
    if first:
        in_specs += [_fixed_spec((1, D_MODEL), (0, 0)), _fixed_spec((1, D_MODEL), (0, 0))]
        args += [ln_g, ln_b]
    in_specs += [_fixed_spec((None, D_MODEL, W_PROJ), (layer, 0, 0)), _fixed_spec((None, 1, W_PROJ), (layer, 0, 0)),
                 _fixed_spec((None, 1, LANES), (layer, 0, 0)), _fixed_spec((None, 1, LANES), (layer, 0, 0)),
                 _fixed_spec((LANES, LANES), (0, 0)),
                 pl.BlockSpec((tm, LANES), pos), pl.BlockSpec((tm, LANES), pos), pl.BlockSpec((tm, LANES), pos),
                 _fixed_spec((None, 1, SG_WIDTH), (layer, 0, 0)), _fixed_spec((None, 1, SG_WIDTH), (layer, 0, 0))]
    args += [wb, bb, gq, gk, bd, cos_t, s1_t, s2_t, sgg, sgb]
    out_shape, out_specs = [], []
    if first:
        out_shape.append(jax.ShapeDtypeStruct((TOKENS, D_MODEL), F32))
        out_specs.append(pl.BlockSpec((tm, D_MODEL), row))
    out_shape += [
        jax.ShapeDtypeStruct((TOKENS, 2 * NA_WIDTH), BF16),
        jax.ShapeDtypeStruct((TOKENS, NA_WIDTH), BF16),
        jax.ShapeDtypeStruct((TOKENS, NA_WIDTH), BF16),
        jax.ShapeDtypeStruct((TOKENS, 2 * GQA_WIDTH), BF16),
        jax.ShapeDtypeStruct((BATCH, GQA_KV_HEADS, LANES, SEQ), BF16),
        jax.ShapeDtypeStruct((TOKENS, 2 * GQA_KV_HEADS * LANES), BF16),
        jax.ShapeDtypeStruct((TOKENS, SG_WIDTH), BF16),
    ]
    out_specs += [
        pl.BlockSpec((tm, 2 * NA_WIDTH), row),
        pl.BlockSpec((tm, NA_WIDTH), row),
        pl.BlockSpec((tm, NA_WIDTH), row),
        pl.BlockSpec((tm, 2 * GQA_WIDTH), row),
        pl.BlockSpec((1, GQA_KV_HEADS, LANES, tm), lambda i: (i // tiles_per_seq, 0, 0, i % tiles_per_seq)),
        pl.BlockSpec((tm, 2 * GQA_KV_HEADS * LANES), row),
        pl.BlockSpec((tm, SG_WIDTH), row),
    ]
    return pl.pallas_call(
        functools.partial(_proj_body, first),
        grid=(TOKENS // tm,),
        in_specs=in_specs,
        out_specs=out_specs,
        out_shape=out_shape,
        compiler_params=_cparams(1),
        name="proj_first" if first else "proj",
    )(*args)


def _silu_gate(x_ref, wz_ref, bz_ref):
    z = jnp.dot(x_ref[...].astype(BF16), wz_ref[...], preferred_element_type=F32) + bz_ref[...]
    return z * jax.nn.sigmoid(z)


def _softmax_numerators(s_ref, e_ref, n_rows, block_rows, l_ref=None):
    n_keys = s_ref.shape[-1]
    for r in range(0, n_rows, block_rows):
        rows = slice(r, r + block_rows)
        tile_max = s_ref[rows, 0:LANES]
        for c in range(LANES, n_keys, LANES):
            tile_max = jnp.maximum(tile_max, s_ref[rows, c:c + LANES])
        m = jnp.broadcast_to(jnp.max(tile_max, axis=-1, keepdims=True), (block_rows, LANES))
        tile_sum = None
        for c in range(0, n_keys, LANES):
            e = jnp.exp2(s_ref[rows, c:c + LANES] - m)
            e_ref[rows, c:c + LANES] = e.astype(BF16)
            if l_ref is not None:
                tile_sum = e if tile_sum is None else tile_sum + e
        if l_ref is not None:
            l_ref[rows, :] = jnp.broadcast_to(jnp.sum(tile_sum, axis=-1, keepdims=True), (block_rows, LANES))


def _weighted_values(e, v_ones):
    o = jnp.dot(e, v_ones, preferred_element_type=F32)
    return o / pltpu.roll(o, HEAD_DIM, 1)


def _run_stages(n_stages, scores, finish, before_last):
    scores(0)
    for i in range(n_stages):
        if i + 1 < n_stages:
            scores(i + 1)
        else:
            before_last()
        finish(i)


def _na_body(q_ref, k_ref, v_ref, bias0_ref, bias1_ref, x_ref, wz_ref, bz_ref, out_ref,
             s_scr, e_scr, l_scr, y_scr, g_scr):
    step = pl.program_id(1)
    bias_refs = (bias0_ref, bias1_ref)
    first_head = _lane_is_first_head((NA_TQ, LANES))
    stages = [(g, p) for g in range(NA_GROUPS_PER_STEP) for p in range(NA_WIDTH // LANES)]

    def window(g):
        w0 = jnp.clip(NA_ROWS * (NA_GROUPS_PER_STEP * step + g) - NA_KH // 2, 0, ROWS - NA_WIN_ROWS)
        return pl.ds(pl.multiple_of(w0 * GRID_W, 2 * LANES), NA_WIN)

    def scores(i):
        g, p = stages[i]
        rows, col = slice(g * NA_TQ, (g + 1) * NA_TQ), p * LANES
        qs = jnp.concatenate([q_ref[rows, col:col + LANES], q_ref[rows, NA_WIDTH + col:NA_WIDTH + col + LANES]], axis=0)
        s = lax.dot_general(qs, k_ref[window(g), col:col + LANES], (((1,), (1,)), ((), ())), preferred_element_type=F32)
        s_scr[i % 2] = s + bias_refs[g][0, 0, p]

    def finish(i):
        g, p = stages[i]
        rows, col, slot = slice(g * NA_TQ, (g + 1) * NA_TQ), p * LANES, i % 2
        _softmax_numerators(s_scr.at[slot], e_scr.at[slot], 2 * NA_TQ, NA_SOFTMAX_ROWS, l_scr.at[slot])
        o = jnp.dot(e_scr[slot], v_ref[window(g), col:col + LANES], preferred_element_type=F32) / l_scr[slot]
        y_scr[rows, col:col + LANES] = jnp.where(first_head, o[:NA_TQ], o[NA_TQ:])

    def gate():
        g_scr[...] = _silu_gate(x_ref, wz_ref, bz_ref)

    _run_stages(len(stages), scores, finish, gate)
    out_ref[...] = (y_scr[...] * g_scr[...]).astype(BF16)


def _na(layer, q2, k, v, bias, x, wb, bb):
    steps = NA_GROUP_COUNT // NA_GROUPS_PER_STEP
    tq = NA_GROUPS_PER_STEP * NA_TQ

    def bias_spec(g):
        def index(b, s):
            group = NA_GROUPS_PER_STEP * s + g
            return (layer, jnp.minimum(group, 1) + jnp.maximum(group - (NA_GROUP_COUNT - 2), 0), 0, 0, 0)
        return pl.BlockSpec((1, 1, NA_HEADS // 2, 2 * NA_TQ, NA_WIN), index)

    row = lambda b, s: (b * steps + s, 0)
    return pl.pallas_call(
        _na_body,
        grid=(BATCH, steps),
        in_specs=[
            pl.BlockSpec((tq, 2 * NA_WIDTH), row),
            pl.BlockSpec((SEQ, NA_WIDTH), lambda b, s: (b, 0)),
            pl.BlockSpec((SEQ, NA_WIDTH), lambda b, s: (b, 0)),
            bias_spec(0), bias_spec(1),
            pl.BlockSpec((tq, D_MODEL), row),
            _fixed_spec((None, D_MODEL, W_Z), (layer, 0, NAZ_BLOCK)),
            _fixed_spec((None, 1, W_Z), (layer, 0, NAZ_BLOCK)),
        ],
        out_specs=pl.BlockSpec((tq, NA_WIDTH), row),
        out_shape=jax.ShapeDtypeStruct((TOKENS, NA_WIDTH), BF16),
        scratch_shapes=[pltpu.VMEM((2, 2 * NA_TQ, NA_WIN), F32), pltpu.VMEM((2, 2 * NA_TQ, NA_WIN), BF16),
                        pltpu.VMEM((2, 2 * NA_TQ, LANES), F32), pltpu.VMEM((tq, NA_WIDTH), F32),
                        pltpu.VMEM((tq, NA_WIDTH), F32)],
        compiler_params=_cparams(2),
        name="na",
    )(q2, k, v, bias, bias, x, wb, bb)


def _gqa_body(q_ref, kt_ref, v_ref, x_ref, wz_ref, bz_ref, out_ref, s_scr, e_scr, y_scr, g_scr):
    m = GQA_STAGE_ROWS
    tiles_per_kv = GQA_WIDTH // LANES // GQA_KV_HEADS
    first_head = _lane_is_first_head((m, LANES))
    stages = [(p, rb, h) for p in range(GQA_WIDTH // LANES) for rb in range(GQA_TQ // m) for h in range(2)]

    def scores(i):
        p, rb, h = stages[i]
        col = h * GQA_WIDTH + p * LANES
        s_scr[i % 2] = jnp.dot(q_ref[rb * m:(rb + 1) * m, col:col + LANES], kt_ref[0, p // tiles_per_kv],
                               preferred_element_type=F32)

    def finish(i):
        p, rb, h = stages[i]
        rows, col, slot, kv = slice(rb * m, (rb + 1) * m), p * LANES, i % 2, p // tiles_per_kv
        _softmax_numerators(s_scr.at[slot], e_scr.at[slot], m, GQA_SOFTMAX_ROWS)
        vcol = (2 * kv + h) * LANES
        o = _weighted_values(e_scr[slot], v_ref[:, vcol:vcol + LANES])
        if h == 0:
            y_scr[rows, col:col + LANES] = o
        else:
            y_scr[rows, col:col + LANES] = jnp.where(first_head, y_scr[rows, col:col + LANES], o)

    def gate():
        g_scr[...] = _silu_gate(x_ref, wz_ref, bz_ref)

    _run_stages(len(stages), scores, finish, gate)
    out_ref[...] = (y_scr[...] * g_scr[...]).astype(BF16)


def _gqa(layer, q2, kt, v, x, wb, bb):
    steps = SEQ // GQA_TQ
    row = lambda b, s: (b * steps + s, 0)
    return pl.pallas_call(
        _gqa_body,
        grid=(BATCH, steps),
        in_specs=[
            pl.BlockSpec((GQA_TQ, 2 * GQA_WIDTH), row),
            pl.BlockSpec((1, GQA_KV_HEADS, LANES, SEQ), lambda b, s: (b, 0, 0, 0)),
            pl.BlockSpec((SEQ, 2 * GQA_KV_HEADS * LANES), lambda b, s: (b, 0)),
            pl.BlockSpec((GQA_TQ, D_MODEL), row),
            _fixed_spec((None, D_MODEL, W_Z), (layer, 0, GQZ_BLOCK)),
            _fixed_spec((None, 1, W_Z), (layer, 0, GQZ_BLOCK)),
        ],
        out_specs=pl.BlockSpec((GQA_TQ, GQA_WIDTH), row),
        out_shape=jax.ShapeDtypeStruct((TOKENS, GQA_WIDTH), BF16),
        scratch_shapes=[pltpu.VMEM((2, GQA_STAGE_ROWS, SEQ), F32), pltpu.VMEM((2, GQA_STAGE_ROWS, SEQ), BF16),
                        pltpu.VMEM((GQA_TQ, GQA_WIDTH), F32), pltpu.VMEM((GQA_TQ, GQA_WIDTH), F32)],
        compiler_params=_cparams(2),
        name="gqa",
    )(q2, kt, v, x, wb, bb)


def _merge_body(x_ref, ya_ref, yb_ref, vn_ref, wu_ref, bu_ref, wz_ref, bz_ref, ws_ref, bs_ref,
                wg_ref, bg_ref, wo_ref, wa_ref, wb_ref, wc_ref, bo_ref, lng_ref, lnb_ref, out_ref, mixed_ref):
    for r0 in range(0, MERGE_TM, MERGE_SUB):
        _merge_rows(slice(r0, r0 + MERGE_SUB), x_ref, ya_ref, yb_ref, vn_ref, wu_ref, bu_ref, wz_ref, bz_ref, ws_ref,
                    bs_ref, wg_ref, bg_ref, wo_ref, wa_ref, wb_ref, wc_ref, bo_ref, lng_ref, lnb_ref, out_ref, mixed_ref)


def _merge_rows(rows, x_ref, ya_ref, yb_ref, vn_ref, wu_ref, bu_ref, wz_ref, bz_ref, ws_ref, bs_ref,
                wg_ref, bg_ref, wo_ref, wa_ref, wb_ref, wc_ref, bo_ref, lng_ref, lnb_ref, out_ref, mixed_ref):
    n_chunks = MERGE_SUB // SG_CHUNK
    x = x_ref[rows, :]
    xb = x.astype(BF16)

    first_head = _lane_is_first_head((SG_CHUNK, LANES))
    chunk_rows = [slice(rows.start + c * SG_CHUNK, rows.start + (c + 1) * SG_CHUNK) for c in range(n_chunks)]
    for pp in range(SG_WIDTH // LANES):
        col = pp * LANES
        rhs = jnp.concatenate([vn_ref[cr, col:col + LANES] for cr in chunk_rows], axis=1)
        res = jnp.dot(ws_ref[pp], rhs, preferred_element_type=F32)
        for c, cr in enumerate(chunk_rows):
            blk = jnp.where(first_head, res[:SG_CHUNK, c * LANES:(c + 1) * LANES], res[SG_CHUNK:, c * LANES:(c + 1) * LANES])
            mixed_ref[cr, col:col + LANES] = blk + bs_ref[:, col:col + LANES]
    u = jnp.dot(xb, wu_ref[...], preferred_element_type=F32) + bu_ref[...]
    z = jnp.dot(xb, wz_ref[...], preferred_element_type=F32) + bz_ref[...]
    tc = (u * mixed_ref[rows, :] * (z * jax.nn.sigmoid(z))).astype(BF16)

    def gate(i):
        lo = i * D_MODEL
        return jax.nn.sigmoid(jnp.dot(xb, wg_ref[:, lo:lo + D_MODEL], preferred_element_type=F32) + bg_ref[:, lo:lo + D_MODEL])

    merged = gate(0) * jnp.dot(ya_ref[rows, :], wa_ref[...], preferred_element_type=F32)
    merged = merged + gate(1) * jnp.dot(yb_ref[rows, :], wb_ref[...], preferred_element_type=F32)
    merged = merged + gate(2) * jnp.dot(tc, wc_ref[...], preferred_element_type=F32)
    sub = jnp.dot(merged.astype(BF16), wo_ref[...], preferred_element_type=F32) + bo_ref[...]
    out_ref[rows, :] = _layer_norm(DEEPNORM_ALPHA * x + sub, lng_ref[...], lnb_ref[...])


def _merge(layer, x, ya, yb, vn, wb, bb, ws, bs, wbr, bo, lng, lnb):
    tm = MERGE_TM
    row = lambda i: (i, 0)
    vec = lambda width: _fixed_spec((None, 1, width), (layer, 0, 0))
    return pl.pallas_call(
        _merge_body,
        grid=(TOKENS // tm,),
        in_specs=[
            pl.BlockSpec((tm, D_MODEL), row),
            pl.BlockSpec((tm, NA_WIDTH), row),
            pl.BlockSpec((tm, GQA_WIDTH), row),
            pl.BlockSpec((tm, SG_WIDTH), row),
            _fixed_spec((None, D_MODEL, W_Z), (layer, 0, SGU_BLOCK)), _fixed_spec((None, 1, W_Z), (layer, 0, SGU_BLOCK)),
            _fixed_spec((None, D_MODEL, W_Z), (layer, 0, SGZ_BLOCK)), _fixed_spec((None, 1, W_Z), (layer, 0, SGZ_BLOCK)),
            _fixed_spec((None, SG_GROUPS // 2, 2 * SG_CHUNK, SG_CHUNK), (layer, 0, 0, 0)),
            _fixed_spec((None, SG_CHUNK, SG_WIDTH), (layer, 0, 0)),
            _fixed_spec((None, D_MODEL, N_BRANCH * D_MODEL), (layer, 0, GATE_BLOCK)),
            _fixed_spec((None, 1, N_BRANCH * D_MODEL), (layer, 0, GATE_BLOCK)),
            _fixed_spec((None, D_MODEL, D_MODEL), (layer, 0, 0)),
            _fixed_spec((None, NA_WIDTH, D_MODEL), (layer, 2, 0)),
            _fixed_spec((None, GQA_WIDTH, D_MODEL), (layer, 3, 0)),
            _fixed_spec((None, SG_WIDTH, D_MODEL), (layer, 4, 0)),
            vec(D_MODEL), vec(D_MODEL), vec(D_MODEL),
        ],
        out_specs=pl.BlockSpec((tm, D_MODEL), row),
        out_shape=jax.ShapeDtypeStruct((TOKENS, D_MODEL), F32),
        scratch_shapes=[pltpu.VMEM((tm, SG_WIDTH), F32)],
        compiler_params=_cparams(1),
        name="merge",
    )(x, ya, yb, vn, wb, bb, wb, bb, ws, bs, wb, bb, wbr, wbr, wbr, wbr, bo, lng, lnb)


def _rope_tables():
    t = np.arange(SEQ)
    row = (t // GRID_W).astype(np.float32)
    col = (t % GRID_W).astype(np.float32)
    freqs = np.float32(ROPE_THETA) ** (-np.arange(0, ROPE_AXIS_DIM, 2, dtype=np.float32) / np.float32(ROPE_AXIS_DIM))
    ang = np.concatenate([row[:, None] * freqs, col[:, None] * freqs], axis=-1).astype(np.float32)
    cos = np.cos(ang.astype(np.float64)).astype(np.float32)
    sin = np.sin(ang.astype(np.float64)).astype(np.float32)
    pair = (np.arange(LANES) % HEAD_DIM) // 2
    even = (np.arange(LANES) % 2 == 0)[None, :]
    cos_t = cos[:, pair]
    s1_t = np.where(even, -sin[:, pair], 0.0).astype(np.float32)
    s2_t = np.where(even, 0.0, sin[:, pair]).astype(np.float32)
    return jnp.asarray(cos_t), jnp.asarray(s1_t), jnp.asarray(s2_t)


def _reorder_columns(a, pad_value_shape):
    parts = []
    for seg in _W_SEGMENTS:
        if seg is None:
            parts.append(jnp.zeros(pad_value_shape + (W_PAD,), a.dtype))
        else:
            parts.append(a[..., seg[0]:seg[1]])
    return jnp.concatenate(parts, axis=-1)


def kernel(x, ln_in_g, ln_in_b, w_in, b_in, na_rpb, q_norm_g, k_norm_g, sg_ln_g, sg_ln_b, sg_w, sg_b,
           w_br_a, w_br_b, w_br_c, w_out, b_out, ln_post_g, ln_post_b):
    assert x.shape == (BATCH, SEQ, D_MODEL) and w_in.shape == (DEPTH, D_MODEL, O_END)
    cos_t, s1_t, s2_t = _rope_tables()
    bd = jnp.asarray(np.kron(np.eye(2), np.ones((HEAD_DIM, HEAD_DIM))), BF16)
    vec3 = lambda v: v.reshape(DEPTH, 1, -1)

    wb = _reorder_columns(w_in, (DEPTH, D_MODEL)).astype(BF16)
    bb = vec3(_reorder_columns(b_in, (DEPTH,)))
    wbr = jnp.concatenate([w_out, w_br_a, w_br_b, w_br_c], axis=1).astype(BF16)
    ws = sg_w.reshape(DEPTH, SG_GROUPS // 2, 2 * SG_CHUNK, SG_CHUNK).astype(BF16)
    bs = jnp.repeat(jnp.swapaxes(sg_b, 1, 2), HEAD_DIM, axis=2)
    gq = vec3(jnp.tile(q_norm_g * SCORE_SCALE, (1, 2)))
    gk = vec3(jnp.tile(k_norm_g, (1, 2)))
    sgg, sgb = vec3(sg_ln_g), vec3(sg_ln_b)
    bo, lng, lnb = vec3(b_out), vec3(ln_post_g), vec3(ln_post_b)
    bias = _na_bias(na_rpb.reshape(-1))

    h = x.reshape(TOKENS, D_MODEL)
    for l in range(DEPTH):
        proj_args = (wb, bb, gq, gk, bd, cos_t, s1_t, s2_t, sgg, sgb)
        if l == 0:
            h, naq, nak, nav, gqq, gqkt, gqv, vn = _proj(l, True, h, ln_in_g.reshape(1, -1), ln_in_b.reshape(1, -1), *proj_args)
        else:
            naq, nak, nav, gqq, gqkt, gqv, vn = _proj(l, False, h, None, None, *proj_args)
        ya = _na(l, naq, nak, nav, bias, h, wb, bb)
        yb = _gqa(l, gqq, gqkt, gqv, h, wb, bb)
        h = _merge(l, h, ya, yb, vn, wb, bb, ws, bs, wbr, bo, lng, lnb)
    return h.reshape(BATCH, SEQ, D_MODEL)
```

```python
import functools

import numpy as np
import jax
import jax.numpy as jnp
from jax import lax
from jax.experimental import pallas as pl
from jax.experimental.pallas import tpu as pltpu

F32 = jnp.float32
BF16 = jnp.bfloat16

D_MODEL = 1024
BATCH = 8
SEQ = 2048
DEPTH = 2
GRID_W = 64
ROWS = SEQ // GRID_W
HEAD_DIM = 64
NA_HEADS = 8
NA_WIDTH = 512
NA_KH = 8
NA_KW = 16
RPB_ROWS = 2 * NA_KH - 1
RPB_COLS = 2 * NA_KW - 1
GQA_HEADS = 8
GQA_KV_HEADS = 2
GQA_WIDTH = 512
GQA_KV_WIDTH = 128
ROPE_THETA = 10000.0
ROPE_AXIS_DIM = HEAD_DIM // 2
SG_WIDTH = 512
SG_GROUPS = 8
SG_CHUNK = 128
N_BRANCH = 3
LN_EPS = 1e-5
RMS_EPS = 1e-6
DEEPNORM_ALPHA = (2.0 * DEPTH) ** 0.25
ATTN_SCALE = HEAD_DIM ** -0.5
LOG2E = 1.4426950408889634
SCORE_SCALE = ATTN_SCALE * LOG2E
MASK_VALUE = -1e30

TOKENS = BATCH * SEQ
LANES = 128
VMEM_LIMIT = 56 * 1024 * 1024

_SPLITS = (NA_WIDTH,) * 4 + (GQA_WIDTH, GQA_KV_WIDTH, GQA_KV_WIDTH, GQA_WIDTH) + (SG_WIDTH,) * 3 + (N_BRANCH * D_MODEL,)
_OFFS = [int(v) for v in np.concatenate([[0], np.cumsum(_SPLITS)])]
(O_NAQ, O_NAK, O_NAV, O_NAZ, O_GQQ, O_GQK, O_GQV, O_GQZ, O_SGU, O_SGV, O_SGZ, O_GATE, O_END) = _OFFS

W_PROJ = 3072
W_Z = 512
_W_SEGMENTS = ((O_NAQ, O_NAZ), (O_GQQ, O_GQZ), (O_SGV, O_SGZ), None, (O_GATE, O_END),
               (O_NAZ, O_GQQ), (O_GQZ, O_SGU), (O_SGU, O_SGV), (O_SGZ, O_GATE))
W_PAD = 256
W_TOTAL = 8192
GATE_BLOCK = 1
NAZ_BLOCK, GQZ_BLOCK, SGU_BLOCK, SGZ_BLOCK = 12, 13, 14, 15
P_NAQ, P_NAK, P_NAV, P_GQQ, P_GQK, P_GQV, P_SGV = 0, 512, 1024, 1536, 2048, 2176, 2304

PROJ_TM = 1024
PROJ_SUB = 512
NA_ROWS = 4
NA_TQ = NA_ROWS * GRID_W
NA_WIN_ROWS = 12
NA_WIN = NA_WIN_ROWS * GRID_W
NA_GROUP_COUNT = ROWS // NA_ROWS
NA_GROUPS_PER_STEP = 2
NA_TYPES = 3
NA_SOFTMAX_ROWS = 32
GQA_TQ = 512
GQA_STAGE_ROWS = 256
GQA_SOFTMAX_ROWS = 16
MERGE_TM = 512
MERGE_SUB = 256


def _cparams(n_axes):
    return pltpu.CompilerParams(dimension_semantics=("arbitrary",) * n_axes, vmem_limit_bytes=VMEM_LIMIT)


def _fixed_spec(block_shape, index):
    return pl.BlockSpec(block_shape, lambda *_: index, pipeline_mode=pl.Buffered(1))


def _layer_norm(x, g, b):
    mu = jnp.mean(x, axis=-1, keepdims=True)
    xc = x - mu
    var = jnp.mean(xc * xc, axis=-1, keepdims=True)
    return xc * lax.rsqrt(var + LN_EPS) * g + b


def _lane_is_first_head(shape):
    return lax.broadcasted_iota(jnp.int32, shape, len(shape) - 1) % LANES < HEAD_DIM


def _na_window_start(step):
    return min(max(NA_ROWS * step - NA_KH // 2, 0), ROWS - NA_WIN_ROWS)


_NA_TYPE_STEPS = (0, 1, NA_GROUP_COUNT - 1)


def _na_tile_plan():
    plan = {}
    for t, step in enumerate(_NA_TYPE_STEPS):
        w0 = _na_window_start(step)
        for dq in range(NA_ROWS):
            r = NA_ROWS * step + dq
            row_start = min(max(r - NA_KH // 2, 0), ROWS - NA_KH)
            for j in range(NA_WIN_ROWS):
                key_row = w0 + j
                inside = row_start <= key_row < row_start + NA_KH
                plan[(t, dq, j)] = key_row - r + (NA_KH - 1) if inside else None
    return plan


def _na_bias_body(rpb_ref, out_ref):
    layer, pair = pl.program_id(0), pl.program_id(1)
    qi = lax.broadcasted_iota(jnp.int32, (GRID_W, GRID_W), 0)
    ki = lax.broadcasted_iota(jnp.int32, (GRID_W, GRID_W), 1)
    d = jnp.clip(ki - qi + (NA_KW - 1), 0, RPB_COLS - 1)
    col_start = jnp.clip(qi - NA_KW // 2, 0, GRID_W - NA_KW)
    col_valid = (ki >= col_start) & (ki < col_start + NA_KW)
    masked = jnp.full((GRID_W, GRID_W), MASK_VALUE, F32)
    plan = _na_tile_plan()
    for hsel in range(2):
        base = ((layer * NA_HEADS + 2 * pair + hsel) * RPB_ROWS) * RPB_COLS
        for ro in range(RPB_ROWS):
            users = [key for key, val in plan.items() if val == ro]
            if not users:
                continue
            tile = jnp.zeros((GRID_W, GRID_W), F32)
            for c in range(RPB_COLS):
                tile = jnp.where(d == c, rpb_ref[base + ro * RPB_COLS + c], tile)
            tile = jnp.where(col_valid, tile * LOG2E, MASK_VALUE)
            for (t, dq, j) in users:
                row0 = hsel * NA_TQ + dq * GRID_W
                out_ref[0, t, 0, row0:row0 + GRID_W, j * GRID_W:(j + 1) * GRID_W] = tile
        for (t, dq, j), val in plan.items():
            if val is None:
                row0 = hsel * NA_TQ + dq * GRID_W
                out_ref[0, t, 0, row0:row0 + GRID_W, j * GRID_W:(j + 1) * GRID_W] = masked


def _na_bias(rpb_flat):
    return pl.pallas_call(
        _na_bias_body,
        grid=(DEPTH, NA_HEADS // 2),
        in_specs=[pl.BlockSpec(memory_space=pltpu.SMEM)],
        out_specs=pl.BlockSpec((1, NA_TYPES, 1, 2 * NA_TQ, NA_WIN), lambda l, p: (l, 0, p, 0, 0)),
        out_shape=jax.ShapeDtypeStruct((DEPTH, NA_TYPES, NA_HEADS // 2, 2 * NA_TQ, NA_WIN), F32),
        compiler_params=_cparams(2),
        name="na_bias",
    )(rpb_flat)


def _proj_body(first, *refs):
    if first:
        (x_ref, lng_ref, lnb_ref, w_ref, b_ref, gq_ref, gk_ref, bd_ref, cos_ref, s1_ref, s2_ref,
         sgg_ref, sgb_ref, h_ref, naq_ref, nak_ref, nav_ref, gqq_ref, gqkt_ref, gqv_ref, vn_ref) = refs
    else:
        (x_ref, w_ref, b_ref, gq_ref, gk_ref, bd_ref, cos_ref, s1_ref, s2_ref,
         sgg_ref, sgb_ref, naq_ref, nak_ref, nav_ref, gqq_ref, gqkt_ref, gqv_ref, vn_ref) = refs
    for r0 in range(0, PROJ_TM, PROJ_SUB):
        _proj_rows(first, slice(r0, r0 + PROJ_SUB), x_ref, lng_ref if first else None, lnb_ref if first else None,
                   w_ref, b_ref, gq_ref, gk_ref, bd_ref, cos_ref, s1_ref, s2_ref, sgg_ref, sgb_ref,
                   h_ref if first else None, naq_ref, nak_ref, nav_ref, gqq_ref, gqkt_ref, gqv_ref, vn_ref)


def _proj_rows(first, rows, x_ref, lng_ref, lnb_ref, w_ref, b_ref, gq_ref, gk_ref, bd_ref, cos_ref, s1_ref, s2_ref,
               sgg_ref, sgb_ref, h_ref, naq_ref, nak_ref, nav_ref, gqq_ref, gqkt_ref, gqv_ref, vn_ref):
    x = x_ref[rows, :]
    if first:
        x = _layer_norm(x, lng_ref[...], lnb_ref[...])
        h_ref[rows, :] = x
    xb = x.astype(BF16)

    def proj(a, b):
        return jnp.dot(xb, w_ref[:, a:b], preferred_element_type=F32) + b_ref[:, a:b]

    def store_head_masked(dst_ref, val, col):
        first_head = _lane_is_first_head(val.shape)
        dst_ref[rows, col:col + LANES] = jnp.where(first_head, val, 0.0).astype(BF16)
        dst_ref[rows, NA_WIDTH + col:NA_WIDTH + col + LANES] = jnp.where(first_head, 0.0, val).astype(BF16)

    def store_with_ones(dst_ref, val, col, second_copy):
        first_head = _lane_is_first_head(val.shape)
        dst_ref[rows, col:col + LANES] = jnp.where(first_head, val, 1.0).astype(BF16)
        dst_ref[rows, second_copy + col:second_copy + col + LANES] = jnp.where(first_head, 1.0, val).astype(BF16)

    def rms_rope(val, gain):
        sq = val * val
        hi = sq.astype(BF16)
        lo = (sq - hi.astype(F32)).astype(BF16)
        ssum = (jnp.dot(hi, bd_ref[...], preferred_element_type=F32)
                + jnp.dot(lo, bd_ref[...], preferred_element_type=F32))
        xn = val * lax.rsqrt(ssum * (1.0 / HEAD_DIM) + RMS_EPS) * gain
        return (xn * cos_ref[rows, :] + pltpu.roll(xn, LANES - 1, 1) * s1_ref[rows, :]
                + pltpu.roll(xn, 1, 1) * s2_ref[rows, :])

    q = proj(P_NAQ, P_NAQ + NA_WIDTH) * SCORE_SCALE
    for p in range(NA_WIDTH // LANES):
        store_head_masked(naq_ref, q[:, p * LANES:(p + 1) * LANES], p * LANES)
    nak_ref[rows, :] = proj(P_NAK, P_NAK + NA_WIDTH).astype(BF16)
    nav_ref[rows, :] = proj(P_NAV, P_NAV + NA_WIDTH).astype(BF16)

    q = proj(P_GQQ, P_GQQ + GQA_WIDTH)
    for p in range(GQA_WIDTH // LANES):
        store_head_masked(gqq_ref, rms_rope(q[:, p * LANES:(p + 1) * LANES], gq_ref[...]), p * LANES)
    k = rms_rope(proj(P_GQK, P_GQK + GQA_KV_WIDTH), gk_ref[...])
    v = proj(P_GQV, P_GQV + GQA_KV_WIDTH)
    first_head = _lane_is_first_head(k.shape)
    k_sw = pltpu.roll(k, HEAD_DIM, 1)
    v_sw = pltpu.roll(v, HEAD_DIM, 1)
    gqkt_ref[0, 0, :, rows] = jnp.where(first_head, k, k_sw).T.astype(BF16)
    gqkt_ref[0, 1, :, rows] = jnp.where(first_head, k_sw, k).T.astype(BF16)
    store_with_ones(gqv_ref, jnp.where(first_head, v, v_sw), 0, LANES)
    store_with_ones(gqv_ref, jnp.where(first_head, v_sw, v), 2 * LANES, LANES)

    vn_ref[rows, :] = _layer_norm(proj(P_SGV, P_SGV + SG_WIDTH), sgg_ref[...], sgb_ref[...]).astype(BF16)


def _proj(layer, first, x, ln_g, ln_b, wb, bb, gq, gk, bd, cos_t, s1_t, s2_t, sgg, sgb):
    tm = PROJ_TM
    tiles_per_seq = SEQ // tm
    row = lambda i: (i, 0)
    pos = lambda i: (i % tiles_per_seq, 0)
    in_specs = [pl.BlockSpec((tm, D_MODEL), row)]
    args = [x]
    if first:
        in_specs += [_fixed_spec((1, D_MODEL), (0, 0)), _fixed_spec((1, D_MODEL), (0, 0))]
        args += [ln_g, ln_b]
    in_specs += [_fixed_spec((None, D_MODEL, W_PROJ), (layer, 0, 0)), _fixed_spec((None, 1, W_PROJ), (layer, 0, 0)),
                 _fixed_spec((None, 1, LANES), (layer, 0, 0)), _fixed_spec((None, 1, LANES), (layer, 0, 0)),
                 _fixed_spec((LANES, LANES), (0, 0)),
                 pl.BlockSpec((tm, LANES), pos), pl.BlockSpec((tm, LANES), pos), pl.BlockSpec((tm, LANES), pos),
                 _fixed_spec((None, 1, SG_WIDTH), (layer, 0, 0)), _fixed_spec((None, 1, SG_WIDTH), (layer, 0, 0))]
    args += [wb, bb, gq, gk, bd, cos_t, s1_t, s2_t, sgg, sgb]
    out_shape, out_specs = [], []
    if first:
        out_shape.append(jax.ShapeDtypeStruct((TOKENS, D_MODEL), F32))
        out_specs.append(pl.BlockSpec((tm, D_MODEL), row))
    out_shape += [
        jax.ShapeDtypeStruct((TOKENS, 2 * NA_WIDTH), BF16),
        jax.ShapeDtypeStruct((TOKENS, NA_WIDTH), BF16),
        jax.ShapeDtypeStruct((TOKENS, NA_WIDTH), BF16),
        jax.ShapeDtypeStruct((TOKENS, 2 * GQA_WIDTH), BF16),
        jax.ShapeDtypeStruct((BATCH, GQA_KV_HEADS, LANES, SEQ), BF16),
        jax.ShapeDtypeStruct((TOKENS, 2 * GQA_KV_HEADS * LANES), BF16),
        jax.ShapeDtypeStruct((TOKENS, SG_WIDTH), BF16),
    ]
    out_specs += [
        pl.BlockSpec((tm, 2 * NA_WIDTH), row),
        pl.BlockSpec((tm, NA_WIDTH), row),
        pl.BlockSpec((tm, NA_WIDTH), row),
        pl.BlockSpec((tm, 2 * GQA_WIDTH), row),
        pl.BlockSpec((1, GQA_KV_HEADS, LANES, tm), lambda i: (i // tiles_per_seq, 0, 0, i % tiles_per_seq)),
        pl.BlockSpec((tm, 2 * GQA_KV_HEADS * LANES), row),
        pl.BlockSpec((tm, SG_WIDTH), row),
    ]
    return pl.pallas_call(
        functools.partial(_proj_body, first),
        grid=(TOKENS // tm,),
        in_specs=in_specs,
        out_specs=out_specs,
        out_shape=out_shape,
        compiler_params=_cparams(1),
        name="proj_first" if first else "proj",
    )(*args)


def _silu_gate(x_ref, wz_ref, bz_ref):
    z = jnp.dot(x_ref[...].astype(BF16), wz_ref[...], preferred_element_type=F32) + bz_ref[...]
    return z * jax.nn.sigmoid(z)


def _softmax_numerators(s_ref, e_ref, n_rows, block_rows, l_ref=None):
    n_tiles = s_ref.shape[0]
    for r in range(0, n_rows, block_rows):
        rows = slice(r, r + block_rows)
        tile_max = s_ref[0, rows, :]
        for c in range(1, n_tiles):
            tile_max = jnp.maximum(tile_max, s_ref[c, rows, :])
        m = jnp.broadcast_to(jnp.max(tile_max, axis=-1, keepdims=True), (block_rows, LANES))
        tile_sum = None
        for c in range(n_tiles):
            e = jnp.exp2(s_ref[c, rows, :] - m)
            e_ref[c, rows, :] = e.astype(BF16)
            if l_ref is not None:
                tile_sum = e if tile_sum is None else tile_sum + e
        if l_ref is not None:
            l_ref[rows, :] = jnp.broadcast_to(jnp.sum(tile_sum, axis=-1, keepdims=True), (block_rows, LANES))


def _store_key_tiles(dst_ref, val):
    for c in range(dst_ref.shape[0]):
        dst_ref[c] = val[:, c * LANES:(c + 1) * LANES]


def _load_key_tiles(src_ref):
    return jnp.concatenate([src_ref[c] for c in range(src_ref.shape[0])], axis=1)


def _weighted_values(e, v_ones):
    o = jnp.dot(e, v_ones, preferred_element_type=F32)
    return o / pltpu.roll(o, HEAD_DIM, 1)


def _run_stages(n_stages, scores, finish, before_last):
    scores(0)
    for i in range(n_stages):
        if i + 1 < n_stages:
            scores(i + 1)
        else:
            before_last()
        finish(i)


def _na_body(q_ref, k_ref, v_ref, bias0_ref, bias1_ref, x_ref, wz_ref, bz_ref, out_ref,
             s_scr, e_scr, l_scr, y_scr, g_scr):
    step = pl.program_id(1)
    bias_refs = (bias0_ref, bias1_ref)
    first_head = _lane_is_first_head((NA_TQ, LANES))
    stages = [(g, p) for g in range(NA_GROUPS_PER_STEP) for p in range(NA_WIDTH // LANES)]

    def window(g):
        w0 = jnp.clip(NA_ROWS * (NA_GROUPS_PER_STEP * step + g) - NA_KH // 2, 0, ROWS - NA_WIN_ROWS)
        return pl.ds(pl.multiple_of(w0 * GRID_W, 2 * LANES), NA_WIN)

    def scores(i):
        g, p = stages[i]
        rows, col = slice(g * NA_TQ, (g + 1) * NA_TQ), p * LANES
        qs = jnp.concatenate([q_ref[rows, col:col + LANES], q_ref[rows, NA_WIDTH + col:NA_WIDTH + col + LANES]], axis=0)
        s = lax.dot_general(qs, k_ref[window(g), col:col + LANES], (((1,), (1,)), ((), ())), preferred_element_type=F32)
        _store_key_tiles(s_scr.at[i % 2], s + bias_refs[g][0, 0, p])

    def finish(i):
        g, p = stages[i]
        rows, col, slot = slice(g * NA_TQ, (g + 1) * NA_TQ), p * LANES, i % 2
        _softmax_numerators(s_scr.at[slot], e_scr.at[slot], 2 * NA_TQ, NA_SOFTMAX_ROWS, l_scr.at[slot])
        o = jnp.dot(_load_key_tiles(e_scr.at[slot]), v_ref[window(g), col:col + LANES],
                    preferred_element_type=F32) / l_scr[slot]
        y_scr[rows, col:col + LANES] = jnp.where(first_head, o[:NA_TQ], o[NA_TQ:])

    def gate():
        g_scr[...] = _silu_gate(x_ref, wz_ref, bz_ref)

    _run_stages(len(stages), scores, finish, gate)
    out_ref[...] = (y_scr[...] * g_scr[...]).astype(BF16)


def _na(layer, q2, k, v, bias, x, wb, bb):
    steps = NA_GROUP_COUNT // NA_GROUPS_PER_STEP
    tq = NA_GROUPS_PER_STEP * NA_TQ

    def bias_spec(g):
        def index(b, s):
            group = NA_GROUPS_PER_STEP * s + g
            return (layer, jnp.minimum(group, 1) + jnp.maximum(group - (NA_GROUP_COUNT - 2), 0), 0, 0, 0)
        return pl.BlockSpec((1, 1, NA_HEADS // 2, 2 * NA_TQ, NA_WIN), index)

    row = lambda b, s: (b * steps + s, 0)
    return pl.pallas_call(
        _na_body,
        grid=(BATCH, steps),
        in_specs=[
            pl.BlockSpec((tq, 2 * NA_WIDTH), row),
            pl.BlockSpec((SEQ, NA_WIDTH), lambda b, s: (b, 0)),
            pl.BlockSpec((SEQ, NA_WIDTH), lambda b, s: (b, 0)),
            bias_spec(0), bias_spec(1),
            pl.BlockSpec((tq, D_MODEL), row),
            _fixed_spec((None, D_MODEL, W_Z), (layer, 0, NAZ_BLOCK)),
            _fixed_spec((None, 1, W_Z), (layer, 0, NAZ_BLOCK)),
        ],
        out_specs=pl.BlockSpec((tq, NA_WIDTH), row),
        out_shape=jax.ShapeDtypeStruct((TOKENS, NA_WIDTH), BF16),
        scratch_shapes=[pltpu.VMEM((2, NA_WIN // LANES, 2 * NA_TQ, LANES), F32),
                        pltpu.VMEM((2, NA_WIN // LANES, 2 * NA_TQ, LANES), BF16),
                        pltpu.VMEM((2, 2 * NA_TQ, LANES), F32), pltpu.VMEM((tq, NA_WIDTH), F32),
                        pltpu.VMEM((tq, NA_WIDTH), F32)],
        compiler_params=_cparams(2),
        name="na",
    )(q2, k, v, bias, bias, x, wb, bb)


def _gqa_body(q_ref, kt_ref, v_ref, x_ref, wz_ref, bz_ref, out_ref, s_scr, e_scr, y_scr, g_scr):
    m = GQA_STAGE_ROWS
    tiles_per_kv = GQA_WIDTH // LANES // GQA_KV_HEADS
    first_head = _lane_is_first_head((m, LANES))
    stages = [(p, rb, h) for p in range(GQA_WIDTH // LANES) for rb in range(GQA_TQ // m) for h in range(2)]

    def scores(i):
        p, rb, h = stages[i]
        col = h * GQA_WIDTH + p * LANES
        _store_key_tiles(s_scr.at[i % 2], jnp.dot(q_ref[rb * m:(rb + 1) * m, col:col + LANES],
                                                  kt_ref[0, p // tiles_per_kv], preferred_element_type=F32))

    def finish(i):
        p, rb, h = stages[i]
        rows, col, slot, kv = slice(rb * m, (rb + 1) * m), p * LANES, i % 2, p // tiles_per_kv
        _softmax_numerators(s_scr.at[slot], e_scr.at[slot], m, GQA_SOFTMAX_ROWS)
        vcol = (2 * kv + h) * LANES
        o = _weighted_values(_load_key_tiles(e_scr.at[slot]), v_ref[:, vcol:vcol + LANES])
        if h == 0:
            y_scr[rows, col:col + LANES] = o
        else:
            y_scr[rows, col:col + LANES] = jnp.where(first_head, y_scr[rows, col:col + LANES], o)

    def gate():
        g_scr[...] = _silu_gate(x_ref, wz_ref, bz_ref)

    _run_stages(len(stages), scores, finish, gate)
    out_ref[...] = (y_scr[...] * g_scr[...]).astype(BF16)


def _gqa(layer, q2, kt, v, x, wb, bb):
    steps = SEQ // GQA_TQ
    row = lambda b, s: (b * steps + s, 0)
    return pl.pallas_call(
        _gqa_body,
        grid=(BATCH, steps),
        in_specs=[
            pl.BlockSpec((GQA_TQ, 2 * GQA_WIDTH), row),
            pl.BlockSpec((1, GQA_KV_HEADS, LANES, SEQ), lambda b, s: (b, 0, 0, 0)),
            pl.BlockSpec((SEQ, 2 * GQA_KV_HEADS * LANES), lambda b, s: (b, 0)),
            pl.BlockSpec((GQA_TQ, D_MODEL), row),
            _fixed_spec((None, D_MODEL, W_Z), (layer, 0, GQZ_BLOCK)),
            _fixed_spec((None, 1, W_Z), (layer, 0, GQZ_BLOCK)),
        ],
        out_specs=pl.BlockSpec((GQA_TQ, GQA_WIDTH), row),
        out_shape=jax.ShapeDtypeStruct((TOKENS, GQA_WIDTH), BF16),
        scratch_shapes=[pltpu.VMEM((2, SEQ // LANES, GQA_STAGE_ROWS, LANES), F32),
                        pltpu.VMEM((2, SEQ // LANES, GQA_STAGE_ROWS, LANES), BF16),
                        pltpu.VMEM((GQA_TQ, GQA_WIDTH), F32), pltpu.VMEM((GQA_TQ, GQA_WIDTH), F32)],
        compiler_params=_cparams(2),
        name="gqa",
    )(q2, kt, v, x, wb, bb)


def _merge_body(x_ref, ya_ref, yb_ref, vn_ref, wu_ref, bu_ref, wz_ref, bz_ref, ws_ref, bs_ref,
                wg_ref, bg_ref, wo_ref, wa_ref, wb_ref, wc_ref, bo_ref, lng_ref, lnb_ref, out_ref, mixed_ref):
    for r0 in range(0, MERGE_TM, MERGE_SUB):
        _merge_rows(slice(r0, r0 + MERGE_SUB), x_ref, ya_ref, yb_ref, vn_ref, wu_ref, bu_ref, wz_ref, bz_ref, ws_ref,
                    bs_ref, wg_ref, bg_ref, wo_ref, wa_ref, wb_ref, wc_ref, bo_ref, lng_ref, lnb_ref, out_ref, mixed_ref)


def _merge_rows(rows, x_ref, ya_ref, yb_ref, vn_ref, wu_ref, bu_ref, wz_ref, bz_ref, ws_ref, bs_ref,
                wg_ref, bg_ref, wo_ref, wa_ref, wb_ref, wc_ref, bo_ref, lng_ref, lnb_ref, out_ref, mixed_ref):
    n_chunks = MERGE_SUB // SG_CHUNK
    x = x_ref[rows, :]
    xb = x.astype(BF16)

    first_head = _lane_is_first_head((SG_CHUNK, LANES))
    chunk_rows = [slice(rows.start + c * SG_CHUNK, rows.start + (c + 1) * SG_CHUNK) for c in range(n_chunks)]
    for pp in range(SG_WIDTH // LANES):
        col = pp * LANES
        rhs = jnp.concatenate([vn_ref[cr, col:col + LANES] for cr in chunk_rows], axis=1)
        res = jnp.dot(ws_ref[pp], rhs, preferred_element_type=F32)
        for c, cr in enumerate(chunk_rows):
            blk = jnp.where(first_head, res[:SG_CHUNK, c * LANES:(c + 1) * LANES], res[SG_CHUNK:, c * LANES:(c + 1) * LANES])
            mixed_ref[cr, col:col + LANES] = blk + bs_ref[:, col:col + LANES]
    u = jnp.dot(xb, wu_ref[...], preferred_element_type=F32) + bu_ref[...]
    z = jnp.dot(xb, wz_ref[...], preferred_element_type=F32) + bz_ref[...]
    tc = (u * mixed_ref[rows, :] * (z * jax.nn.sigmoid(z))).astype(BF16)

    def gate(i):
        lo = i * D_MODEL
        return jax.nn.sigmoid(jnp.dot(xb, wg_ref[:, lo:lo + D_MODEL], preferred_element_type=F32) + bg_ref[:, lo:lo + D_MODEL])

    merged = gate(0) * jnp.dot(ya_ref[rows, :], wa_ref[...], preferred_element_type=F32)
    merged = merged + gate(1) * jnp.dot(yb_ref[rows, :], wb_ref[...], preferred_element_type=F32)
    merged = merged + gate(2) * jnp.dot(tc, wc_ref[...], preferred_element_type=F32)
    sub = jnp.dot(merged.astype(BF16), wo_ref[...], preferred_element_type=F32) + bo_ref[...]
    out_ref[rows, :] = _layer_norm(DEEPNORM_ALPHA * x + sub, lng_ref[...], lnb_ref[...])


def _merge(layer, x, ya, yb, vn, wb, bb, ws, bs, wbr, bo, lng, lnb):
    tm = MERGE_TM
    row = lambda i: (i, 0)
    vec = lambda width: _fixed_spec((None, 1, width), (layer, 0, 0))
    return pl.pallas_call(
        _merge_body,
        grid=(TOKENS // tm,),
        in_specs=[
            pl.BlockSpec((tm, D_MODEL), row),
            pl.BlockSpec((tm, NA_WIDTH), row),
            pl.BlockSpec((tm, GQA_WIDTH), row),
            pl.BlockSpec((tm, SG_WIDTH), row),
            _fixed_spec((None, D_MODEL, W_Z), (layer, 0, SGU_BLOCK)), _fixed_spec((None, 1, W_Z), (layer, 0, SGU_BLOCK)),
            _fixed_spec((None, D_MODEL, W_Z), (layer, 0, SGZ_BLOCK)), _fixed_spec((None, 1, W_Z), (layer, 0, SGZ_BLOCK)),
            _fixed_spec((None, SG_GROUPS // 2, 2 * SG_CHUNK, SG_CHUNK), (layer, 0, 0, 0)),
            _fixed_spec((None, SG_CHUNK, SG_WIDTH), (layer, 0, 0)),
            _fixed_spec((None, D_MODEL, N_BRANCH * D_MODEL), (layer, 0, GATE_BLOCK)),
            _fixed_spec((None, 1, N_BRANCH * D_MODEL), (layer, 0, GATE_BLOCK)),
            _fixed_spec((None, D_MODEL, D_MODEL), (layer, 0, 0)),
            _fixed_spec((None, NA_WIDTH, D_MODEL), (layer, 2, 0)),
            _fixed_spec((None, GQA_WIDTH, D_MODEL), (layer, 3, 0)),
            _fixed_spec((None, SG_WIDTH, D_MODEL), (layer, 4, 0)),
            vec(D_MODEL), vec(D_MODEL), vec(D_MODEL),
        ],
        out_specs=pl.BlockSpec((tm, D_MODEL), row),
        out_shape=jax.ShapeDtypeStruct((TOKENS, D_MODEL), F32),
        scratch_shapes=[pltpu.VMEM((tm, SG_WIDTH), F32)],
        compiler_params=_cparams(1),
        name="merge",
    )(x, ya, yb, vn, wb, bb, wb, bb, ws, bs, wb, bb, wbr, wbr, wbr, wbr, bo, lng, lnb)


def _rope_tables():
    t = np.arange(SEQ)
    row = (t // GRID_W).astype(np.float32)
    col = (t % GRID_W).astype(np.float32)
    freqs = np.float32(ROPE_THETA) ** (-np.arange(0, ROPE_AXIS_DIM, 2, dtype=np.float32) / np.float32(ROPE_AXIS_DIM))
    ang = np.concatenate([row[:, None] * freqs, col[:, None] * freqs], axis=-1).astype(np.float32)
    cos = np.cos(ang.astype(np.float64)).astype(np.float32)
    sin = np.sin(ang.astype(np.float64)).astype(np.float32)
    pair = (np.arange(LANES) % HEAD_DIM) // 2
    even = (np.arange(LANES) % 2 == 0)[None, :]
    cos_t = cos[:, pair]
    s1_t = np.where(even, -sin[:, pair], 0.0).astype(np.float32)
    s2_t = np.where(even, 0.0, sin[:, pair]).astype(np.float32)
    return jnp.asarray(cos_t), jnp.asarray(s1_t), jnp.asarray(s2_t)


def _reorder_columns(a, pad_value_shape):
    parts = []
    for seg in _W_SEGMENTS:
        if seg is None:
            parts.append(jnp.zeros(pad_value_shape + (W_PAD,), a.dtype))
        else:
            parts.append(a[..., seg[0]:seg[1]])
    return jnp.concatenate(parts, axis=-1)


def kernel(x, ln_in_g, ln_in_b, w_in, b_in, na_rpb, q_norm_g, k_norm_g, sg_ln_g, sg_ln_b, sg_w, sg_b,
           w_br_a, w_br_b, w_br_c, w_out, b_out, ln_post_g, ln_post_b):
    assert x.shape == (BATCH, SEQ, D_MODEL) and w_in.shape == (DEPTH, D_MODEL, O_END)
    cos_t, s1_t, s2_t = _rope_tables()
    bd = jnp.asarray(np.kron(np.eye(2), np.ones((HEAD_DIM, HEAD_DIM))), BF16)
    vec3 = lambda v: v.reshape(DEPTH, 1, -1)

    wb = _reorder_columns(w_in, (DEPTH, D_MODEL)).astype(BF16)
    bb = vec3(_reorder_columns(b_in, (DEPTH,)))
    wbr = jnp.concatenate([w_out, w_br_a, w_br_b, w_br_c], axis=1).astype(BF16)
    ws = sg_w.reshape(DEPTH, SG_GROUPS // 2, 2 * SG_CHUNK, SG_CHUNK).astype(BF16)
    bs = jnp.repeat(jnp.swapaxes(sg_b, 1, 2), HEAD_DIM, axis=2)
    gq = vec3(jnp.tile(q_norm_g * SCORE_SCALE, (1, 2)))
    gk = vec3(jnp.tile(k_norm_g, (1, 2)))
    sgg, sgb = vec3(sg_ln_g), vec3(sg_ln_b)
    bo, lng, lnb = vec3(b_out), vec3(ln_post_g), vec3(ln_post_b)
    bias = _na_bias(na_rpb.reshape(-1))

    h = x.reshape(TOKENS, D_MODEL)
    for l in range(DEPTH):
        proj_args = (wb, bb, gq, gk, bd, cos_t, s1_t, s2_t, sgg, sgb)
        if l == 0:
            h, naq, nak, nav, gqq, gqkt, gqv, vn = _proj(l, True, h, ln_in_g.reshape(1, -1), ln_in_b.reshape(1, -1), *proj_args)
        else:
            naq, nak, nav, gqq, gqkt, gqv, vn = _proj(l, False, h, None, None, *proj_args)
        ya = _na(l, naq, nak, nav, bias, h, wb, bb)
        yb = _gqa(l, gqq, gqkt, gqv, h, wb, bb)
        h = _merge(l, h, ya, yb, vn, wb, bb, ws, bs, wbr, bo, lng, lnb)
    return h.reshape(BATCH, SEQ, D_MODEL)
```

```python
import functools

import numpy as np
import jax
import jax.numpy as jnp
from jax import lax
from jax.experimental import pallas as pl
from jax.experimental.pallas import tpu as pltpu

F32 = jnp.float32
BF16 = jnp.bfloat16

D_MODEL = 1024
BATCH = 8
SEQ = 2048
DEPTH = 2
GRID_W = 64
ROWS = SEQ // GRID_W
HEAD_DIM = 64
NA_HEADS = 8
NA_WIDTH = 512
NA_KH = 8
NA_KW = 16
RPB_ROWS = 2 * NA_KH - 1
RPB_COLS = 2 * NA_KW - 1
GQA_HEADS = 8
GQA_KV_HEADS = 2
GQA_WIDTH = 512
GQA_KV_WIDTH = 128
ROPE_THETA = 10000.0
ROPE_AXIS_DIM = HEAD_DIM // 2
SG_WIDTH = 512
SG_GROUPS = 8
SG_CHUNK = 128
N_BRANCH = 3
LN_EPS = 1e-5
RMS_EPS = 1e-6
DEEPNORM_ALPHA = (2.0 * DEPTH) ** 0.25
ATTN_SCALE = HEAD_DIM ** -0.5
LOG2E = 1.4426950408889634
SCORE_SCALE = ATTN_SCALE * LOG2E
MASK_VALUE = -1e30

TOKENS = BATCH * SEQ
LANES = 128
VMEM_LIMIT = 56 * 1024 * 1024

_SPLITS = (NA_WIDTH,) * 4 + (GQA_WIDTH, GQA_KV_WIDTH, GQA_KV_WIDTH, GQA_WIDTH) + (SG_WIDTH,) * 3 + (N_BRANCH * D_MODEL,)
_OFFS = [int(v) for v in np.concatenate([[0], np.cumsum(_SPLITS)])]
(O_NAQ, O_NAK, O_NAV, O_NAZ, O_GQQ, O_GQK, O_GQV, O_GQZ, O_SGU, O_SGV, O_SGZ, O_GATE, O_END) = _OFFS

PROJ_SEGMENTS = ((O_NAQ, O_NAZ), (O_GQQ, O_GQZ), (O_SGV, O_SGZ))
P_NAQ, P_NAK, P_NAV = (0, 0), (0, NA_WIDTH), (0, 2 * NA_WIDTH)
P_GQQ, P_GQK, P_GQV = (1, 0), (1, GQA_WIDTH), (1, GQA_WIDTH + GQA_KV_WIDTH)
P_SGV = (2, 0)

PROJ_TM = 1024
PROJ_SUB = 512
NA_ROWS = 4
NA_TQ = NA_ROWS * GRID_W
NA_WIN_ROWS = 12
NA_WIN = NA_WIN_ROWS * GRID_W
NA_GROUP_COUNT = ROWS // NA_ROWS
NA_GROUPS_PER_STEP = 2
NA_TYPES = 3
NA_SOFTMAX_ROWS = 32
GQA_TQ = 512
GQA_STAGE_ROWS = 256
GQA_SOFTMAX_ROWS = 16
MERGE_TM = 512
MERGE_SUB = 256


def _cparams(n_axes):
    return pltpu.CompilerParams(dimension_semantics=("arbitrary",) * n_axes, vmem_limit_bytes=VMEM_LIMIT)


def _fixed_spec(block_shape, index):
    return pl.BlockSpec(block_shape, lambda *_: index, pipeline_mode=pl.Buffered(1))


def _column_spec(rows, layer, start, stop):
    return _fixed_spec((pl.Element(1), pl.Element(rows), pl.Element(stop - start)), (layer, 0, start))


def _layer_norm(x, g, b):
    mu = jnp.mean(x, axis=-1, keepdims=True)
    xc = x - mu
    var = jnp.mean(xc * xc, axis=-1, keepdims=True)
    return xc * lax.rsqrt(var + LN_EPS) * g + b


def _lane_is_first_head(shape):
    return lax.broadcasted_iota(jnp.int32, shape, len(shape) - 1) % LANES < HEAD_DIM


def _na_window_start(step):
    return min(max(NA_ROWS * step - NA_KH // 2, 0), ROWS - NA_WIN_ROWS)


_NA_TYPE_STEPS = (0, 1, NA_GROUP_COUNT - 1)


def _na_tile_plan():
    plan = {}
    for t, step in enumerate(_NA_TYPE_STEPS):
        w0 = _na_window_start(step)
        for dq in range(NA_ROWS):
            r = NA_ROWS * step + dq
            row_start = min(max(r - NA_KH // 2, 0), ROWS - NA_KH)
            for j in range(NA_WIN_ROWS):
                key_row = w0 + j
                inside = row_start <= key_row < row_start + NA_KH
                plan[(t, dq, j)] = key_row - r + (NA_KH - 1) if inside else None
    return plan


def _na_bias_body(rpb_ref, out_ref):
    layer, pair = pl.program_id(0), pl.program_id(1)
    qi = lax.broadcasted_iota(jnp.int32, (GRID_W, GRID_W), 0)
    ki = lax.broadcasted_iota(jnp.int32, (GRID_W, GRID_W), 1)
    d = jnp.clip(ki - qi + (NA_KW - 1), 0, RPB_COLS - 1)
    col_start = jnp.clip(qi - NA_KW // 2, 0, GRID_W - NA_KW)
    col_valid = (ki >= col_start) & (ki < col_start + NA_KW)
    masked = jnp.full((GRID_W, GRID_W), MASK_VALUE, F32)
    plan = _na_tile_plan()
    for hsel in range(2):
        base = ((layer * NA_HEADS + 2 * pair + hsel) * RPB_ROWS) * RPB_COLS
        for ro in range(RPB_ROWS):
            users = [key for key, val in plan.items() if val == ro]
            if not users:
                continue
            tile = jnp.zeros((GRID_W, GRID_W), F32)
            for c in range(RPB_COLS):
                tile = jnp.where(d == c, rpb_ref[base + ro * RPB_COLS + c], tile)
            tile = jnp.where(col_valid, tile * LOG2E, MASK_VALUE)
            for (t, dq, j) in users:
                row0 = hsel * NA_TQ + dq * GRID_W
                out_ref[0, t, 0, row0:row0 + GRID_W, j * GRID_W:(j + 1) * GRID_W] = tile
        for (t, dq, j), val in plan.items():
            if val is None:
                row0 = hsel * NA_TQ + dq * GRID_W
                out_ref[0, t, 0, row0:row0 + GRID_W, j * GRID_W:(j + 1) * GRID_W] = masked


def _na_bias(rpb_flat):
    return pl.pallas_call(
        _na_bias_body,
        grid=(DEPTH, NA_HEADS // 2),
        in_specs=[pl.BlockSpec(memory_space=pltpu.SMEM)],
        out_specs=pl.BlockSpec((1, NA_TYPES, 1, 2 * NA_TQ, NA_WIN), lambda l, p: (l, 0, p, 0, 0)),
        out_shape=jax.ShapeDtypeStruct((DEPTH, NA_TYPES, NA_HEADS // 2, 2 * NA_TQ, NA_WIN), F32),
        compiler_params=_cparams(2),
        name="na_bias",
    )(rpb_flat)


def _proj_body(first, *refs):
    if first:
        (x_ref, lng_ref, lnb_ref, w0_ref, w1_ref, w2_ref, b0_ref, b1_ref, b2_ref, gq_ref, gk_ref, bd_ref,
         cos_ref, s1_ref, s2_ref, sgg_ref, sgb_ref,
         h_ref, naq_ref, nak_ref, nav_ref, gqq_ref, gqkt_ref, gqv_ref, vn_ref) = refs
    else:
        (x_ref, w0_ref, w1_ref, w2_ref, b0_ref, b1_ref, b2_ref, gq_ref, gk_ref, bd_ref,
         cos_ref, s1_ref, s2_ref, sgg_ref, sgb_ref,
         naq_ref, nak_ref, nav_ref, gqq_ref, gqkt_ref, gqv_ref, vn_ref) = refs
    w_refs = tuple(r.at[0] for r in (w0_ref, w1_ref, w2_ref))
    b_refs = tuple(r.at[0] for r in (b0_ref, b1_ref, b2_ref))
    for r0 in range(0, PROJ_TM, PROJ_SUB):
        _proj_rows(first, slice(r0, r0 + PROJ_SUB), x_ref, lng_ref if first else None, lnb_ref if first else None,
                   w_refs, b_refs, gq_ref, gk_ref, bd_ref, cos_ref, s1_ref, s2_ref, sgg_ref, sgb_ref,
                   h_ref if first else None, naq_ref, nak_ref, nav_ref, gqq_ref, gqkt_ref, gqv_ref, vn_ref)


def _proj_rows(first, rows, x_ref, lng_ref, lnb_ref, w_refs, b_refs, gq_ref, gk_ref, bd_ref, cos_ref, s1_ref, s2_ref,
               sgg_ref, sgb_ref, h_ref, naq_ref, nak_ref, nav_ref, gqq_ref, gqkt_ref, gqv_ref, vn_ref):
    x = x_ref[rows, :]
    if first:
        x = _layer_norm(x, lng_ref[...], lnb_ref[...])
        h_ref[rows, :] = x
    xb = x.astype(BF16)

    def proj(where, width):
        seg, col = where
        return (jnp.dot(xb, w_refs[seg][:, col:col + width], preferred_element_type=F32)
                + b_refs[seg][:, col:col + width])

    def store_head_masked(dst_ref, val, col):
        first_head = _lane_is_first_head(val.shape)
        dst_ref[rows, col:col + LANES] = jnp.where(first_head, val, 0.0).astype(BF16)
        dst_ref[rows, NA_WIDTH + col:NA_WIDTH + col + LANES] = jnp.where(first_head, 0.0, val).astype(BF16)

    def store_with_ones(dst_ref, val, col, second_copy):
        first_head = _lane_is_first_head(val.shape)
        dst_ref[rows, col:col + LANES] = jnp.where(first_head, val, 1.0).astype(BF16)
        dst_ref[rows, second_copy + col:second_copy + col + LANES] = jnp.where(first_head, 1.0, val).astype(BF16)

    def rms_rope(val, gain):
        sq = val * val
        hi = sq.astype(BF16)
        lo = (sq - hi.astype(F32)).astype(BF16)
        ssum = (jnp.dot(hi, bd_ref[...], preferred_element_type=F32)
                + jnp.dot(lo, bd_ref[...], preferred_element_type=F32))
        xn = val * lax.rsqrt(ssum * (1.0 / HEAD_DIM) + RMS_EPS) * gain
        return (xn * cos_ref[rows, :] + pltpu.roll(xn, LANES - 1, 1) * s1_ref[rows, :]
                + pltpu.roll(xn, 1, 1) * s2_ref[rows, :])

    q = proj(P_NAQ, NA_WIDTH) * SCORE_SCALE
    for p in range(NA_WIDTH // LANES):
        store_head_masked(naq_ref, q[:, p * LANES:(p + 1) * LANES], p * LANES)
    nak_ref[rows, :] = proj(P_NAK, NA_WIDTH).astype(BF16)
    nav_ref[rows, :] = proj(P_NAV, NA_WIDTH).astype(BF16)

    q = proj(P_GQQ, GQA_WIDTH)
    for p in range(GQA_WIDTH // LANES):
        store_head_masked(gqq_ref, rms_rope(q[:, p * LANES:(p + 1) * LANES], gq_ref[...]), p * LANES)
    k = rms_rope(proj(P_GQK, GQA_KV_WIDTH), gk_ref[...])
    v = proj(P_GQV, GQA_KV_WIDTH)
    first_head = _lane_is_first_head(k.shape)
    k_sw = pltpu.roll(k, HEAD_DIM, 1)
    v_sw = pltpu.roll(v, HEAD_DIM, 1)
    gqkt_ref[0, 0, :, rows] = jnp.where(first_head, k, k_sw).T.astype(BF16)
    gqkt_ref[0, 1, :, rows] = jnp.where(first_head, k_sw, k).T.astype(BF16)
    store_with_ones(gqv_ref, jnp.where(first_head, v, v_sw), 0, LANES)
    store_with_ones(gqv_ref, jnp.where(first_head, v_sw, v), 2 * LANES, LANES)

    vn_ref[rows, :] = _layer_norm(proj(P_SGV, SG_WIDTH), sgg_ref[...], sgb_ref[...]).astype(BF16)


def _proj(layer, first, x, ln_g, ln_b, wb, bb, gq, gk, bd, cos_t, s1_t, s2_t, sgg, sgb):
    tm = PROJ_TM
    tiles_per_seq = SEQ // tm
    row = lambda i: (i, 0)
    pos = lambda i: (i % tiles_per_seq, 0)
    in_specs = [pl.BlockSpec((tm, D_MODEL), row)]
    args = [x]
    if first:
        in_specs += [_fixed_spec((1, D_MODEL), (0, 0)), _fixed_spec((1, D_MODEL), (0, 0))]
        args += [ln_g, ln_b]
    in_specs += [_column_spec(D_MODEL, layer, a, b) for a, b in PROJ_SEGMENTS]
    in_specs += [_column_spec(1, layer, a, b) for a, b in PROJ_SEGMENTS]
    in_specs += [_fixed_spec((None, 1, LANES), (layer, 0, 0)), _fixed_spec((None, 1, LANES), (layer, 0, 0)),
                 _fixed_spec((LANES, LANES), (0, 0)),
                 pl.BlockSpec((tm, LANES), pos), pl.BlockSpec((tm, LANES), pos), pl.BlockSpec((tm, LANES), pos),
                 _fixed_spec((None, 1, SG_WIDTH), (layer, 0, 0)), _fixed_spec((None, 1, SG_WIDTH), (layer, 0, 0))]
    args += [wb] * len(PROJ_SEGMENTS) + [bb] * len(PROJ_SEGMENTS) + [gq, gk, bd, cos_t, s1_t, s2_t, sgg, sgb]
    out_shape, out_specs = [], []
    if first:
        out_shape.append(jax.ShapeDtypeStruct((TOKENS, D_MODEL), F32))
        out_specs.append(pl.BlockSpec((tm, D_MODEL), row))
    out_shape += [
        jax.ShapeDtypeStruct((TOKENS, 2 * NA_WIDTH), BF16),
        jax.ShapeDtypeStruct((TOKENS, NA_WIDTH), BF16),
        jax.ShapeDtypeStruct((TOKENS, NA_WIDTH), BF16),
        jax.ShapeDtypeStruct((TOKENS, 2 * GQA_WIDTH), BF16),
        jax.ShapeDtypeStruct((BATCH, GQA_KV_HEADS, LANES, SEQ), BF16),
        jax.ShapeDtypeStruct((TOKENS, 2 * GQA_KV_HEADS * LANES), BF16),
        jax.ShapeDtypeStruct((TOKENS, SG_WIDTH), BF16),
    ]
    out_specs += [
        pl.BlockSpec((tm, 2 * NA_WIDTH), row),
        pl.BlockSpec((tm, NA_WIDTH), row),
        pl.BlockSpec((tm, NA_WIDTH), row),
        pl.BlockSpec((tm, 2 * GQA_WIDTH), row),
        pl.BlockSpec((1, GQA_KV_HEADS, LANES, tm), lambda i: (i // tiles_per_seq, 0, 0, i % tiles_per_seq)),
        pl.BlockSpec((tm, 2 * GQA_KV_HEADS * LANES), row),
        pl.BlockSpec((tm, SG_WIDTH), row),
    ]
    return pl.pallas_call(
        functools.partial(_proj_body, first),
        grid=(TOKENS // tm,),
        in_specs=in_specs,
        out_specs=out_specs,
        out_shape=out_shape,
        compiler_params=_cparams(1),
        name="proj_first" if first else "proj",
    )(*args)


def _silu_gate(x_ref, wz_ref, bz_ref):
    z = jnp.dot(x_ref[...].astype(BF16), wz_ref[0], preferred_element_type=F32) + bz_ref[0]
    return z * jax.nn.sigmoid(z)


def _softmax_numerators(s_ref, e_ref, n_rows, block_rows, l_ref=None):
    for r in range(0, n_rows, block_rows):
        rows = slice(r, r + block_rows)
        blk = s_ref[rows, :]
        e = jnp.exp2(blk - jnp.max(blk, axis=-1, keepdims=True))
        e_ref[rows, :] = e.astype(BF16)
        if l_ref is not None:
            l_ref[rows, :] = jnp.broadcast_to(jnp.sum(e, axis=-1, keepdims=True), (block_rows, LANES))


def _weighted_values(e, v_ones):
    o = jnp.dot(e, v_ones, preferred_element_type=F32)
    return o / pltpu.roll(o, HEAD_DIM, 1)


def _run_stages(n_stages, scores, finish, before_last):
    scores(0)
    for i in range(n_stages):
        if i + 1 < n_stages:
            scores(i + 1)
        else:
            before_last()
        finish(i)


def _na_body(q_ref, k_ref, v_ref, bias0_ref, bias1_ref, x_ref, wz_ref, bz_ref, out_ref,
             s_scr, e_scr, l_scr, y_scr, g_scr):
    step = pl.program_id(1)
    bias_refs = (bias0_ref, bias1_ref)
    first_head = _lane_is_first_head((NA_TQ, LANES))
    stages = [(g, p) for g in range(NA_GROUPS_PER_STEP) for p in range(NA_WIDTH // LANES)]

    def window(g):
        w0 = jnp.clip(NA_ROWS * (NA_GROUPS_PER_STEP * step + g) - NA_KH // 2, 0, ROWS - NA_WIN_ROWS)
        return pl.ds(pl.multiple_of(w0 * GRID_W, 2 * LANES), NA_WIN)

    def scores(i):
        g, p = stages[i]
        rows, col = slice(g * NA_TQ, (g + 1) * NA_TQ), p * LANES
        qs = jnp.concatenate([q_ref[rows, col:col + LANES], q_ref[rows, NA_WIDTH + col:NA_WIDTH + col + LANES]], axis=0)
        s = lax.dot_general(qs, k_ref[window(g), col:col + LANES], (((1,), (1,)), ((), ())), preferred_element_type=F32)
        s_scr[i % 2] = s + bias_refs[g][0, 0, p]

    def finish(i):
        g, p = stages[i]
        rows, col, slot = slice(g * NA_TQ, (g + 1) * NA_TQ), p * LANES, i % 2
        _softmax_numerators(s_scr.at[slot], e_scr.at[slot], 2 * NA_TQ, NA_SOFTMAX_ROWS, l_scr.at[slot])
        o = jnp.dot(e_scr[slot], v_ref[window(g), col:col + LANES], preferred_element_type=F32) / l_scr[slot]
        y_scr[rows, col:col + LANES] = jnp.where(first_head, o[:NA_TQ], o[NA_TQ:])

    def gate():
        g_scr[...] = _silu_gate(x_ref, wz_ref, bz_ref)

    _run_stages(len(stages), scores, finish, gate)
    out_ref[...] = (y_scr[...] * g_scr[...]).astype(BF16)


def _na(layer, q2, k, v, bias, x, wb, bb):
    steps = NA_GROUP_COUNT // NA_GROUPS_PER_STEP
    tq = NA_GROUPS_PER_STEP * NA_TQ

    def bias_spec(g):
        def index(b, s):
            group = NA_GROUPS_PER_STEP * s + g
            return (layer, jnp.minimum(group, 1) + jnp.maximum(group - (NA_GROUP_COUNT - 2), 0), 0, 0, 0)
        return pl.BlockSpec((1, 1, NA_HEADS // 2, 2 * NA_TQ, NA_WIN), index)

    row = lambda b, s: (b * steps + s, 0)
    return pl.pallas_call(
        _na_body,
        grid=(BATCH, steps),
        in_specs=[
            pl.BlockSpec((tq, 2 * NA_WIDTH), row),
            pl.BlockSpec((SEQ, NA_WIDTH), lambda b, s: (b, 0)),
            pl.BlockSpec((SEQ, NA_WIDTH), lambda b, s: (b, 0)),
            bias_spec(0), bias_spec(1),
            pl.BlockSpec((tq, D_MODEL), row),
            _column_spec(D_MODEL, layer, O_NAZ, O_GQQ),
            _column_spec(1, layer, O_NAZ, O_GQQ),
        ],
        out_specs=pl.BlockSpec((tq, NA_WIDTH), row),
        out_shape=jax.ShapeDtypeStruct((TOKENS, NA_WIDTH), BF16),
        scratch_shapes=[pltpu.VMEM((2, 2 * NA_TQ, NA_WIN), F32), pltpu.VMEM((2, 2 * NA_TQ, NA_WIN), BF16),
                        pltpu.VMEM((2, 2 * NA_TQ, LANES), F32), pltpu.VMEM((tq, NA_WIDTH), F32),
                        pltpu.VMEM((tq, NA_WIDTH), F32)],
        compiler_params=_cparams(2),
        name="na",
    )(q2, k, v, bias, bias, x, wb, bb)


def _gqa_body(q_ref, kt_ref, v_ref, x_ref, wz_ref, bz_ref, out_ref, s_scr, e_scr, y_scr, g_scr):
    m = GQA_STAGE_ROWS
    tiles_per_kv = GQA_WIDTH // LANES // GQA_KV_HEADS
    first_head = _lane_is_first_head((m, LANES))
    stages = [(p, rb, h) for p in range(GQA_WIDTH // LANES) for rb in range(GQA_TQ // m) for h in range(2)]

    def scores(i):
        p, rb, h = stages[i]
        col = h * GQA_WIDTH + p * LANES
        s_scr[i % 2] = jnp.dot(q_ref[rb * m:(rb + 1) * m, col:col + LANES], kt_ref[0, p // tiles_per_kv],
                               preferred_element_type=F32)

    def finish(i):
        p, rb, h = stages[i]
        rows, col, slot, kv = slice(rb * m, (rb + 1) * m), p * LANES, i % 2, p // tiles_per_kv
        _softmax_numerators(s_scr.at[slot], e_scr.at[slot], m, GQA_SOFTMAX_ROWS)
        vcol = (2 * kv + h) * LANES
        o = _weighted_values(e_scr[slot], v_ref[:, vcol:vcol + LANES])
        if h == 0:
            y_scr[rows, col:col + LANES] = o
        else:
            y_scr[rows, col:col + LANES] = jnp.where(first_head, y_scr[rows, col:col + LANES], o)

    def gate():
        g_scr[...] = _silu_gate(x_ref, wz_ref, bz_ref)

    _run_stages(len(stages), scores, finish, gate)
    out_ref[...] = (y_scr[...] * g_scr[...]).astype(BF16)


def _gqa(layer, q2, kt, v, x, wb, bb):
    steps = SEQ // GQA_TQ
    row = lambda b, s: (b * steps + s, 0)
    return pl.pallas_call(
        _gqa_body,
        grid=(BATCH, steps),
        in_specs=[
            pl.BlockSpec((GQA_TQ, 2 * GQA_WIDTH), row),
            pl.BlockSpec((1, GQA_KV_HEADS, LANES, SEQ), lambda b, s: (b, 0, 0, 0)),
            pl.BlockSpec((SEQ, 2 * GQA_KV_HEADS * LANES), lambda b, s: (b, 0)),
            pl.BlockSpec((GQA_TQ, D_MODEL), row),
            _column_spec(D_MODEL, layer, O_GQZ, O_SGU),
            _column_spec(1, layer, O_GQZ, O_SGU),
        ],
        out_specs=pl.BlockSpec((GQA_TQ, GQA_WIDTH), row),
        out_shape=jax.ShapeDtypeStruct((TOKENS, GQA_WIDTH), BF16),
        scratch_shapes=[pltpu.VMEM((2, GQA_STAGE_ROWS, SEQ), F32), pltpu.VMEM((2, GQA_STAGE_ROWS, SEQ), BF16),
                        pltpu.VMEM((GQA_TQ, GQA_WIDTH), F32), pltpu.VMEM((GQA_TQ, GQA_WIDTH), F32)],
        compiler_params=_cparams(2),
        name="gqa",
    )(q2, kt, v, x, wb, bb)


def _merge_body(x_ref, ya_ref, yb_ref, vn_ref, wu_ref, bu_ref, wz_ref, bz_ref, ws_ref, bs_ref,
                wg_ref, bg_ref, wo_ref, wa_ref, wb_ref, wc_ref, bo_ref, lng_ref, lnb_ref, out_ref, mixed_ref):
    for r0 in range(0, MERGE_TM, MERGE_SUB):
        _merge_rows(slice(r0, r0 + MERGE_SUB), x_ref, ya_ref, yb_ref, vn_ref, wu_ref, bu_ref, wz_ref, bz_ref, ws_ref,
                    bs_ref, wg_ref, bg_ref, wo_ref, wa_ref, wb_ref, wc_ref, bo_ref, lng_ref, lnb_ref, out_ref, mixed_ref)


def _merge_rows(rows, x_ref, ya_ref, yb_ref, vn_ref, wu_ref, bu_ref, wz_ref, bz_ref, ws_ref, bs_ref,
                wg_ref, bg_ref, wo_ref, wa_ref, wb_ref, wc_ref, bo_ref, lng_ref, lnb_ref, out_ref, mixed_ref):
    n_chunks = MERGE_SUB // SG_CHUNK
    x = x_ref[rows, :]
    xb = x.astype(BF16)

    first_head = _lane_is_first_head((SG_CHUNK, LANES))
    chunk_rows = [slice(rows.start + c * SG_CHUNK, rows.start + (c + 1) * SG_CHUNK) for c in range(n_chunks)]
    for pp in range(SG_WIDTH // LANES):
        col = pp * LANES
        rhs = jnp.concatenate([vn_ref[cr, col:col + LANES] for cr in chunk_rows], axis=1)
        res = jnp.dot(ws_ref[pp], rhs, preferred_element_type=F32)
        for c, cr in enumerate(chunk_rows):
            blk = jnp.where(first_head, res[:SG_CHUNK, c * LANES:(c + 1) * LANES], res[SG_CHUNK:, c * LANES:(c + 1) * LANES])
            mixed_ref[cr, col:col + LANES] = blk + bs_ref[:, col:col + LANES]
    u = jnp.dot(xb, wu_ref[0], preferred_element_type=F32) + bu_ref[0]
    z = jnp.dot(xb, wz_ref[0], preferred_element_type=F32) + bz_ref[0]
    tc = (u * mixed_ref[rows, :] * (z * jax.nn.sigmoid(z))).astype(BF16)

    def gate(i):
        lo = i * D_MODEL
        return jax.nn.sigmoid(jnp.dot(xb, wg_ref[0, :, lo:lo + D_MODEL], preferred_element_type=F32)
                              + bg_ref[0, :, lo:lo + D_MODEL])

    merged = gate(0) * jnp.dot(ya_ref[rows, :], wa_ref[...], preferred_element_type=F32)
    merged = merged + gate(1) * jnp.dot(yb_ref[rows, :], wb_ref[...], preferred_element_type=F32)
    merged = merged + gate(2) * jnp.dot(tc, wc_ref[...], preferred_element_type=F32)
    sub = jnp.dot(merged.astype(BF16), wo_ref[...], preferred_element_type=F32) + bo_ref[...]
    out_ref[rows, :] = _layer_norm(DEEPNORM_ALPHA * x + sub, lng_ref[...], lnb_ref[...])


def _merge(layer, x, ya, yb, vn, wb, bb, ws, bs, wo, wa, wbb, wc, bo, lng, lnb):
    tm = MERGE_TM
    row = lambda i: (i, 0)
    vec = lambda width: _fixed_spec((None, 1, width), (layer, 0, 0))
    return pl.pallas_call(
        _merge_body,
        grid=(TOKENS // tm,),
        in_specs=[
            pl.BlockSpec((tm, D_MODEL), row),
            pl.BlockSpec((tm, NA_WIDTH), row),
            pl.BlockSpec((tm, GQA_WIDTH), row),
            pl.BlockSpec((tm, SG_WIDTH), row),
            _column_spec(D_MODEL, layer, O_SGU, O_SGV), _column_spec(1, layer, O_SGU, O_SGV),
            _column_spec(D_MODEL, layer, O_SGZ, O_GATE), _column_spec(1, layer, O_SGZ, O_GATE),
            _fixed_spec((None, SG_GROUPS // 2, 2 * SG_CHUNK, SG_CHUNK), (layer, 0, 0, 0)),
            _fixed_spec((None, SG_CHUNK, SG_WIDTH), (layer, 0, 0)),
            _column_spec(D_MODEL, layer, O_GATE, O_END), _column_spec(1, layer, O_GATE, O_END),
            _fixed_spec((None, D_MODEL, D_MODEL), (layer, 0, 0)),
            _fixed_spec((None, NA_WIDTH, D_MODEL), (layer, 0, 0)),
            _fixed_spec((None, GQA_WIDTH, D_MODEL), (layer, 0, 0)),
            _fixed_spec((None, SG_WIDTH, D_MODEL), (layer, 0, 0)),
            vec(D_MODEL), vec(D_MODEL), vec(D_MODEL),
        ],
        out_specs=pl.BlockSpec((tm, D_MODEL), row),
        out_shape=jax.ShapeDtypeStruct((TOKENS, D_MODEL), F32),
        scratch_shapes=[pltpu.VMEM((tm, SG_WIDTH), F32)],
        compiler_params=_cparams(1),
        name="merge",
    )(x, ya, yb, vn, wb, bb, wb, bb, ws, bs, wb, bb, wo, wa, wbb, wc, bo, lng, lnb)


def _rope_tables():
    t = np.arange(SEQ)
    row = (t // GRID_W).astype(np.float32)
    col = (t % GRID_W).astype(np.float32)
    freqs = np.float32(ROPE_THETA) ** (-np.arange(0, ROPE_AXIS_DIM, 2, dtype=np.float32) / np.float32(ROPE_AXIS_DIM))
    ang = np.concatenate([row[:, None] * freqs, col[:, None] * freqs], axis=-1).astype(np.float32)
    cos = np.cos(ang.astype(np.float64)).astype(np.float32)
    sin = np.sin(ang.astype(np.float64)).astype(np.float32)
    pair = (np.arange(LANES) % HEAD_DIM) // 2
    even = (np.arange(LANES) % 2 == 0)[None, :]
    cos_t = cos[:, pair]
    s1_t = np.where(even, -sin[:, pair], 0.0).astype(np.float32)
    s2_t = np.where(even, 0.0, sin[:, pair]).astype(np.float32)
    return jnp.asarray(cos_t), jnp.asarray(s1_t), jnp.asarray(s2_t)


def kernel(x, ln_in_g, ln_in_b, w_in, b_in, na_rpb, q_norm_g, k_norm_g, sg_ln_g, sg_ln_b, sg_w, sg_b,
           w_br_a, w_br_b, w_br_c, w_out, b_out, ln_post_g, ln_post_b):
    assert x.shape == (BATCH, SEQ, D_MODEL) and w_in.shape == (DEPTH, D_MODEL, O_END)
    cos_t, s1_t, s2_t = _rope_tables()
    bd = jnp.asarray(np.kron(np.eye(2), np.ones((HEAD_DIM, HEAD_DIM))), BF16)
    vec3 = lambda v: v.reshape(DEPTH, 1, -1)

    wb, bb = w_in.astype(BF16), vec3(b_in)
    wo, wa, wbb, wc = (w.astype(BF16) for w in (w_out, w_br_a, w_br_b, w_br_c))
    ws = sg_w.reshape(DEPTH, SG_GROUPS // 2, 2 * SG_CHUNK, SG_CHUNK).astype(BF16)
    bs = jnp.repeat(jnp.swapaxes(sg_b, 1, 2), HEAD_DIM, axis=2)
    gq = vec3(jnp.tile(q_norm_g * SCORE_SCALE, (1, 2)))
    gk = vec3(jnp.tile(k_norm_g, (1, 2)))
    sgg, sgb = vec3(sg_ln_g), vec3(sg_ln_b)
    bo, lng, lnb = vec3(b_out), vec3(ln_post_g), vec3(ln_post_b)
    bias = _na_bias(na_rpb.reshape(-1))

    h = x.reshape(TOKENS, D_MODEL)
    for l in range(DEPTH):
        proj_args = (wb, bb, gq, gk, bd, cos_t, s1_t, s2_t, sgg, sgb)
        if l == 0:
            h, naq, nak, nav, gqq, gqkt, gqv, vn = _proj(l, True, h, ln_in_g.reshape(1, -1), ln_in_b.reshape(1, -1), *proj_args)
        else:
            naq, nak, nav, gqq, gqkt, gqv, vn = _proj(l, False, h, None, None, *proj_args)
        ya = _na(l, naq, nak, nav, bias, h, wb, bb)
        yb = _gqa(l, gqq, gqkt, gqv, h, wb, bb)
        h = _merge(l, h, ya, yb, vn, wb, bb, ws, bs, wo, wa, wbb, wc, bo, lng, lnb)
    return h.reshape(BATCH, SEQ, D_MODEL)
```

```python
import functools

import numpy as np
import jax
import jax.numpy as jnp
from jax import lax
from jax.experimental import pallas as pl
from jax.experimental.pallas import tpu as pltpu

F32 = jnp.float32
BF16 = jnp.bfloat16

D_MODEL = 1024
BATCH = 8
SEQ = 2048
DEPTH = 2
GRID_W = 64
ROWS = SEQ // GRID_W
HEAD_DIM = 64
NA_HEADS = 8
NA_WIDTH = 512
NA_KH = 8
NA_KW = 16
RPB_ROWS = 2 * NA_KH - 1
RPB_COLS = 2 * NA_KW - 1
GQA_HEADS = 8
GQA_KV_HEADS = 2
GQA_WIDTH = 512
GQA_KV_WIDTH = 128
ROPE_THETA = 10000.0
ROPE_AXIS_DIM = HEAD_DIM // 2
SG_WIDTH = 512
SG_GROUPS = 8
SG_CHUNK = 128
N_BRANCH = 3
LN_EPS = 1e-5
RMS_EPS = 1e-6
DEEPNORM_ALPHA = (2.0 * DEPTH) ** 0.25
ATTN_SCALE = HEAD_DIM ** -0.5
LOG2E = 1.4426950408889634
SCORE_SCALE = ATTN_SCALE * LOG2E
MASK_VALUE = -1e30

TOKENS = BATCH * SEQ
LANES = 128
VMEM_LIMIT = 56 * 1024 * 1024

_SPLITS = (NA_WIDTH,) * 4 + (GQA_WIDTH, GQA_KV_WIDTH, GQA_KV_WIDTH, GQA_WIDTH) + (SG_WIDTH,) * 3 + (N_BRANCH * D_MODEL,)
_OFFS = [int(v) for v in np.concatenate([[0], np.cumsum(_SPLITS)])]
(O_NAQ, O_NAK, O_NAV, O_NAZ, O_GQQ, O_GQK, O_GQV, O_GQZ, O_SGU, O_SGV, O_SGZ, O_GATE, O_END) = _OFFS

PROJ_SEGMENTS = ((O_NAQ, O_NAZ), (O_GQQ, O_GQZ), (O_SGV, O_SGZ))
P_NAQ, P_NAK, P_NAV = (0, 0), (0, NA_WIDTH), (0, 2 * NA_WIDTH)
P_GQQ, P_GQK, P_GQV = (1, 0), (1, GQA_WIDTH), (1, GQA_WIDTH + GQA_KV_WIDTH)
P_SGV = (2, 0)

PROJ_TM = 1024
PROJ_SUB = 512
NA_ROWS = 4
NA_TQ = NA_ROWS * GRID_W
NA_WIN_ROWS = 12
NA_WIN = NA_WIN_ROWS * GRID_W
NA_GROUP_COUNT = ROWS // NA_ROWS
NA_GROUPS_PER_STEP = 2
NA_TYPES = 3
NA_SOFTMAX_ROWS = 32
GQA_TQ = 1024
GQA_STAGE_ROWS = 512
GQA_SOFTMAX_ROWS = 16
MERGE_TM = 512
MERGE_SUB = 256


def _cparams(n_axes):
    return pltpu.CompilerParams(dimension_semantics=("arbitrary",) * n_axes, vmem_limit_bytes=VMEM_LIMIT)


def _fixed_spec(block_shape, index):
    return pl.BlockSpec(block_shape, lambda *_: index, pipeline_mode=pl.Buffered(1))


def _column_spec(rows, layer, start, stop):
    return _fixed_spec((pl.Element(1), pl.Element(rows), pl.Element(stop - start)), (layer, 0, start))


def _layer_norm(x, g, b):
    mu = jnp.mean(x, axis=-1, keepdims=True)
    xc = x - mu
    var = jnp.mean(xc * xc, axis=-1, keepdims=True)
    return xc * lax.rsqrt(var + LN_EPS) * g + b


def _lane_is_first_head(shape):
    return lax.broadcasted_iota(jnp.int32, shape, len(shape) - 1) % LANES < HEAD_DIM


def _na_window_start(step):
    return min(max(NA_ROWS * step - NA_KH // 2, 0), ROWS - NA_WIN_ROWS)


_NA_TYPE_STEPS = (0, 1, NA_GROUP_COUNT - 1)


def _na_tile_plan():
    plan = {}
    for t, step in enumerate(_NA_TYPE_STEPS):
        w0 = _na_window_start(step)
        for dq in range(NA_ROWS):
            r = NA_ROWS * step + dq
            row_start = min(max(r - NA_KH // 2, 0), ROWS - NA_KH)
            for j in range(NA_WIN_ROWS):
                key_row = w0 + j
                inside = row_start <= key_row < row_start + NA_KH
                plan[(t, dq, j)] = key_row - r + (NA_KH - 1) if inside else None
    return plan


def _na_bias_body(rpb_ref, out_ref):
    layer, pair = pl.program_id(0), pl.program_id(1)
    qi = lax.broadcasted_iota(jnp.int32, (GRID_W, GRID_W), 0)
    ki = lax.broadcasted_iota(jnp.int32, (GRID_W, GRID_W), 1)
    d = jnp.clip(ki - qi + (NA_KW - 1), 0, RPB_COLS - 1)
    col_start = jnp.clip(qi - NA_KW // 2, 0, GRID_W - NA_KW)
    col_valid = (ki >= col_start) & (ki < col_start + NA_KW)
    masked = jnp.full((GRID_W, GRID_W), MASK_VALUE, F32)
    plan = _na_tile_plan()
    for hsel in range(2):
        base = ((layer * NA_HEADS + 2 * pair + hsel) * RPB_ROWS) * RPB_COLS
        for ro in range(RPB_ROWS):
            users = [key for key, val in plan.items() if val == ro]
            if not users:
                continue
            tile = jnp.zeros((GRID_W, GRID_W), F32)
            for c in range(RPB_COLS):
                tile = jnp.where(d == c, rpb_ref[base + ro * RPB_COLS + c], tile)
            tile = jnp.where(col_valid, tile * LOG2E, MASK_VALUE)
            for (t, dq, j) in users:
                row0 = hsel * NA_TQ + dq * GRID_W
                out_ref[0, t, 0, row0:row0 + GRID_W, j * GRID_W:(j + 1) * GRID_W] = tile
        for (t, dq, j), val in plan.items():
            if val is None:
                row0 = hsel * NA_TQ + dq * GRID_W
                out_ref[0, t, 0, row0:row0 + GRID_W, j * GRID_W:(j + 1) * GRID_W] = masked


def _na_bias(rpb_flat):
    return pl.pallas_call(
        _na_bias_body,
        grid=(DEPTH, NA_HEADS // 2),
        in_specs=[pl.BlockSpec(memory_space=pltpu.SMEM)],
        out_specs=pl.BlockSpec((1, NA_TYPES, 1, 2 * NA_TQ, NA_WIN), lambda l, p: (l, 0, p, 0, 0)),
        out_shape=jax.ShapeDtypeStruct((DEPTH, NA_TYPES, NA_HEADS // 2, 2 * NA_TQ, NA_WIN), F32),
        compiler_params=_cparams(2),
        name="na_bias",
    )(rpb_flat)


def _proj_body(first, *refs):
    if first:
        (x_ref, lng_ref, lnb_ref, w0_ref, w1_ref, w2_ref, b0_ref, b1_ref, b2_ref, gq_ref, gk_ref, bd_ref,
         cos_ref, s1_ref, s2_ref, sgg_ref, sgb_ref,
         h_ref, naq_ref, nak_ref, nav_ref, gqq_ref, gqkt_ref, gqv_ref, vn_ref) = refs
    else:
        (x_ref, w0_ref, w1_ref, w2_ref, b0_ref, b1_ref, b2_ref, gq_ref, gk_ref, bd_ref,
         cos_ref, s1_ref, s2_ref, sgg_ref, sgb_ref,
         naq_ref, nak_ref, nav_ref, gqq_ref, gqkt_ref, gqv_ref, vn_ref) = refs
    w_refs = tuple(r.at[0] for r in (w0_ref, w1_ref, w2_ref))
    b_refs = tuple(r.at[0] for r in (b0_ref, b1_ref, b2_ref))
    for r0 in range(0, PROJ_TM, PROJ_SUB):
        _proj_rows(first, slice(r0, r0 + PROJ_SUB), x_ref, lng_ref if first else None, lnb_ref if first else None,
                   w_refs, b_refs, gq_ref, gk_ref, bd_ref, cos_ref, s1_ref, s2_ref, sgg_ref, sgb_ref,
                   h_ref if first else None, naq_ref, nak_ref, nav_ref, gqq_ref, gqkt_ref, gqv_ref, vn_ref)


def _proj_rows(first, rows, x_ref, lng_ref, lnb_ref, w_refs, b_refs, gq_ref, gk_ref, bd_ref, cos_ref, s1_ref, s2_ref,
               sgg_ref, sgb_ref, h_ref, naq_ref, nak_ref, nav_ref, gqq_ref, gqkt_ref, gqv_ref, vn_ref):
    x = x_ref[rows, :]
    if first:
        x = _layer_norm(x, lng_ref[...], lnb_ref[...])
        h_ref[rows, :] = x
    xb = x.astype(BF16)

    def proj(where, width):
        seg, col = where
        return (jnp.dot(xb, w_refs[seg][:, col:col + width], preferred_element_type=F32)
                + b_refs[seg][:, col:col + width])

    def store_head_masked(dst_ref, val, col):
        first_head = _lane_is_first_head(val.shape)
        dst_ref[rows, col:col + LANES] = jnp.where(first_head, val, 0.0).astype(BF16)
        dst_ref[rows, NA_WIDTH + col:NA_WIDTH + col + LANES] = jnp.where(first_head, 0.0, val).astype(BF16)

    def store_with_ones(dst_ref, val, col, second_copy):
        first_head = _lane_is_first_head(val.shape)
        dst_ref[rows, col:col + LANES] = jnp.where(first_head, val, 1.0).astype(BF16)
        dst_ref[rows, second_copy + col:second_copy + col + LANES] = jnp.where(first_head, 1.0, val).astype(BF16)

    def rms_rope(val, gain):
        sq = val * val
        hi = sq.astype(BF16)
        lo = (sq - hi.astype(F32)).astype(BF16)
        ssum = jnp.dot(jnp.concatenate([hi, lo], axis=1), bd_ref[...], preferred_element_type=F32)
        xn = val * lax.rsqrt(ssum * (1.0 / HEAD_DIM) + RMS_EPS) * gain
        return (xn * cos_ref[rows, :] + pltpu.roll(xn, LANES - 1, 1) * s1_ref[rows, :]
                + pltpu.roll(xn, 1, 1) * s2_ref[rows, :])

    q = proj(P_NAQ, NA_WIDTH) * SCORE_SCALE
    for p in range(NA_WIDTH // LANES):
        store_head_masked(naq_ref, q[:, p * LANES:(p + 1) * LANES], p * LANES)
    nak_ref[rows, :] = proj(P_NAK, NA_WIDTH).astype(BF16)
    nav_ref[rows, :] = proj(P_NAV, NA_WIDTH).astype(BF16)

    q = proj(P_GQQ, GQA_WIDTH)
    for p in range(GQA_WIDTH // LANES):
        store_head_masked(gqq_ref, rms_rope(q[:, p * LANES:(p + 1) * LANES], gq_ref[...]), p * LANES)
    k = rms_rope(proj(P_GQK, GQA_KV_WIDTH), gk_ref[...])
    v = proj(P_GQV, GQA_KV_WIDTH)
    first_head = _lane_is_first_head(k.shape)
    k_sw = pltpu.roll(k, HEAD_DIM, 1)
    v_sw = pltpu.roll(v, HEAD_DIM, 1)
    gqkt_ref[0, 0, :, rows] = jnp.where(first_head, k, k_sw).T.astype(BF16)
    gqkt_ref[0, 1, :, rows] = jnp.where(first_head, k_sw, k).T.astype(BF16)
    store_with_ones(gqv_ref, jnp.where(first_head, v, v_sw), 0, LANES)
    store_with_ones(gqv_ref, jnp.where(first_head, v_sw, v), 2 * LANES, LANES)

    vn_ref[rows, :] = _layer_norm(proj(P_SGV, SG_WIDTH), sgg_ref[...], sgb_ref[...]).astype(BF16)


def _proj(layer, first, x, ln_g, ln_b, wb, bb, gq, gk, bd, cos_t, s1_t, s2_t, sgg, sgb):
    tm = PROJ_TM
    tiles_per_seq = SEQ // tm
    row = lambda i: (i, 0)
    pos = lambda i: (i % tiles_per_seq, 0)
    in_specs = [pl.BlockSpec((tm, D_MODEL), row)]
    args = [x]
    if first:
        in_specs += [_fixed_spec((1, D_MODEL), (0, 0)), _fixed_spec((1, D_MODEL), (0, 0))]
        args += [ln_g, ln_b]
    in_specs += [_column_spec(D_MODEL, layer, a, b) for a, b in PROJ_SEGMENTS]
    in_specs += [_column_spec(1, layer, a, b) for a, b in PROJ_SEGMENTS]
    in_specs += [_fixed_spec((None, 1, LANES), (layer, 0, 0)), _fixed_spec((None, 1, LANES), (layer, 0, 0)),
                 _fixed_spec((2 * LANES, LANES), (0, 0)),
                 pl.BlockSpec((tm, LANES), pos), pl.BlockSpec((tm, LANES), pos), pl.BlockSpec((tm, LANES), pos),
                 _fixed_spec((None, 1, SG_WIDTH), (layer, 0, 0)), _fixed_spec((None, 1, SG_WIDTH), (layer, 0, 0))]
    args += [wb] * len(PROJ_SEGMENTS) + [bb] * len(PROJ_SEGMENTS) + [gq, gk, bd, cos_t, s1_t, s2_t, sgg, sgb]
    out_shape, out_specs = [], []
    if first:
        out_shape.append(jax.ShapeDtypeStruct((TOKENS, D_MODEL), F32))
        out_specs.append(pl.BlockSpec((tm, D_MODEL), row))
    out_shape += [
        jax.ShapeDtypeStruct((TOKENS, 2 * NA_WIDTH), BF16),
        jax.ShapeDtypeStruct((TOKENS, NA_WIDTH), BF16),
        jax.ShapeDtypeStruct((TOKENS, NA_WIDTH), BF16),
        jax.ShapeDtypeStruct((TOKENS, 2 * GQA_WIDTH), BF16),
        jax.ShapeDtypeStruct((BATCH, GQA_KV_HEADS, LANES, SEQ), BF16),
        jax.ShapeDtypeStruct((TOKENS, 2 * GQA_KV_HEADS * LANES), BF16),
        jax.ShapeDtypeStruct((TOKENS, SG_WIDTH), BF16),
    ]
    out_specs += [
        pl.BlockSpec((tm, 2 * NA_WIDTH), row),
        pl.BlockSpec((tm, NA_WIDTH), row),
        pl.BlockSpec((tm, NA_WIDTH), row),
        pl.BlockSpec((tm, 2 * GQA_WIDTH), row),
        pl.BlockSpec((1, GQA_KV_HEADS, LANES, tm), lambda i: (i // tiles_per_seq, 0, 0, i % tiles_per_seq)),
        pl.BlockSpec((tm, 2 * GQA_KV_HEADS * LANES), row),
        pl.BlockSpec((tm, SG_WIDTH), row),
    ]
    return pl.pallas_call(
        functools.partial(_proj_body, first),
        grid=(TOKENS // tm,),
        in_specs=in_specs,
        out_specs=out_specs,
        out_shape=out_shape,
        compiler_params=_cparams(1),
        name="proj_first" if first else "proj",
    )(*args)


def _silu_gate(x_ref, wz_ref, bz_ref):
    z = jnp.dot(x_ref[...].astype(BF16), wz_ref[0], preferred_element_type=F32) + bz_ref[0]
    return z * jax.nn.sigmoid(z)


def _softmax_numerators(s_ref, e_ref, n_rows, block_rows, l_ref=None):
    for r in range(0, n_rows, block_rows):
        rows = slice(r, r + block_rows)
        blk = s_ref[rows, :]
        e = jnp.exp2(blk - jnp.max(blk, axis=-1, keepdims=True))
        e_ref[rows, :] = e.astype(BF16)
        if l_ref is not None:
            l_ref[rows, :] = jnp.broadcast_to(jnp.sum(e, axis=-1, keepdims=True), (block_rows, LANES))


def _weighted_values(e, v_ones):
    o = jnp.dot(e, v_ones, preferred_element_type=F32)
    return o / pltpu.roll(o, HEAD_DIM, 1)


def _run_stages(n_stages, scores, finish, before_last):
    scores(0)
    for i in range(n_stages):
        if i + 1 < n_stages:
            scores(i + 1)
        else:
            before_last()
        finish(i)


def _na_body(q_ref, k_ref, v_ref, bias0_ref, bias1_ref, x_ref, wz_ref, bz_ref, out_ref,
             s_scr, e_scr, l_scr, y_scr, g_scr):
    step = pl.program_id(1)
    bias_refs = (bias0_ref, bias1_ref)
    first_head = _lane_is_first_head((NA_TQ, LANES))
    stages = [(g, p) for g in range(NA_GROUPS_PER_STEP) for p in range(NA_WIDTH // LANES)]

    def window(g):
        w0 = jnp.clip(NA_ROWS * (NA_GROUPS_PER_STEP * step + g) - NA_KH // 2, 0, ROWS - NA_WIN_ROWS)
        return pl.ds(pl.multiple_of(w0 * GRID_W, 2 * LANES), NA_WIN)

    def scores(i):
        g, p = stages[i]
        rows, col = slice(g * NA_TQ, (g + 1) * NA_TQ), p * LANES
        qs = jnp.concatenate([q_ref[rows, col:col + LANES], q_ref[rows, NA_WIDTH + col:NA_WIDTH + col + LANES]], axis=0)
        s = lax.dot_general(qs, k_ref[window(g), col:col + LANES], (((1,), (1,)), ((), ())), preferred_element_type=F32)
        s_scr[i % 2] = s + bias_refs[g][0, 0, p]

    def finish(i):
        g, p = stages[i]
        rows, col, slot = slice(g * NA_TQ, (g + 1) * NA_TQ), p * LANES, i % 2
        _softmax_numerators(s_scr.at[slot], e_scr.at[slot], 2 * NA_TQ, NA_SOFTMAX_ROWS, l_scr.at[slot])
        o = jnp.dot(e_scr[slot], v_ref[window(g), col:col + LANES], preferred_element_type=F32) / l_scr[slot]
        y_scr[rows, col:col + LANES] = jnp.where(first_head, o[:NA_TQ], o[NA_TQ:])

    def gate():
        g_scr[...] = _silu_gate(x_ref, wz_ref, bz_ref)

    _run_stages(len(stages), scores, finish, gate)
    out_ref[...] = (y_scr[...] * g_scr[...]).astype(BF16)


def _na(layer, q2, k, v, bias, x, wb, bb):
    steps = NA_GROUP_COUNT // NA_GROUPS_PER_STEP
    tq = NA_GROUPS_PER_STEP * NA_TQ

    def bias_spec(g):
        def index(b, s):
            group = NA_GROUPS_PER_STEP * s + g
            return (layer, jnp.minimum(group, 1) + jnp.maximum(group - (NA_GROUP_COUNT - 2), 0), 0, 0, 0)
        return pl.BlockSpec((1, 1, NA_HEADS // 2, 2 * NA_TQ, NA_WIN), index)

    row = lambda b, s: (b * steps + s, 0)
    return pl.pallas_call(
        _na_body,
        grid=(BATCH, steps),
        in_specs=[
            pl.BlockSpec((tq, 2 * NA_WIDTH), row),
            pl.BlockSpec((SEQ, NA_WIDTH), lambda b, s: (b, 0)),
            pl.BlockSpec((SEQ, NA_WIDTH), lambda b, s: (b, 0)),
            bias_spec(0), bias_spec(1),
            pl.BlockSpec((tq, D_MODEL), row),
            _column_spec(D_MODEL, layer, O_NAZ, O_GQQ),
            _column_spec(1, layer, O_NAZ, O_GQQ),
        ],
        out_specs=pl.BlockSpec((tq, NA_WIDTH), row),
        out_shape=jax.ShapeDtypeStruct((TOKENS, NA_WIDTH), BF16),
        scratch_shapes=[pltpu.VMEM((2, 2 * NA_TQ, NA_WIN), F32), pltpu.VMEM((2, 2 * NA_TQ, NA_WIN), BF16),
                        pltpu.VMEM((2, 2 * NA_TQ, LANES), F32), pltpu.VMEM((tq, NA_WIDTH), F32),
                        pltpu.VMEM((tq, NA_WIDTH), F32)],
        compiler_params=_cparams(2),
        name="na",
    )(q2, k, v, bias, bias, x, wb, bb)


def _gqa_body(q_ref, kt_ref, v_ref, x_ref, wz_ref, bz_ref, out_ref, s_scr, e_scr, y_scr, g_scr):
    m = GQA_STAGE_ROWS
    tiles_per_kv = GQA_WIDTH // LANES // GQA_KV_HEADS
    first_head = _lane_is_first_head((m, LANES))
    stages = [(p, rb, h) for p in range(GQA_WIDTH // LANES) for rb in range(GQA_TQ // m) for h in range(2)]

    def scores(i):
        p, rb, h = stages[i]
        col = h * GQA_WIDTH + p * LANES
        s_scr[i % 2] = jnp.dot(q_ref[rb * m:(rb + 1) * m, col:col + LANES], kt_ref[0, p // tiles_per_kv],
                               preferred_element_type=F32)

    def finish(i):
        p, rb, h = stages[i]
        rows, col, slot, kv = slice(rb * m, (rb + 1) * m), p * LANES, i % 2, p // tiles_per_kv
        _softmax_numerators(s_scr.at[slot], e_scr.at[slot], m, GQA_SOFTMAX_ROWS)
        vcol = (2 * kv + h) * LANES
        o = _weighted_values(e_scr[slot], v_ref[:, vcol:vcol + LANES])
        if h == 0:
            y_scr[rows, col:col + LANES] = o
        else:
            y_scr[rows, col:col + LANES] = jnp.where(first_head, y_scr[rows, col:col + LANES], o)

    def gate():
        g_scr[...] = _silu_gate(x_ref, wz_ref, bz_ref)

    _run_stages(len(stages), scores, finish, gate)
    out_ref[...] = (y_scr[...] * g_scr[...]).astype(BF16)


def _gqa(layer, q2, kt, v, x, wb, bb):
    steps = SEQ // GQA_TQ
    row = lambda b, s: (b * steps + s, 0)
    return pl.pallas_call(
        _gqa_body,
        grid=(BATCH, steps),
        in_specs=[
            pl.BlockSpec((GQA_TQ, 2 * GQA_WIDTH), row),
            pl.BlockSpec((1, GQA_KV_HEADS, LANES, SEQ), lambda b, s: (b, 0, 0, 0)),
            pl.BlockSpec((SEQ, 2 * GQA_KV_HEADS * LANES), lambda b, s: (b, 0)),
            pl.BlockSpec((GQA_TQ, D_MODEL), row),
            _column_spec(D_MODEL, layer, O_GQZ, O_SGU),
            _column_spec(1, layer, O_GQZ, O_SGU),
        ],
        out_specs=pl.BlockSpec((GQA_TQ, GQA_WIDTH), row),
        out_shape=jax.ShapeDtypeStruct((TOKENS, GQA_WIDTH), BF16),
        scratch_shapes=[pltpu.VMEM((2, GQA_STAGE_ROWS, SEQ), F32), pltpu.VMEM((2, GQA_STAGE_ROWS, SEQ), BF16),
                        pltpu.VMEM((GQA_TQ, GQA_WIDTH), F32), pltpu.VMEM((GQA_TQ, GQA_WIDTH), F32)],
        compiler_params=_cparams(2),
        name="gqa",
    )(q2, kt, v, x, wb, bb)


def _merge_body(x_ref, ya_ref, yb_ref, vn_ref, wu_ref, bu_ref, wz_ref, bz_ref, ws_ref, bs_ref,
                wg_ref, bg_ref, wo_ref, wa_ref, wb_ref, wc_ref, bo_ref, lng_ref, lnb_ref, out_ref, mixed_ref):
    for r0 in range(0, MERGE_TM, MERGE_SUB):
        _merge_rows(slice(r0, r0 + MERGE_SUB), x_ref, ya_ref, yb_ref, vn_ref, wu_ref, bu_ref, wz_ref, bz_ref, ws_ref,
                    bs_ref, wg_ref, bg_ref, wo_ref, wa_ref, wb_ref, wc_ref, bo_ref, lng_ref, lnb_ref, out_ref, mixed_ref)


def _merge_rows(rows, x_ref, ya_ref, yb_ref, vn_ref, wu_ref, bu_ref, wz_ref, bz_ref, ws_ref, bs_ref,
                wg_ref, bg_ref, wo_ref, wa_ref, wb_ref, wc_ref, bo_ref, lng_ref, lnb_ref, out_ref, mixed_ref):
    n_chunks = MERGE_SUB // SG_CHUNK
    x = x_ref[rows, :]
    xb = x.astype(BF16)

    first_head = _lane_is_first_head((SG_CHUNK, LANES))
    chunk_rows = [slice(rows.start + c * SG_CHUNK, rows.start + (c + 1) * SG_CHUNK) for c in range(n_chunks)]
    for pp in range(SG_WIDTH // LANES):
        col = pp * LANES
        rhs = jnp.concatenate([vn_ref[cr, col:col + LANES] for cr in chunk_rows], axis=1)
        res = jnp.dot(ws_ref[pp], rhs, preferred_element_type=F32)
        for c, cr in enumerate(chunk_rows):
            blk = jnp.where(first_head, res[:SG_CHUNK, c * LANES:(c + 1) * LANES], res[SG_CHUNK:, c * LANES:(c + 1) * LANES])
            mixed_ref[cr, col:col + LANES] = blk + bs_ref[:, col:col + LANES]
    u = jnp.dot(xb, wu_ref[0], preferred_element_type=F32) + bu_ref[0]
    z = jnp.dot(xb, wz_ref[0], preferred_element_type=F32) + bz_ref[0]
    tc = (u * mixed_ref[rows, :] * (z * jax.nn.sigmoid(z))).astype(BF16)

    def gate(i):
        lo = i * D_MODEL
        return jax.nn.sigmoid(jnp.dot(xb, wg_ref[0, :, lo:lo + D_MODEL], preferred_element_type=F32)
                              + bg_ref[0, :, lo:lo + D_MODEL])

    merged = gate(0) * jnp.dot(ya_ref[rows, :], wa_ref[...], preferred_element_type=F32)
    merged = merged + gate(1) * jnp.dot(yb_ref[rows, :], wb_ref[...], preferred_element_type=F32)
    merged = merged + gate(2) * jnp.dot(tc, wc_ref[...], preferred_element_type=F32)
    sub = jnp.dot(merged.astype(BF16), wo_ref[...], preferred_element_type=F32) + bo_ref[...]
    out_ref[rows, :] = _layer_norm(DEEPNORM_ALPHA * x + sub, lng_ref[...], lnb_ref[...])


def _merge(layer, x, ya, yb, vn, wb, bb, ws, bs, wo, wa, wbb, wc, bo, lng, lnb):
    tm = MERGE_TM
    row = lambda i: (i, 0)
    vec = lambda width: _fixed_spec((None, 1, width), (layer, 0, 0))
    return pl.pallas_call(
        _merge_body,
        grid=(TOKENS // tm,),
        in_specs=[
            pl.BlockSpec((tm, D_MODEL), row),
            pl.BlockSpec((tm, NA_WIDTH), row),
            pl.BlockSpec((tm, GQA_WIDTH), row),
            pl.BlockSpec((tm, SG_WIDTH), row),
            _column_spec(D_MODEL, layer, O_SGU, O_SGV), _column_spec(1, layer, O_SGU, O_SGV),
            _column_spec(D_MODEL, layer, O_SGZ, O_GATE), _column_spec(1, layer, O_SGZ, O_GATE),
            _fixed_spec((None, SG_GROUPS // 2, 2 * SG_CHUNK, SG_CHUNK), (layer, 0, 0, 0)),
            _fixed_spec((None, SG_CHUNK, SG_WIDTH), (layer, 0, 0)),
            _column_spec(D_MODEL, layer, O_GATE, O_END), _column_spec(1, layer, O_GATE, O_END),
            _fixed_spec((None, D_MODEL, D_MODEL), (layer, 0, 0)),
            _fixed_spec((None, NA_WIDTH, D_MODEL), (layer, 0, 0)),
            _fixed_spec((None, GQA_WIDTH, D_MODEL), (layer, 0, 0)),
            _fixed_spec((None, SG_WIDTH, D_MODEL), (layer, 0, 0)),
            vec(D_MODEL), vec(D_MODEL), vec(D_MODEL),
        ],
        out_specs=pl.BlockSpec((tm, D_MODEL), row),
        out_shape=jax.ShapeDtypeStruct((TOKENS, D_MODEL), F32),
        scratch_shapes=[pltpu.VMEM((tm, SG_WIDTH), F32)],
        compiler_params=_cparams(1),
        name="merge",
    )(x, ya, yb, vn, wb, bb, wb, bb, ws, bs, wb, bb, wo, wa, wbb, wc, bo, lng, lnb)


def _rope_tables():
    t = np.arange(SEQ)
    row = (t // GRID_W).astype(np.float32)
    col = (t % GRID_W).astype(np.float32)
    freqs = np.float32(ROPE_THETA) ** (-np.arange(0, ROPE_AXIS_DIM, 2, dtype=np.float32) / np.float32(ROPE_AXIS_DIM))
    ang = np.concatenate([row[:, None] * freqs, col[:, None] * freqs], axis=-1).astype(np.float32)
    cos = np.cos(ang.astype(np.float64)).astype(np.float32)
    sin = np.sin(ang.astype(np.float64)).astype(np.float32)
    pair = (np.arange(LANES) % HEAD_DIM) // 2
    even = (np.arange(LANES) % 2 == 0)[None, :]
    cos_t = cos[:, pair]
    s1_t = np.where(even, -sin[:, pair], 0.0).astype(np.float32)
    s2_t = np.where(even, 0.0, sin[:, pair]).astype(np.float32)
    return jnp.asarray(cos_t), jnp.asarray(s1_t), jnp.asarray(s2_t)


def kernel(x, ln_in_g, ln_in_b, w_in, b_in, na_rpb, q_norm_g, k_norm_g, sg_ln_g, sg_ln_b, sg_w, sg_b,
           w_br_a, w_br_b, w_br_c, w_out, b_out, ln_post_g, ln_post_b):
    assert x.shape == (BATCH, SEQ, D_MODEL) and w_in.shape == (DEPTH, D_MODEL, O_END)
    cos_t, s1_t, s2_t = _rope_tables()
    bd = jnp.asarray(np.tile(np.kron(np.eye(2), np.ones((HEAD_DIM, HEAD_DIM))), (2, 1)), BF16)
    vec3 = lambda v: v.reshape(DEPTH, 1, -1)

    wb, bb = w_in.astype(BF16), vec3(b_in)
    wo, wa, wbb, wc = (w.astype(BF16) for w in (w_out, w_br_a, w_br_b, w_br_c))
    ws = sg_w.reshape(DEPTH, SG_GROUPS // 2, 2 * SG_CHUNK, SG_CHUNK).astype(BF16)
    bs = jnp.repeat(jnp.swapaxes(sg_b, 1, 2), HEAD_DIM, axis=2)
    gq = vec3(jnp.tile(q_norm_g * SCORE_SCALE, (1, 2)))
    gk = vec3(jnp.tile(k_norm_g, (1, 2)))
    sgg, sgb = vec3(sg_ln_g), vec3(sg_ln_b)
    bo, lng, lnb = vec3(b_out), vec3(ln_post_g), vec3(ln_post_b)
    bias = _na_bias(na_rpb.reshape(-1))

    h = x.reshape(TOKENS, D_MODEL)
    for l in range(DEPTH):
        proj_args = (wb, bb, gq, gk, bd, cos_t, s1_t, s2_t, sgg, sgb)
        if l == 0:
            h, naq, nak, nav, gqq, gqkt, gqv, vn = _proj(l, True, h, ln_in_g.reshape(1, -1), ln_in_b.reshape(1, -1), *proj_args)
        else:
            naq, nak, nav, gqq, gqkt, gqv, vn = _proj(l, False, h, None, None, *proj_args)
        ya = _na(l, naq, nak, nav, bias, h, wb, bb)
        yb = _gqa(l, gqq, gqkt, gqv, h, wb, bb)
        h = _merge(l, h, ya, yb, vn, wb, bb, ws, bs, wo, wa, wbb, wc, bo, lng, lnb)
    return h.reshape(BATCH, SEQ, D_MODEL)
```

```python
import functools

import numpy as np
import jax
import jax.numpy as jnp
from jax import lax
from jax.experimental import pallas as pl
from jax.experimental.pallas import tpu as pltpu

F32 = jnp.float32
BF16 = jnp.bfloat16

D_MODEL = 1024
BATCH = 8
SEQ = 2048
DEPTH = 2
GRID_W = 64
ROWS = SEQ // GRID_W
HEAD_DIM = 64
NA_HEADS = 8
NA_WIDTH = 512
NA_KH = 8
NA_KW = 16
RPB_ROWS = 2 * NA_KH - 1
RPB_COLS = 2 * NA_KW - 1
GQA_HEADS = 8
GQA_KV_HEADS = 2
GQA_WIDTH = 512
GQA_KV_WIDTH = 128
ROPE_THETA = 10000.0
ROPE_AXIS_DIM = HEAD_DIM // 2
SG_WIDTH = 512
SG_GROUPS = 8
SG_CHUNK = 128
N_BRANCH = 3
LN_EPS = 1e-5
RMS_EPS = 1e-6
DEEPNORM_ALPHA = (2.0 * DEPTH) ** 0.25
ATTN_SCALE = HEAD_DIM ** -0.5
LOG2E = 1.4426950408889634
SCORE_SCALE = ATTN_SCALE * LOG2E
MASK_VALUE = -1e30

TOKENS = BATCH * SEQ
LANES = 128
VMEM_LIMIT = 56 * 1024 * 1024

_SPLITS = (NA_WIDTH,) * 4 + (GQA_WIDTH, GQA_KV_WIDTH, GQA_KV_WIDTH, GQA_WIDTH) + (SG_WIDTH,) * 3 + (N_BRANCH * D_MODEL,)
_OFFS = [int(v) for v in np.concatenate([[0], np.cumsum(_SPLITS)])]
(O_NAQ, O_NAK, O_NAV, O_NAZ, O_GQQ, O_GQK, O_GQV, O_GQZ, O_SGU, O_SGV, O_SGZ, O_GATE, O_END) = _OFFS

PROJ_SEGMENTS = ((O_NAQ, O_NAZ), (O_GQQ, O_GQZ), (O_SGV, O_SGZ))
P_NAQ, P_NAK, P_NAV = (0, 0), (0, NA_WIDTH), (0, 2 * NA_WIDTH)
P_GQQ, P_GQK, P_GQV = (1, 0), (1, GQA_WIDTH), (1, GQA_WIDTH + GQA_KV_WIDTH)
P_SGV = (2, 0)

PROJ_TM = 1024
PROJ_SUB = 256
NA_ROWS = 4
NA_TQ = NA_ROWS * GRID_W
NA_WIN_ROWS = 12
NA_WIN = NA_WIN_ROWS * GRID_W
NA_GROUP_COUNT = ROWS // NA_ROWS
NA_GROUPS_PER_STEP = 2
NA_TYPES = 3
NA_SOFTMAX_ROWS = 16
GQA_TQ = 1024
GQA_STAGE_ROWS = 512
GQA_SOFTMAX_ROWS = 16
MERGE_TM = 1024
MERGE_SUB = 256


def _cparams(n_axes):
    return pltpu.CompilerParams(dimension_semantics=("arbitrary",) * n_axes, vmem_limit_bytes=VMEM_LIMIT)


def _fixed_spec(block_shape, index):
    return pl.BlockSpec(block_shape, lambda *_: index, pipeline_mode=pl.Buffered(1))


def _column_spec(rows, layer, start, stop):
    return _fixed_spec((pl.Element(1), pl.Element(rows), pl.Element(stop - start)), (layer, 0, start))


def _layer_norm(x, g, b):
    mu = jnp.mean(x, axis=-1, keepdims=True)
    xc = x - mu
    var = jnp.mean(xc * xc, axis=-1, keepdims=True)
    return xc * lax.rsqrt(var + LN_EPS) * g + b


def _lane_is_first_head(shape):
    return lax.broadcasted_iota(jnp.int32, shape, len(shape) - 1) % LANES < HEAD_DIM


def _na_window_start(step):
    return min(max(NA_ROWS * step - NA_KH // 2, 0), ROWS - NA_WIN_ROWS)


_NA_TYPE_STEPS = (0, 1, NA_GROUP_COUNT - 1)


def _na_tile_plan():
    plan = {}
    for t, step in enumerate(_NA_TYPE_STEPS):
        w0 = _na_window_start(step)
        for dq in range(NA_ROWS):
            r = NA_ROWS * step + dq
            row_start = min(max(r - NA_KH // 2, 0), ROWS - NA_KH)
            for j in range(NA_WIN_ROWS):
                key_row = w0 + j
                inside = row_start <= key_row < row_start + NA_KH
                plan[(t, dq, j)] = key_row - r + (NA_KH - 1) if inside else None
    return plan


def _na_bias_body(rpb_ref, out_ref):
    layer, pair = pl.program_id(0), pl.program_id(1)
    qi = lax.broadcasted_iota(jnp.int32, (GRID_W, GRID_W), 0)
    ki = lax.broadcasted_iota(jnp.int32, (GRID_W, GRID_W), 1)
    lane_entry = jnp.clip(lax.broadcasted_iota(jnp.int32, (8, LANES), 1) - (GRID_W - (NA_KW - 1)), 0, RPB_COLS - 1)
    col_start = jnp.clip(qi - NA_KW // 2, 0, GRID_W - NA_KW)
    col_valid = (ki >= col_start) & (ki < col_start + NA_KW)
    masked = jnp.full((GRID_W, GRID_W), MASK_VALUE, F32)
    plan = _na_tile_plan()
    for hsel in range(2):
        base = ((layer * NA_HEADS + 2 * pair + hsel) * RPB_ROWS) * RPB_COLS
        for ro in range(RPB_ROWS):
            users = [key for key, val in plan.items() if val == ro]
            if not users:
                continue
            pattern = jnp.zeros((8, LANES), F32)
            for c in range(RPB_COLS):
                pattern = jnp.where(lane_entry == c, rpb_ref[base + ro * RPB_COLS + c], pattern)
            rows = jnp.concatenate([pattern] * (GRID_W // 8), axis=0)
            tile = pltpu.roll(rows, GRID_W, 1, stride=1, stride_axis=0)[:, :GRID_W]
            tile = jnp.where(col_valid, tile * LOG2E, MASK_VALUE)
            for (t, dq, j) in users:
                row0 = hsel * NA_TQ + dq * GRID_W
                out_ref[0, t, 0, row0:row0 + GRID_W, j * GRID_W:(j + 1) * GRID_W] = tile
        for (t, dq, j), val in plan.items():
            if val is None:
                row0 = hsel * NA_TQ + dq * GRID_W
                out_ref[0, t, 0, row0:row0 + GRID_W, j * GRID_W:(j + 1) * GRID_W] = masked


def _na_bias(rpb_flat):
    return pl.pallas_call(
        _na_bias_body,
        grid=(DEPTH, NA_HEADS // 2),
        in_specs=[pl.BlockSpec(memory_space=pltpu.SMEM)],
        out_specs=pl.BlockSpec((1, NA_TYPES, 1, 2 * NA_TQ, NA_WIN), lambda l, p: (l, 0, p, 0, 0)),
        out_shape=jax.ShapeDtypeStruct((DEPTH, NA_TYPES, NA_HEADS // 2, 2 * NA_TQ, NA_WIN), F32),
        compiler_params=_cparams(2),
        name="na_bias",
    )(rpb_flat)


def _proj_body(first, *refs):
    if first:
        (x_ref, lng_ref, lnb_ref, w0_ref, w1_ref, w2_ref, b0_ref, b1_ref, b2_ref, gq_ref, gk_ref, bd_ref,
         cos_ref, s1_ref, s2_ref, sgg_ref, sgb_ref,
         h_ref, naq_ref, nak_ref, nav_ref, gqq_ref, gqkt_ref, gqv_ref, vn_ref) = refs
    else:
        (x_ref, w0_ref, w1_ref, w2_ref, b0_ref, b1_ref, b2_ref, gq_ref, gk_ref, bd_ref,
         cos_ref, s1_ref, s2_ref, sgg_ref, sgb_ref,
         naq_ref, nak_ref, nav_ref, gqq_ref, gqkt_ref, gqv_ref, vn_ref) = refs
    w_refs = tuple(r.at[0] for r in (w0_ref, w1_ref, w2_ref))
    b_refs = tuple(r.at[0] for r in (b0_ref, b1_ref, b2_ref))
    for r0 in range(0, PROJ_TM, PROJ_SUB):
        _proj_rows(first, slice(r0, r0 + PROJ_SUB), x_ref, lng_ref if first else None, lnb_ref if first else None,
                   w_refs, b_refs, gq_ref, gk_ref, bd_ref, cos_ref, s1_ref, s2_ref, sgg_ref, sgb_ref,
                   h_ref if first else None, naq_ref, nak_ref, nav_ref, gqq_ref, gqkt_ref, gqv_ref, vn_ref)


def _proj_rows(first, rows, x_ref, lng_ref, lnb_ref, w_refs, b_refs, gq_ref, gk_ref, bd_ref, cos_ref, s1_ref, s2_ref,
               sgg_ref, sgb_ref, h_ref, naq_ref, nak_ref, nav_ref, gqq_ref, gqkt_ref, gqv_ref, vn_ref):
    x = x_ref[rows, :]
    if first:
        x = _layer_norm(x, lng_ref[...], lnb_ref[...])
        h_ref[rows, :] = x
    xb = x.astype(BF16)

    def proj(where, width):
        seg, col = where
        return (jnp.dot(xb, w_refs[seg][:, col:col + width], preferred_element_type=F32)
                + b_refs[seg][:, col:col + width])

    def store_head_masked(dst_ref, val, col):
        first_head = _lane_is_first_head(val.shape)
        dst_ref[rows, col:col + LANES] = jnp.where(first_head, val, 0.0).astype(BF16)
        dst_ref[rows, NA_WIDTH + col:NA_WIDTH + col + LANES] = jnp.where(first_head, 0.0, val).astype(BF16)

    def store_with_ones(dst_ref, val, col, second_copy):
        first_head = _lane_is_first_head(val.shape)
        dst_ref[rows, col:col + LANES] = jnp.where(first_head, val, 1.0).astype(BF16)
        dst_ref[rows, second_copy + col:second_copy + col + LANES] = jnp.where(first_head, 1.0, val).astype(BF16)

    def rms_rope(val, gain):
        sq = val * val
        hi = sq.astype(BF16)
        lo = (sq - hi.astype(F32)).astype(BF16)
        ssum = jnp.dot(jnp.concatenate([hi, lo], axis=1), bd_ref[...], preferred_element_type=F32)
        xn = val * lax.rsqrt(ssum * (1.0 / HEAD_DIM) + RMS_EPS) * gain
        return (xn * cos_ref[rows, :] + pltpu.roll(xn, LANES - 1, 1) * s1_ref[rows, :]
                + pltpu.roll(xn, 1, 1) * s2_ref[rows, :])

    q = proj(P_NAQ, NA_WIDTH) * SCORE_SCALE
    for p in range(NA_WIDTH // LANES):
        store_head_masked(naq_ref, q[:, p * LANES:(p + 1) * LANES], p * LANES)
    nak_ref[rows, :] = proj(P_NAK, NA_WIDTH).astype(BF16)
    nav_ref[rows, :] = proj(P_NAV, NA_WIDTH).astype(BF16)

    q = proj(P_GQQ, GQA_WIDTH)
    for p in range(GQA_WIDTH // LANES):
        store_head_masked(gqq_ref, rms_rope(q[:, p * LANES:(p + 1) * LANES], gq_ref[...]), p * LANES)
    k = rms_rope(proj(P_GQK, GQA_KV_WIDTH), gk_ref[...])
    v = proj(P_GQV, GQA_KV_WIDTH)
    first_head = _lane_is_first_head(k.shape)
    k_sw = pltpu.roll(k, HEAD_DIM, 1)
    v_sw = pltpu.roll(v, HEAD_DIM, 1)
    gqkt_ref[0, 0, :, rows] = jnp.where(first_head, k, k_sw).T.astype(BF16)
    gqkt_ref[0, 1, :, rows] = jnp.where(first_head, k_sw, k).T.astype(BF16)
    store_with_ones(gqv_ref, jnp.where(first_head, v, v_sw), 0, LANES)
    store_with_ones(gqv_ref, jnp.where(first_head, v_sw, v), 2 * LANES, LANES)

    vn_ref[rows, :] = _layer_norm(proj(P_SGV, SG_WIDTH), sgg_ref[...], sgb_ref[...]).astype(BF16)


def _proj(layer, first, x, ln_g, ln_b, wb, bb, gq, gk, bd, cos_t, s1_t, s2_t, sgg, sgb):
    tm = PROJ_TM
    tiles_per_seq = SEQ // tm
    row = lambda i: (i, 0)
    pos = lambda i: (i % tiles_per_seq, 0)
    in_specs = [pl.BlockSpec((tm, D_MODEL), row)]
    args = [x]
    if first:
        in_specs += [_fixed_spec((1, D_MODEL), (0, 0)), _fixed_spec((1, D_MODEL), (0, 0))]
        args += [ln_g, ln_b]
    in_specs += [_column_spec(D_MODEL, layer, a, b) for a, b in PROJ_SEGMENTS]
    in_specs += [_column_spec(1, layer, a, b) for a, b in PROJ_SEGMENTS]
    in_specs += [_fixed_spec((None, 1, LANES), (layer, 0, 0)), _fixed_spec((None, 1, LANES), (layer, 0, 0)),
                 _fixed_spec((2 * LANES, LANES), (0, 0)),
                 pl.BlockSpec((tm, LANES), pos), pl.BlockSpec((tm, LANES), pos), pl.BlockSpec((tm, LANES), pos),
                 _fixed_spec((None, 1, SG_WIDTH), (layer, 0, 0)), _fixed_spec((None, 1, SG_WIDTH), (layer, 0, 0))]
    args += [wb] * len(PROJ_SEGMENTS) + [bb] * len(PROJ_SEGMENTS) + [gq, gk, bd, cos_t, s1_t, s2_t, sgg, sgb]
    out_shape, out_specs = [], []
    if first:
        out_shape.append(jax.ShapeDtypeStruct((TOKENS, D_MODEL), F32))
        out_specs.append(pl.BlockSpec((tm, D_MODEL), row))
    out_shape += [
        jax.ShapeDtypeStruct((TOKENS, 2 * NA_WIDTH), BF16),
        jax.ShapeDtypeStruct((TOKENS, NA_WIDTH), BF16),
        jax.ShapeDtypeStruct((TOKENS, NA_WIDTH), BF16),
        jax.ShapeDtypeStruct((TOKENS, 2 * GQA_WIDTH), BF16),
        jax.ShapeDtypeStruct((BATCH, GQA_KV_HEADS, LANES, SEQ), BF16),
        jax.ShapeDtypeStruct((TOKENS, 2 * GQA_KV_HEADS * LANES), BF16),
        jax.ShapeDtypeStruct((TOKENS, SG_WIDTH), BF16),
    ]
    out_specs += [
        pl.BlockSpec((tm, 2 * NA_WIDTH), row),
        pl.BlockSpec((tm, NA_WIDTH), row),
        pl.BlockSpec((tm, NA_WIDTH), row),
        pl.BlockSpec((tm, 2 * GQA_WIDTH), row),
        pl.BlockSpec((1, GQA_KV_HEADS, LANES, tm), lambda i: (i // tiles_per_seq, 0, 0, i % tiles_per_seq)),
        pl.BlockSpec((tm, 2 * GQA_KV_HEADS * LANES), row),
        pl.BlockSpec((tm, SG_WIDTH), row),
    ]
    return pl.pallas_call(
        functools.partial(_proj_body, first),
        grid=(TOKENS // tm,),
        in_specs=in_specs,
        out_specs=out_specs,
        out_shape=out_shape,
        compiler_params=_cparams(1),
        name="proj_first" if first else "proj",
    )(*args)


def _silu_gate(x_ref, wz_ref, bz_ref):
    z = jnp.dot(x_ref[...].astype(BF16), wz_ref[0], preferred_element_type=F32) + bz_ref[0]
    return z * jax.nn.sigmoid(z)


def _softmax_numerators(s_ref, e_ref, n_rows, block_rows, l_ref=None):
    for r in range(0, n_rows, block_rows):
        rows = slice(r, r + block_rows)
        blk = s_ref[rows, :]
        e = jnp.exp2(blk - jnp.max(blk, axis=-1, keepdims=True))
        e_ref[rows, :] = e.astype(BF16)
        if l_ref is not None:
            l_ref[rows, :] = jnp.broadcast_to(jnp.sum(e, axis=-1, keepdims=True), (block_rows, LANES))


def _weighted_values(e, v_ones):
    o = jnp.dot(e, v_ones, preferred_element_type=F32)
    return o / pltpu.roll(o, HEAD_DIM, 1)


def _run_stages(n_stages, scores, finish, before_last):
    scores(0)
    for i in range(n_stages):
        if i + 1 < n_stages:
            scores(i + 1)
        else:
            before_last()
        finish(i)


def _na_body(q_ref, k_ref, v_ref, bias0_ref, bias1_ref, x_ref, wz_ref, bz_ref, out_ref,
             s_scr, e_scr, l_scr, y_scr, g_scr):
    step = pl.program_id(1)
    bias_refs = (bias0_ref, bias1_ref)
    first_head = _lane_is_first_head((NA_TQ, LANES))
    stages = [(g, p) for g in range(NA_GROUPS_PER_STEP) for p in range(NA_WIDTH // LANES)]

    def window(g):
        w0 = jnp.clip(NA_ROWS * (NA_GROUPS_PER_STEP * step + g) - NA_KH // 2, 0, ROWS - NA_WIN_ROWS)
        return pl.ds(pl.multiple_of(w0 * GRID_W, 2 * LANES), NA_WIN)

    def scores(i):
        g, p = stages[i]
        rows, col = slice(g * NA_TQ, (g + 1) * NA_TQ), p * LANES
        qs = jnp.concatenate([q_ref[rows, col:col + LANES], q_ref[rows, NA_WIDTH + col:NA_WIDTH + col + LANES]], axis=0)
        s = lax.dot_general(qs, k_ref[window(g), col:col + LANES], (((1,), (1,)), ((), ())), preferred_element_type=F32)
        s_scr[i % 2] = s + bias_refs[g][0, 0, p]

    def finish(i):
        g, p = stages[i]
        rows, col, slot = slice(g * NA_TQ, (g + 1) * NA_TQ), p * LANES, i % 2
        _softmax_numerators(s_scr.at[slot], e_scr.at[slot], 2 * NA_TQ, NA_SOFTMAX_ROWS, l_scr.at[slot])
        o = jnp.dot(e_scr[slot], v_ref[window(g), col:col + LANES], preferred_element_type=F32) / l_scr[slot]
        y_scr[rows, col:col + LANES] = jnp.where(first_head, o[:NA_TQ], o[NA_TQ:])

    def gate():
        g_scr[...] = _silu_gate(x_ref, wz_ref, bz_ref)

    _run_stages(len(stages), scores, finish, gate)
    out_ref[...] = (y_scr[...] * g_scr[...]).astype(BF16)


def _na(layer, q2, k, v, bias, x, wb, bb):
    steps = NA_GROUP_COUNT // NA_GROUPS_PER_STEP
    tq = NA_GROUPS_PER_STEP * NA_TQ

    def bias_spec(g):
        def index(b, s):
            group = NA_GROUPS_PER_STEP * s + g
            return (layer, jnp.minimum(group, 1) + jnp.maximum(group - (NA_GROUP_COUNT - 2), 0), 0, 0, 0)
        return pl.BlockSpec((1, 1, NA_HEADS // 2, 2 * NA_TQ, NA_WIN), index)

    row = lambda b, s: (b * steps + s, 0)
    return pl.pallas_call(
        _na_body,
        grid=(BATCH, steps),
        in_specs=[
            pl.BlockSpec((tq, 2 * NA_WIDTH), row),
            pl.BlockSpec((SEQ, NA_WIDTH), lambda b, s: (b, 0)),
            pl.BlockSpec((SEQ, NA_WIDTH), lambda b, s: (b, 0)),
            bias_spec(0), bias_spec(1),
            pl.BlockSpec((tq, D_MODEL), row),
            _column_spec(D_MODEL, layer, O_NAZ, O_GQQ),
            _column_spec(1, layer, O_NAZ, O_GQQ),
        ],
        out_specs=pl.BlockSpec((tq, NA_WIDTH), row),
        out_shape=jax.ShapeDtypeStruct((TOKENS, NA_WIDTH), BF16),
        scratch_shapes=[pltpu.VMEM((2, 2 * NA_TQ, NA_WIN), F32), pltpu.VMEM((2, 2 * NA_TQ, NA_WIN), BF16),
                        pltpu.VMEM((2, 2 * NA_TQ, LANES), F32), pltpu.VMEM((tq, NA_WIDTH), F32),
                        pltpu.VMEM((tq, NA_WIDTH), F32)],
        compiler_params=_cparams(2),
        name="na",
    )(q2, k, v, bias, bias, x, wb, bb)


def _gqa_body(q_ref, kt_ref, v_ref, x_ref, wz_ref, bz_ref, out_ref, s_scr, e_scr, y_scr, g_scr):
    m = GQA_STAGE_ROWS
    tiles_per_kv = GQA_WIDTH // LANES // GQA_KV_HEADS
    first_head = _lane_is_first_head((m, LANES))
    stages = [(p, rb, h) for p in range(GQA_WIDTH // LANES) for rb in range(GQA_TQ // m) for h in range(2)]

    def scores(i):
        p, rb, h = stages[i]
        col = h * GQA_WIDTH + p * LANES
        s_scr[i % 2] = jnp.dot(q_ref[rb * m:(rb + 1) * m, col:col + LANES], kt_ref[0, p // tiles_per_kv],
                               preferred_element_type=F32)

    def finish(i):
        p, rb, h = stages[i]
        rows, col, slot, kv = slice(rb * m, (rb + 1) * m), p * LANES, i % 2, p // tiles_per_kv
        _softmax_numerators(s_scr.at[slot], e_scr.at[slot], m, GQA_SOFTMAX_ROWS)
        vcol = (2 * kv + h) * LANES
        o = _weighted_values(e_scr[slot], v_ref[:, vcol:vcol + LANES])
        if h == 0:
            y_scr[rows, col:col + LANES] = o
        else:
            y_scr[rows, col:col + LANES] = jnp.where(first_head, y_scr[rows, col:col + LANES], o)

    def gate():
        g_scr[...] = _silu_gate(x_ref, wz_ref, bz_ref)

    _run_stages(len(stages), scores, finish, gate)
    out_ref[...] = (y_scr[...] * g_scr[...]).astype(BF16)


def _gqa(layer, q2, kt, v, x, wb, bb):
    steps = SEQ // GQA_TQ
    row = lambda b, s: (b * steps + s, 0)
    return pl.pallas_call(
        _gqa_body,
        grid=(BATCH, steps),
        in_specs=[
            pl.BlockSpec((GQA_TQ, 2 * GQA_WIDTH), row),
            pl.BlockSpec((1, GQA_KV_HEADS, LANES, SEQ), lambda b, s: (b, 0, 0, 0)),
            pl.BlockSpec((SEQ, 2 * GQA_KV_HEADS * LANES), lambda b, s: (b, 0)),
            pl.BlockSpec((GQA_TQ, D_MODEL), row),
            _column_spec(D_MODEL, layer, O_GQZ, O_SGU),
            _column_spec(1, layer, O_GQZ, O_SGU),
        ],
        out_specs=pl.BlockSpec((GQA_TQ, GQA_WIDTH), row),
        out_shape=jax.ShapeDtypeStruct((TOKENS, GQA_WIDTH), BF16),
        scratch_shapes=[pltpu.VMEM((2, GQA_STAGE_ROWS, SEQ), F32), pltpu.VMEM((2, GQA_STAGE_ROWS, SEQ), BF16),
                        pltpu.VMEM((GQA_TQ, GQA_WIDTH), F32), pltpu.VMEM((GQA_TQ, GQA_WIDTH), F32)],
        compiler_params=_cparams(2),
        name="gqa",
    )(q2, kt, v, x, wb, bb)


def _merge_body(x_ref, ya_ref, yb_ref, vn_ref, wu_ref, bu_ref, wz_ref, bz_ref, ws_ref, bs_ref,
                wg_ref, bg_ref, wo_ref, wa_ref, wb_ref, wc_ref, bo_ref, lng_ref, lnb_ref, out_ref, mixed_ref):
    for r0 in range(0, MERGE_TM, MERGE_SUB):
        _merge_rows(slice(r0, r0 + MERGE_SUB), x_ref, ya_ref, yb_ref, vn_ref, wu_ref, bu_ref, wz_ref, bz_ref, ws_ref,
                    bs_ref, wg_ref, bg_ref, wo_ref, wa_ref, wb_ref, wc_ref, bo_ref, lng_ref, lnb_ref, out_ref, mixed_ref)


def _merge_rows(rows, x_ref, ya_ref, yb_ref, vn_ref, wu_ref, bu_ref, wz_ref, bz_ref, ws_ref, bs_ref,
                wg_ref, bg_ref, wo_ref, wa_ref, wb_ref, wc_ref, bo_ref, lng_ref, lnb_ref, out_ref, mixed_ref):
    n_chunks = MERGE_SUB // SG_CHUNK
    x = x_ref[rows, :]
    xb = x.astype(BF16)

    first_head = _lane_is_first_head((SG_CHUNK, LANES))
    chunk_rows = [slice(rows.start + c * SG_CHUNK, rows.start + (c + 1) * SG_CHUNK) for c in range(n_chunks)]
    for pp in range(SG_WIDTH // LANES):
        col = pp * LANES
        rhs = jnp.concatenate([vn_ref[cr, col:col + LANES] for cr in chunk_rows], axis=1)
        res = jnp.dot(ws_ref[pp], rhs, preferred_element_type=F32)
        for c, cr in enumerate(chunk_rows):
            blk = jnp.where(first_head, res[:SG_CHUNK, c * LANES:(c + 1) * LANES], res[SG_CHUNK:, c * LANES:(c + 1) * LANES])
            mixed_ref[cr, col:col + LANES] = blk + bs_ref[:, col:col + LANES]
    u = jnp.dot(xb, wu_ref[0], preferred_element_type=F32) + bu_ref[0]
    z = jnp.dot(xb, wz_ref[0], preferred_element_type=F32) + bz_ref[0]
    tc = (u * mixed_ref[rows, :] * (z * jax.nn.sigmoid(z))).astype(BF16)

    def gate(i):
        lo = i * D_MODEL
        return jax.nn.sigmoid(jnp.dot(xb, wg_ref[0, :, lo:lo + D_MODEL], preferred_element_type=F32)
                              + bg_ref[0, :, lo:lo + D_MODEL])

    merged = gate(0) * jnp.dot(ya_ref[rows, :], wa_ref[...], preferred_element_type=F32)
    merged = merged + gate(1) * jnp.dot(yb_ref[rows, :], wb_ref[...], preferred_element_type=F32)
    merged = merged + gate(2) * jnp.dot(tc, wc_ref[...], preferred_element_type=F32)
    sub = jnp.dot(merged.astype(BF16), wo_ref[...], preferred_element_type=F32) + bo_ref[...]
    out_ref[rows, :] = _layer_norm(DEEPNORM_ALPHA * x + sub, lng_ref[...], lnb_ref[...])


def _merge(layer, x, ya, yb, vn, wb, bb, ws, bs, wo, wa, wbb, wc, bo, lng, lnb):
    tm = MERGE_TM
    row = lambda i: (i, 0)
    vec = lambda width: _fixed_spec((None, 1, width), (layer, 0, 0))
    return pl.pallas_call(
        _merge_body,
        grid=(TOKENS // tm,),
        in_specs=[
            pl.BlockSpec((tm, D_MODEL), row),
            pl.BlockSpec((tm, NA_WIDTH), row),
            pl.BlockSpec((tm, GQA_WIDTH), row),
            pl.BlockSpec((tm, SG_WIDTH), row),
            _column_spec(D_MODEL, layer, O_SGU, O_SGV), _column_spec(1, layer, O_SGU, O_SGV),
            _column_spec(D_MODEL, layer, O_SGZ, O_GATE), _column_spec(1, layer, O_SGZ, O_GATE),
            _fixed_spec((None, SG_GROUPS // 2, 2 * SG_CHUNK, SG_CHUNK), (layer, 0, 0, 0)),
            _fixed_spec((None, SG_CHUNK, SG_WIDTH), (layer, 0, 0)),
            _column_spec(D_MODEL, layer, O_GATE, O_END), _column_spec(1, layer, O_GATE, O_END),
            _fixed_spec((None, D_MODEL, D_MODEL), (layer, 0, 0)),
            _fixed_spec((None, NA_WIDTH, D_MODEL), (layer, 0, 0)),
            _fixed_spec((None, GQA_WIDTH, D_MODEL), (layer, 0, 0)),
            _fixed_spec((None, SG_WIDTH, D_MODEL), (layer, 0, 0)),
            vec(D_MODEL), vec(D_MODEL), vec(D_MODEL),
        ],
        out_specs=pl.BlockSpec((tm, D_MODEL), row),
        out_shape=jax.ShapeDtypeStruct((TOKENS, D_MODEL), F32),
        scratch_shapes=[pltpu.VMEM((tm, SG_WIDTH), F32)],
        compiler_params=_cparams(1),
        name="merge",
    )(x, ya, yb, vn, wb, bb, wb, bb, ws, bs, wb, bb, wo, wa, wbb, wc, bo, lng, lnb)


def _rope_tables():
    t = np.arange(SEQ)
    row = (t // GRID_W).astype(np.float32)
    col = (t % GRID_W).astype(np.float32)
    freqs = np.float32(ROPE_THETA) ** (-np.arange(0, ROPE_AXIS_DIM, 2, dtype=np.float32) / np.float32(ROPE_AXIS_DIM))
    ang = np.concatenate([row[:, None] * freqs, col[:, None] * freqs], axis=-1).astype(np.float32)
    cos = np.cos(ang.astype(np.float64)).astype(np.float32)
    sin = np.sin(ang.astype(np.float64)).astype(np.float32)
    pair = (np.arange(LANES) % HEAD_DIM) // 2
    even = (np.arange(LANES) % 2 == 0)[None, :]
    cos_t = cos[:, pair]
    s1_t = np.where(even, -sin[:, pair], 0.0).astype(np.float32)
    s2_t = np.where(even, 0.0, sin[:, pair]).astype(np.float32)
    return jnp.asarray(cos_t), jnp.asarray(s1_t), jnp.asarray(s2_t)


def kernel(x, ln_in_g, ln_in_b, w_in, b_in, na_rpb, q_norm_g, k_norm_g, sg_ln_g, sg_ln_b, sg_w, sg_b,
           w_br_a, w_br_b, w_br_c, w_out, b_out, ln_post_g, ln_post_b):
    assert x.shape == (BATCH, SEQ, D_MODEL) and w_in.shape == (DEPTH, D_MODEL, O_END)
    cos_t, s1_t, s2_t = _rope_tables()
    bd = jnp.asarray(np.tile(np.kron(np.eye(2), np.ones((HEAD_DIM, HEAD_DIM))), (2, 1)), BF16)
    vec3 = lambda v: v.reshape(DEPTH, 1, -1)

    wb, bb = w_in.astype(BF16), vec3(b_in)
    wo, wa, wbb, wc = (w.astype(BF16) for w in (w_out, w_br_a, w_br_b, w_br_c))
    ws = sg_w.reshape(DEPTH, SG_GROUPS // 2, 2 * SG_CHUNK, SG_CHUNK).astype(BF16)
    bs = jnp.repeat(jnp.swapaxes(sg_b, 1, 2), HEAD_DIM, axis=2)
    gq = vec3(jnp.tile(q_norm_g * SCORE_SCALE, (1, 2)))
    gk = vec3(jnp.tile(k_norm_g, (1, 2)))
    sgg, sgb = vec3(sg_ln_g), vec3(sg_ln_b)
    bo, lng, lnb = vec3(b_out), vec3(ln_post_g), vec3(ln_post_b)
    bias = _na_bias(na_rpb.reshape(-1))

    h = x.reshape(TOKENS, D_MODEL)
    for l in range(DEPTH):
        proj_args = (wb, bb, gq, gk, bd, cos_t, s1_t, s2_t, sgg, sgb)
        if l == 0:
            h, naq, nak, nav, gqq, gqkt, gqv, vn = _proj(l, True, h, ln_in_g.reshape(1, -1), ln_in_b.reshape(1, -1), *proj_args)
        else:
            naq, nak, nav, gqq, gqkt, gqv, vn = _proj(l, False, h, None, None, *proj_args)
        ya = _na(l, naq, nak, nav, bias, h, wb, bb)
        yb = _gqa(l, gqq, gqkt, gqv, h, wb, bb)
        h = _merge(l, h, ya, yb, vn, wb, bb, ws, bs, wo, wa, wbb, wc, bo, lng, lnb)
    return h.reshape(BATCH, SEQ, D_MODEL)
```

```python
import functools

import numpy as np
import jax
import jax.numpy as jnp
from jax import lax
from jax.experimental import pallas as pl
from jax.experimental.pallas import tpu as pltpu

F32 = jnp.float32
BF16 = jnp.bfloat16

D_MODEL = 1024
BATCH = 8
SEQ = 2048
DEPTH = 2
GRID_W = 64
ROWS = SEQ // GRID_W
HEAD_DIM = 64
NA_HEADS = 8
NA_WIDTH = 512
NA_KH = 8
NA_KW = 16
RPB_ROWS = 2 * NA_KH - 1
RPB_COLS = 2 * NA_KW - 1
GQA_HEADS = 8
GQA_KV_HEADS = 2
GQA_WIDTH = 512
GQA_KV_WIDTH = 128
ROPE_THETA = 10000.0
ROPE_AXIS_DIM = HEAD_DIM // 2
SG_WIDTH = 512
SG_GROUPS = 8
SG_CHUNK = 128
N_BRANCH = 3
LN_EPS = 1e-5
RMS_EPS = 1e-6
DEEPNORM_ALPHA = (2.0 * DEPTH) ** 0.25
ATTN_SCALE = HEAD_DIM ** -0.5
LOG2E = 1.4426950408889634
SCORE_SCALE = ATTN_SCALE * LOG2E
MASK_VALUE = -1e30

TOKENS = BATCH * SEQ
LANES = 128
VMEM_LIMIT = 56 * 1024 * 1024

_SPLITS = (NA_WIDTH,) * 4 + (GQA_WIDTH, GQA_KV_WIDTH, GQA_KV_WIDTH, GQA_WIDTH) + (SG_WIDTH,) * 3 + (N_BRANCH * D_MODEL,)
_OFFS = [int(v) for v in np.concatenate([[0], np.cumsum(_SPLITS)])]
(O_NAQ, O_NAK, O_NAV, O_NAZ, O_GQQ, O_GQK, O_GQV, O_GQZ, O_SGU, O_SGV, O_SGZ, O_GATE, O_END) = _OFFS

PROJ_SEGMENTS = ((O_NAQ, O_NAZ), (O_GQQ, O_GQZ), (O_SGV, O_SGZ))
P_NAQ, P_NAK, P_NAV = (0, 0), (0, NA_WIDTH), (0, 2 * NA_WIDTH)
P_GQQ, P_GQK, P_GQV = (1, 0), (1, GQA_WIDTH), (1, GQA_WIDTH + GQA_KV_WIDTH)
P_SGV = (2, 0)

PROJ_TM = 1024
PROJ_SUB = 256
NA_ROWS = 4
NA_TQ = NA_ROWS * GRID_W
NA_WIN_ROWS = 12
NA_WIN = NA_WIN_ROWS * GRID_W
NA_GROUP_COUNT = ROWS // NA_ROWS
NA_GROUPS_PER_STEP = 2
NA_TYPES = 3
NA_SOFTMAX_ROWS = 16
GQA_TQ = 1024
GQA_STAGE_ROWS = 512
GQA_SOFTMAX_ROWS = 16
MERGE_TM = 1024
MERGE_SUB = 256


def _cparams(n_axes):
    return pltpu.CompilerParams(dimension_semantics=("arbitrary",) * n_axes, vmem_limit_bytes=VMEM_LIMIT)


def _fixed_spec(block_shape, index):
    return pl.BlockSpec(block_shape, lambda *_: index, pipeline_mode=pl.Buffered(1))


def _column_spec(rows, layer, start, stop):
    return _fixed_spec((pl.Element(1), pl.Element(rows), pl.Element(stop - start)), (layer, 0, start))


def _layer_norm(x, g, b):
    mu = jnp.mean(x, axis=-1, keepdims=True)
    xc = x - mu
    var = jnp.mean(xc * xc, axis=-1, keepdims=True)
    return xc * lax.rsqrt(var + LN_EPS) * g + b


def _lane_is_first_head(shape):
    return lax.broadcasted_iota(jnp.int32, shape, len(shape) - 1) % LANES < HEAD_DIM


def _head_lane_masks():
    first = jnp.where(_lane_is_first_head((1, LANES)), 1.0, 0.0)
    return first.astype(BF16), (1.0 - first).astype(BF16)


def _na_window_start(step):
    return min(max(NA_ROWS * step - NA_KH // 2, 0), ROWS - NA_WIN_ROWS)


_NA_TYPE_STEPS = (0, 1, NA_GROUP_COUNT - 1)


def _na_tile_plan():
    plan = {}
    for t, step in enumerate(_NA_TYPE_STEPS):
        w0 = _na_window_start(step)
        for dq in range(NA_ROWS):
            r = NA_ROWS * step + dq
            row_start = min(max(r - NA_KH // 2, 0), ROWS - NA_KH)
            for j in range(NA_WIN_ROWS):
                key_row = w0 + j
                inside = row_start <= key_row < row_start + NA_KH
                plan[(t, dq, j)] = key_row - r + (NA_KH - 1) if inside else None
    return plan


def _na_bias_body(rpb_ref, out_ref):
    layer, pair = pl.program_id(0), pl.program_id(1)
    qi = lax.broadcasted_iota(jnp.int32, (GRID_W, GRID_W), 0)
    ki = lax.broadcasted_iota(jnp.int32, (GRID_W, GRID_W), 1)
    lane_entry = jnp.clip(lax.broadcasted_iota(jnp.int32, (8, LANES), 1) - (GRID_W - (NA_KW - 1)), 0, RPB_COLS - 1)
    col_start = jnp.clip(qi - NA_KW // 2, 0, GRID_W - NA_KW)
    col_valid = (ki >= col_start) & (ki < col_start + NA_KW)
    masked = jnp.full((GRID_W, GRID_W), MASK_VALUE, F32)
    plan = _na_tile_plan()
    for hsel in range(2):
        base = ((layer * NA_HEADS + 2 * pair + hsel) * RPB_ROWS) * RPB_COLS
        for ro in range(RPB_ROWS):
            users = [key for key, val in plan.items() if val == ro]
            if not users:
                continue
            pattern = jnp.zeros((8, LANES), F32)
            for c in range(RPB_COLS):
                pattern = jnp.where(lane_entry == c, rpb_ref[base + ro * RPB_COLS + c], pattern)
            rows = jnp.concatenate([pattern] * (GRID_W // 8), axis=0)
            tile = pltpu.roll(rows, GRID_W, 1, stride=1, stride_axis=0)[:, :GRID_W]
            tile = jnp.where(col_valid, tile * LOG2E, MASK_VALUE)
            for (t, dq, j) in users:
                row0 = hsel * NA_TQ + dq * GRID_W
                out_ref[0, t, 0, row0:row0 + GRID_W, j * GRID_W:(j + 1) * GRID_W] = tile
        for (t, dq, j), val in plan.items():
            if val is None:
                row0 = hsel * NA_TQ + dq * GRID_W
                out_ref[0, t, 0, row0:row0 + GRID_W, j * GRID_W:(j + 1) * GRID_W] = masked


def _na_bias(rpb_flat):
    return pl.pallas_call(
        _na_bias_body,
        grid=(DEPTH, NA_HEADS // 2),
        in_specs=[pl.BlockSpec(memory_space=pltpu.SMEM)],
        out_specs=pl.BlockSpec((1, NA_TYPES, 1, 2 * NA_TQ, NA_WIN), lambda l, p: (l, 0, p, 0, 0)),
        out_shape=jax.ShapeDtypeStruct((DEPTH, NA_TYPES, NA_HEADS // 2, 2 * NA_TQ, NA_WIN), F32),
        compiler_params=_cparams(2),
        name="na_bias",
    )(rpb_flat)


def _proj_body(first, *refs):
    if first:
        (x_ref, lng_ref, lnb_ref, w0_ref, w1_ref, w2_ref, b0_ref, b1_ref, b2_ref, gq_ref, gk_ref, bd_ref,
         cos_ref, s1_ref, s2_ref, sgg_ref, sgb_ref,
         h_ref, naq_ref, nak_ref, nav_ref, gqq_ref, gqkt_ref, gqv_ref, vn_ref) = refs
    else:
        (x_ref, w0_ref, w1_ref, w2_ref, b0_ref, b1_ref, b2_ref, gq_ref, gk_ref, bd_ref,
         cos_ref, s1_ref, s2_ref, sgg_ref, sgb_ref,
         naq_ref, nak_ref, nav_ref, gqq_ref, gqkt_ref, gqv_ref, vn_ref) = refs
    w_refs = tuple(r.at[0] for r in (w0_ref, w1_ref, w2_ref))
    b_refs = tuple(r.at[0] for r in (b0_ref, b1_ref, b2_ref))
    for r0 in range(0, PROJ_TM, PROJ_SUB):
        _proj_rows(first, slice(r0, r0 + PROJ_SUB), x_ref, lng_ref if first else None, lnb_ref if first else None,
                   w_refs, b_refs, gq_ref, gk_ref, bd_ref, cos_ref, s1_ref, s2_ref, sgg_ref, sgb_ref,
                   h_ref if first else None, naq_ref, nak_ref, nav_ref, gqq_ref, gqkt_ref, gqv_ref, vn_ref)


def _proj_rows(first, rows, x_ref, lng_ref, lnb_ref, w_refs, b_refs, gq_ref, gk_ref, bd_ref, cos_ref, s1_ref, s2_ref,
               sgg_ref, sgb_ref, h_ref, naq_ref, nak_ref, nav_ref, gqq_ref, gqkt_ref, gqv_ref, vn_ref):
    x = x_ref[rows, :]
    if first:
        x = _layer_norm(x, lng_ref[...], lnb_ref[...])
        h_ref[rows, :] = x
    xb = x.astype(BF16)

    def proj(where, width):
        seg, col = where
        return (jnp.dot(xb, w_refs[seg][:, col:col + width], preferred_element_type=F32)
                + b_refs[seg][:, col:col + width])

    def store_with_ones(dst_ref, val, col, second_copy):
        first_head = _lane_is_first_head(val.shape)
        dst_ref[rows, col:col + LANES] = jnp.where(first_head, val, 1.0).astype(BF16)
        dst_ref[rows, second_copy + col:second_copy + col + LANES] = jnp.where(first_head, 1.0, val).astype(BF16)

    def rms_rope(val, gain):
        sq = val * val
        hi = sq.astype(BF16)
        lo = (sq - hi.astype(F32)).astype(BF16)
        ssum = jnp.dot(jnp.concatenate([hi, lo], axis=1), bd_ref[...], preferred_element_type=F32)
        xn = val * lax.rsqrt(ssum * (1.0 / HEAD_DIM) + RMS_EPS) * gain
        return (xn * cos_ref[rows, :] + pltpu.roll(xn, LANES - 1, 1) * s1_ref[rows, :]
                + pltpu.roll(xn, 1, 1) * s2_ref[rows, :])

    naq_ref[rows, :] = (proj(P_NAQ, NA_WIDTH) * SCORE_SCALE).astype(BF16)
    nak_ref[rows, :] = proj(P_NAK, NA_WIDTH).astype(BF16)
    nav_ref[rows, :] = proj(P_NAV, NA_WIDTH).astype(BF16)

    q = proj(P_GQQ, GQA_WIDTH)
    for p in range(GQA_WIDTH // LANES):
        gqq_ref[rows, p * LANES:(p + 1) * LANES] = rms_rope(q[:, p * LANES:(p + 1) * LANES], gq_ref[...]).astype(BF16)
    k = rms_rope(proj(P_GQK, GQA_KV_WIDTH), gk_ref[...])
    v = proj(P_GQV, GQA_KV_WIDTH)
    first_head = _lane_is_first_head(k.shape)
    k_sw = pltpu.roll(k, HEAD_DIM, 1)
    v_sw = pltpu.roll(v, HEAD_DIM, 1)
    gqkt_ref[0, 0, :, rows] = jnp.where(first_head, k, k_sw).T.astype(BF16)
    gqkt_ref[0, 1, :, rows] = jnp.where(first_head, k_sw, k).T.astype(BF16)
    store_with_ones(gqv_ref, jnp.where(first_head, v, v_sw), 0, LANES)
    store_with_ones(gqv_ref, jnp.where(first_head, v_sw, v), 2 * LANES, LANES)

    vn_ref[rows, :] = _layer_norm(proj(P_SGV, SG_WIDTH), sgg_ref[...], sgb_ref[...]).astype(BF16)


def _proj(layer, first, x, ln_g, ln_b, wb, bb, gq, gk, bd, cos_t, s1_t, s2_t, sgg, sgb):
    tm = PROJ_TM
    tiles_per_seq = SEQ // tm
    row = lambda i: (i, 0)
    pos = lambda i: (i % tiles_per_seq, 0)
    in_specs = [pl.BlockSpec((tm, D_MODEL), row)]
    args = [x]
    if first:
        in_specs += [_fixed_spec((1, D_MODEL), (0, 0)), _fixed_spec((1, D_MODEL), (0, 0))]
        args += [ln_g, ln_b]
    in_specs += [_column_spec(D_MODEL, layer, a, b) for a, b in PROJ_SEGMENTS]
    in_specs += [_column_spec(1, layer, a, b) for a, b in PROJ_SEGMENTS]
    in_specs += [_fixed_spec((None, 1, LANES), (layer, 0, 0)), _fixed_spec((None, 1, LANES), (layer, 0, 0)),
                 _fixed_spec((2 * LANES, LANES), (0, 0)),
                 pl.BlockSpec((tm, LANES), pos), pl.BlockSpec((tm, LANES), pos), pl.BlockSpec((tm, LANES), pos),
                 _fixed_spec((None, 1, SG_WIDTH), (layer, 0, 0)), _fixed_spec((None, 1, SG_WIDTH), (layer, 0, 0))]
    args += [wb] * len(PROJ_SEGMENTS) + [bb] * len(PROJ_SEGMENTS) + [gq, gk, bd, cos_t, s1_t, s2_t, sgg, sgb]
    out_shape, out_specs = [], []
    if first:
        out_shape.append(jax.ShapeDtypeStruct((TOKENS, D_MODEL), F32))
        out_specs.append(pl.BlockSpec((tm, D_MODEL), row))
    out_shape += [
        jax.ShapeDtypeStruct((TOKENS, NA_WIDTH), BF16),
        jax.ShapeDtypeStruct((TOKENS, NA_WIDTH), BF16),
        jax.ShapeDtypeStruct((TOKENS, NA_WIDTH), BF16),
        jax.ShapeDtypeStruct((TOKENS, GQA_WIDTH), BF16),
        jax.ShapeDtypeStruct((BATCH, GQA_KV_HEADS, LANES, SEQ), BF16),
        jax.ShapeDtypeStruct((TOKENS, 2 * GQA_KV_HEADS * LANES), BF16),
        jax.ShapeDtypeStruct((TOKENS, SG_WIDTH), BF16),
    ]
    out_specs += [
        pl.BlockSpec((tm, NA_WIDTH), row),
        pl.BlockSpec((tm, NA_WIDTH), row),
        pl.BlockSpec((tm, NA_WIDTH), row),
        pl.BlockSpec((tm, GQA_WIDTH), row),
        pl.BlockSpec((1, GQA_KV_HEADS, LANES, tm), lambda i: (i // tiles_per_seq, 0, 0, i % tiles_per_seq)),
        pl.BlockSpec((tm, 2 * GQA_KV_HEADS * LANES), row),
        pl.BlockSpec((tm, SG_WIDTH), row),
    ]
    return pl.pallas_call(
        functools.partial(_proj_body, first),
        grid=(TOKENS // tm,),
        in_specs=in_specs,
        out_specs=out_specs,
        out_shape=out_shape,
        compiler_params=_cparams(1),
        name="proj_first" if first else "proj",
    )(*args)


def _silu_gate(x_ref, wz_ref, bz_ref):
    z = jnp.dot(x_ref[...].astype(BF16), wz_ref[0], preferred_element_type=F32) + bz_ref[0]
    return z * jax.nn.sigmoid(z)


def _softmax_numerators(s_ref, e_ref, n_rows, block_rows, l_ref=None):
    for r in range(0, n_rows, block_rows):
        rows = slice(r, r + block_rows)
        blk = s_ref[rows, :]
        e = jnp.exp2(blk - jnp.max(blk, axis=-1, keepdims=True))
        e_ref[rows, :] = e.astype(BF16)
        if l_ref is not None:
            l_ref[rows, :] = jnp.broadcast_to(jnp.sum(e, axis=-1, keepdims=True), (block_rows, LANES))


def _weighted_values(e, v_ones):
    o = jnp.dot(e, v_ones, preferred_element_type=F32)
    return o / pltpu.roll(o, HEAD_DIM, 1)


def _run_stages(n_stages, scores, finish, before_last):
    scores(0)
    for i in range(n_stages):
        if i + 1 < n_stages:
            scores(i + 1)
        else:
            before_last()
        finish(i)


def _na_body(q_ref, k_ref, v_ref, bias0_ref, bias1_ref, x_ref, wz_ref, bz_ref, out_ref,
             s_scr, e_scr, l_scr, y_scr, g_scr):
    step = pl.program_id(1)
    bias_refs = (bias0_ref, bias1_ref)
    first_head = _lane_is_first_head((NA_TQ, LANES))
    keep_first, keep_second = _head_lane_masks()
    stages = [(g, p) for g in range(NA_GROUPS_PER_STEP) for p in range(NA_WIDTH // LANES)]

    def window(g):
        w0 = jnp.clip(NA_ROWS * (NA_GROUPS_PER_STEP * step + g) - NA_KH // 2, 0, ROWS - NA_WIN_ROWS)
        return pl.ds(pl.multiple_of(w0 * GRID_W, 2 * LANES), NA_WIN)

    def scores(i):
        g, p = stages[i]
        rows, col = slice(g * NA_TQ, (g + 1) * NA_TQ), p * LANES
        q = q_ref[rows, col:col + LANES]
        qs = jnp.concatenate([q * keep_first, q * keep_second], axis=0)
        s = lax.dot_general(qs, k_ref[window(g), col:col + LANES], (((1,), (1,)), ((), ())), preferred_element_type=F32)
        s_scr[i % 2] = s + bias_refs[g][0, 0, p]

    def finish(i):
        g, p = stages[i]
        rows, col, slot = slice(g * NA_TQ, (g + 1) * NA_TQ), p * LANES, i % 2
        _softmax_numerators(s_scr.at[slot], e_scr.at[slot], 2 * NA_TQ, NA_SOFTMAX_ROWS, l_scr.at[slot])
        o = jnp.dot(e_scr[slot], v_ref[window(g), col:col + LANES], preferred_element_type=F32) / l_scr[slot]
        y_scr[rows, col:col + LANES] = jnp.where(first_head, o[:NA_TQ], o[NA_TQ:])

    def gate():
        g_scr[...] = _silu_gate(x_ref, wz_ref, bz_ref)

    _run_stages(len(stages), scores, finish, gate)
    out_ref[...] = (y_scr[...] * g_scr[...]).astype(BF16)


def _na(layer, q2, k, v, bias, x, wb, bb):
    steps = NA_GROUP_COUNT // NA_GROUPS_PER_STEP
    tq = NA_GROUPS_PER_STEP * NA_TQ

    def bias_spec(g):
        def index(b, s):
            group = NA_GROUPS_PER_STEP * s + g
            return (layer, jnp.minimum(group, 1) + jnp.maximum(group - (NA_GROUP_COUNT - 2), 0), 0, 0, 0)
        return pl.BlockSpec((1, 1, NA_HEADS // 2, 2 * NA_TQ, NA_WIN), index)

    row = lambda b, s: (b * steps + s, 0)
    return pl.pallas_call(
        _na_body,
        grid=(BATCH, steps),
        in_specs=[
            pl.BlockSpec((tq, NA_WIDTH), row),
            pl.BlockSpec((SEQ, NA_WIDTH), lambda b, s: (b, 0)),
            pl.BlockSpec((SEQ, NA_WIDTH), lambda b, s: (b, 0)),
            bias_spec(0), bias_spec(1),
            pl.BlockSpec((tq, D_MODEL), row),
            _column_spec(D_MODEL, layer, O_NAZ, O_GQQ),
            _column_spec(1, layer, O_NAZ, O_GQQ),
        ],
        out_specs=pl.BlockSpec((tq, NA_WIDTH), row),
        out_shape=jax.ShapeDtypeStruct((TOKENS, NA_WIDTH), BF16),
        scratch_shapes=[pltpu.VMEM((2, 2 * NA_TQ, NA_WIN), F32), pltpu.VMEM((2, 2 * NA_TQ, NA_WIN), BF16),
                        pltpu.VMEM((2, 2 * NA_TQ, LANES), F32), pltpu.VMEM((tq, NA_WIDTH), F32),
                        pltpu.VMEM((tq, NA_WIDTH), F32)],
        compiler_params=_cparams(2),
        name="na",
    )(q2, k, v, bias, bias, x, wb, bb)


def _gqa_body(q_ref, kt_ref, v_ref, x_ref, wz_ref, bz_ref, out_ref, s_scr, e_scr, y_scr, g_scr):
    m = GQA_STAGE_ROWS
    tiles_per_kv = GQA_WIDTH // LANES // GQA_KV_HEADS
    first_head = _lane_is_first_head((m, LANES))
    keep_head = _head_lane_masks()
    stages = [(p, rb, h) for p in range(GQA_WIDTH // LANES) for rb in range(GQA_TQ // m) for h in range(2)]

    def scores(i):
        p, rb, h = stages[i]
        q = q_ref[rb * m:(rb + 1) * m, p * LANES:(p + 1) * LANES] * keep_head[h]
        s_scr[i % 2] = jnp.dot(q, kt_ref[0, p // tiles_per_kv], preferred_element_type=F32)

    def finish(i):
        p, rb, h = stages[i]
        rows, col, slot, kv = slice(rb * m, (rb + 1) * m), p * LANES, i % 2, p // tiles_per_kv
        _softmax_numerators(s_scr.at[slot], e_scr.at[slot], m, GQA_SOFTMAX_ROWS)
        vcol = (2 * kv + h) * LANES
        o = _weighted_values(e_scr[slot], v_ref[:, vcol:vcol + LANES])
        if h == 0:
            y_scr[rows, col:col + LANES] = o
        else:
            y_scr[rows, col:col + LANES] = jnp.where(first_head, y_scr[rows, col:col + LANES], o)

    def gate():
        g_scr[...] = _silu_gate(x_ref, wz_ref, bz_ref)

    _run_stages(len(stages), scores, finish, gate)
    out_ref[...] = (y_scr[...] * g_scr[...]).astype(BF16)


def _gqa(layer, q2, kt, v, x, wb, bb):
    steps = SEQ // GQA_TQ
    row = lambda b, s: (b * steps + s, 0)
    return pl.pallas_call(
        _gqa_body,
        grid=(BATCH, steps),
        in_specs=[
            pl.BlockSpec((GQA_TQ, GQA_WIDTH), row),
            pl.BlockSpec((1, GQA_KV_HEADS, LANES, SEQ), lambda b, s: (b, 0, 0, 0)),
            pl.BlockSpec((SEQ, 2 * GQA_KV_HEADS * LANES), lambda b, s: (b, 0)),
            pl.BlockSpec((GQA_TQ, D_MODEL), row),
            _column_spec(D_MODEL, layer, O_GQZ, O_SGU),
            _column_spec(1, layer, O_GQZ, O_SGU),
        ],
        out_specs=pl.BlockSpec((GQA_TQ, GQA_WIDTH), row),
        out_shape=jax.ShapeDtypeStruct((TOKENS, GQA_WIDTH), BF16),
        scratch_shapes=[pltpu.VMEM((2, GQA_STAGE_ROWS, SEQ), F32), pltpu.VMEM((2, GQA_STAGE_ROWS, SEQ), BF16),
                        pltpu.VMEM((GQA_TQ, GQA_WIDTH), F32), pltpu.VMEM((GQA_TQ, GQA_WIDTH), F32)],
        compiler_params=_cparams(2),
        name="gqa",
    )(q2, kt, v, x, wb, bb)


def _merge_body(x_ref, ya_ref, yb_ref, vn_ref, wu_ref, bu_ref, wz_ref, bz_ref, ws_ref, bs_ref,
                wg_ref, bg_ref, wo_ref, wa_ref, wb_ref, wc_ref, bo_ref, lng_ref, lnb_ref, out_ref, mixed_ref):
    for r0 in range(0, MERGE_TM, MERGE_SUB):
        _merge_rows(slice(r0, r0 + MERGE_SUB), x_ref, ya_ref, yb_ref, vn_ref, wu_ref, bu_ref, wz_ref, bz_ref, ws_ref,
                    bs_ref, wg_ref, bg_ref, wo_ref, wa_ref, wb_ref, wc_ref, bo_ref, lng_ref, lnb_ref, out_ref, mixed_ref)


def _merge_rows(rows, x_ref, ya_ref, yb_ref, vn_ref, wu_ref, bu_ref, wz_ref, bz_ref, ws_ref, bs_ref,
                wg_ref, bg_ref, wo_ref, wa_ref, wb_ref, wc_ref, bo_ref, lng_ref, lnb_ref, out_ref, mixed_ref):
    n_chunks = MERGE_SUB // SG_CHUNK
    x = x_ref[rows, :]
    xb = x.astype(BF16)

    first_head = _lane_is_first_head((SG_CHUNK, LANES))
    chunk_rows = [slice(rows.start + c * SG_CHUNK, rows.start + (c + 1) * SG_CHUNK) for c in range(n_chunks)]
    for pp in range(SG_WIDTH // LANES):
        col = pp * LANES
        rhs = jnp.concatenate([vn_ref[cr, col:col + LANES] for cr in chunk_rows], axis=1)
        res = jnp.dot(ws_ref[pp], rhs, preferred_element_type=F32)
        for c, cr in enumerate(chunk_rows):
            blk = jnp.where(first_head, res[:SG_CHUNK, c * LANES:(c + 1) * LANES], res[SG_CHUNK:, c * LANES:(c + 1) * LANES])
            mixed_ref[cr, col:col + LANES] = blk + bs_ref[:, col:col + LANES]
    u = jnp.dot(xb, wu_ref[0], preferred_element_type=F32) + bu_ref[0]
    z = jnp.dot(xb, wz_ref[0], preferred_element_type=F32) + bz_ref[0]
    tc = (u * mixed_ref[rows, :] * (z * jax.nn.sigmoid(z))).astype(BF16)

    def gate(i):
        lo = i * D_MODEL
        return jax.nn.sigmoid(jnp.dot(xb, wg_ref[0, :, lo:lo + D_MODEL], preferred_element_type=F32)
                              + bg_ref[0, :, lo:lo + D_MODEL])

    merged = gate(0) * jnp.dot(ya_ref[rows, :], wa_ref[...], preferred_element_type=F32)
    merged = merged + gate(1) * jnp.dot(yb_ref[rows, :], wb_ref[...], preferred_element_type=F32)
    merged = merged + gate(2) * jnp.dot(tc, wc_ref[...], preferred_element_type=F32)
    sub = jnp.dot(merged.astype(BF16), wo_ref[...], preferred_element_type=F32) + bo_ref[...]
    out_ref[rows, :] = _layer_norm(DEEPNORM_ALPHA * x + sub, lng_ref[...], lnb_ref[...])


def _merge(layer, x, ya, yb, vn, wb, bb, ws, bs, wo, wa, wbb, wc, bo, lng, lnb):
    tm = MERGE_TM
    row = lambda i: (i, 0)
    vec = lambda width: _fixed_spec((None, 1, width), (layer, 0, 0))
    return pl.pallas_call(
        _merge_body,
        grid=(TOKENS // tm,),
        in_specs=[
            pl.BlockSpec((tm, D_MODEL), row),
            pl.BlockSpec((tm, NA_WIDTH), row),
            pl.BlockSpec((tm, GQA_WIDTH), row),
            pl.BlockSpec((tm, SG_WIDTH), row),
            _column_spec(D_MODEL, layer, O_SGU, O_SGV), _column_spec(1, layer, O_SGU, O_SGV),
            _column_spec(D_MODEL, layer, O_SGZ, O_GATE), _column_spec(1, layer, O_SGZ, O_GATE),
            _fixed_spec((None, SG_GROUPS // 2, 2 * SG_CHUNK, SG_CHUNK), (layer, 0, 0, 0)),
            _fixed_spec((None, SG_CHUNK, SG_WIDTH), (layer, 0, 0)),
            _column_spec(D_MODEL, layer, O_GATE, O_END), _column_spec(1, layer, O_GATE, O_END),
            _fixed_spec((None, D_MODEL, D_MODEL), (layer, 0, 0)),
            _fixed_spec((None, NA_WIDTH, D_MODEL), (layer, 0, 0)),
            _fixed_spec((None, GQA_WIDTH, D_MODEL), (layer, 0, 0)),
            _fixed_spec((None, SG_WIDTH, D_MODEL), (layer, 0, 0)),
            vec(D_MODEL), vec(D_MODEL), vec(D_MODEL),
        ],
        out_specs=pl.BlockSpec((tm, D_MODEL), row),
        out_shape=jax.ShapeDtypeStruct((TOKENS, D_MODEL), F32),
        scratch_shapes=[pltpu.VMEM((tm, SG_WIDTH), F32)],
        compiler_params=_cparams(1),
        name="merge",
    )(x, ya, yb, vn, wb, bb, wb, bb, ws, bs, wb, bb, wo, wa, wbb, wc, bo, lng, lnb)


def _rope_tables():
    t = np.arange(SEQ)
    row = (t // GRID_W).astype(np.float32)
    col = (t % GRID_W).astype(np.float32)
    freqs = np.float32(ROPE_THETA) ** (-np.arange(0, ROPE_AXIS_DIM, 2, dtype=np.float32) / np.float32(ROPE_AXIS_DIM))
    ang = np.concatenate([row[:, None] * freqs, col[:, None] * freqs], axis=-1).astype(np.float32)
    cos = np.cos(ang.astype(np.float64)).astype(np.float32)
    sin = np.sin(ang.astype(np.float64)).astype(np.float32)
    pair = (np.arange(LANES) % HEAD_DIM) // 2
    even = (np.arange(LANES) % 2 == 0)[None, :]
    cos_t = cos[:, pair]
    s1_t = np.where(even, -sin[:, pair], 0.0).astype(np.float32)
    s2_t = np.where(even, 0.0, sin[:, pair]).astype(np.float32)
    return jnp.asarray(cos_t), jnp.asarray(s1_t), jnp.asarray(s2_t)


def kernel(x, ln_in_g, ln_in_b, w_in, b_in, na_rpb, q_norm_g, k_norm_g, sg_ln_g, sg_ln_b, sg_w, sg_b,
           w_br_a, w_br_b, w_br_c, w_out, b_out, ln_post_g, ln_post_b):
    assert x.shape == (BATCH, SEQ, D_MODEL) and w_in.shape == (DEPTH, D_MODEL, O_END)
    cos_t, s1_t, s2_t = _rope_tables()
    bd = jnp.asarray(np.tile(np.kron(np.eye(2), np.ones((HEAD_DIM, HEAD_DIM))), (2, 1)), BF16)
    vec3 = lambda v: v.reshape(DEPTH, 1, -1)

    wb, bb = w_in.astype(BF16), vec3(b_in)
    wo, wa, wbb, wc = (w.astype(BF16) for w in (w_out, w_br_a, w_br_b, w_br_c))
    ws = sg_w.reshape(DEPTH, SG_GROUPS // 2, 2 * SG_CHUNK, SG_CHUNK).astype(BF16)
    bs = jnp.repeat(jnp.swapaxes(sg_b, 1, 2), HEAD_DIM, axis=2)
    gq = vec3(jnp.tile(q_norm_g * SCORE_SCALE, (1, 2)))
    gk = vec3(jnp.tile(k_norm_g, (1, 2)))
    sgg, sgb = vec3(sg_ln_g), vec3(sg_ln_b)
    bo, lng, lnb = vec3(b_out), vec3(ln_post_g), vec3(ln_post_b)
    bias = _na_bias(na_rpb.reshape(-1))

    h = x.reshape(TOKENS, D_MODEL)
    for l in range(DEPTH):
        proj_args = (wb, bb, gq, gk, bd, cos_t, s1_t, s2_t, sgg, sgb)
        if l == 0:
            h, naq, nak, nav, gqq, gqkt, gqv, vn = _proj(l, True, h, ln_in_g.reshape(1, -1), ln_in_b.reshape(1, -1), *proj_args)
        else:
            naq, nak, nav, gqq, gqkt, gqv, vn = _proj(l, False, h, None, None, *proj_args)
        ya = _na(l, naq, nak, nav, bias, h, wb, bb)
        yb = _gqa(l, gqq, gqkt, gqv, h, wb, bb)
        h = _merge(l, h, ya, yb, vn, wb, bb, ws, bs, wo, wa, wbb, wc, bo, lng, lnb)
    return h.reshape(BATCH, SEQ, D_MODEL)
```

```python
import functools

import numpy as np
import jax
import jax.numpy as jnp
from jax import lax
from jax.experimental import pallas as pl
from jax.experimental.pallas import tpu as pltpu

F32 = jnp.float32
BF16 = jnp.bfloat16

D_MODEL = 1024
BATCH = 8
SEQ = 2048
DEPTH = 2
GRID_W = 64
ROWS = SEQ // GRID_W
HEAD_DIM = 64
NA_HEADS = 8
NA_WIDTH = 512
NA_KH = 8
NA_KW = 16
RPB_ROWS = 2 * NA_KH - 1
RPB_COLS = 2 * NA_KW - 1
GQA_HEADS = 8
GQA_KV_HEADS = 2
GQA_WIDTH = 512
GQA_KV_WIDTH = 128
ROPE_THETA = 10000.0
ROPE_AXIS_DIM = HEAD_DIM // 2
SG_WIDTH = 512
SG_GROUPS = 8
SG_CHUNK = 128
N_BRANCH = 3
LN_EPS = 1e-5
RMS_EPS = 1e-6
DEEPNORM_ALPHA = (2.0 * DEPTH) ** 0.25
ATTN_SCALE = HEAD_DIM ** -0.5
LOG2E = 1.4426950408889634
SCORE_SCALE = ATTN_SCALE * LOG2E
MASK_VALUE = -1e30

TOKENS = BATCH * SEQ
LANES = 128
VMEM_LIMIT = 56 * 1024 * 1024

_SPLITS = (NA_WIDTH,) * 4 + (GQA_WIDTH, GQA_KV_WIDTH, GQA_KV_WIDTH, GQA_WIDTH) + (SG_WIDTH,) * 3 + (N_BRANCH * D_MODEL,)
_OFFS = [int(v) for v in np.concatenate([[0], np.cumsum(_SPLITS)])]
(O_NAQ, O_NAK, O_NAV, O_NAZ, O_GQQ, O_GQK, O_GQV, O_GQZ, O_SGU, O_SGV, O_SGZ, O_GATE, O_END) = _OFFS

PROJ_SEGMENTS = ((O_NAQ, O_NAZ), (O_GQQ, O_GQZ), (O_SGV, O_SGZ))
P_NAQ, P_NAK, P_NAV = (0, 0), (0, NA_WIDTH), (0, 2 * NA_WIDTH)
P_GQQ, P_GQK, P_GQV = (1, 0), (1, GQA_WIDTH), (1, GQA_WIDTH + GQA_KV_WIDTH)
P_SGV = (2, 0)

PROJ_TM = 1024
PROJ_SUB = 256
NA_ROWS = 4
NA_TQ = NA_ROWS * GRID_W
NA_WIN_ROWS = 12
NA_WIN = NA_WIN_ROWS * GRID_W
NA_GROUP_COUNT = ROWS // NA_ROWS
NA_GROUPS_PER_STEP = 2
NA_TYPES = 3
NA_SOFTMAX_ROWS = 16
GQA_TQ = 1024
GQA_STAGE_ROWS = 512
GQA_SOFTMAX_ROWS = 16
MERGE_TM = 1024
MERGE_SUB = 256


def _cparams(n_axes):
    return pltpu.CompilerParams(dimension_semantics=("arbitrary",) * n_axes, vmem_limit_bytes=VMEM_LIMIT)


def _fixed_spec(block_shape, index):
    return pl.BlockSpec(block_shape, lambda *_: index, pipeline_mode=pl.Buffered(1))


def _column_spec(rows, layer, start, stop):
    return _fixed_spec((pl.Element(1), pl.Element(rows), pl.Element(stop - start)), (layer, 0, start))


def _layer_norm(x, g, b):
    mu = jnp.mean(x, axis=-1, keepdims=True)
    xc = x - mu
    var = jnp.mean(xc * xc, axis=-1, keepdims=True)
    return xc * lax.rsqrt(var + LN_EPS) * g + b


def _lane_is_first_head(shape):
    return lax.broadcasted_iota(jnp.int32, shape, len(shape) - 1) % LANES < HEAD_DIM


def _head_lane_masks():
    first = jnp.where(_lane_is_first_head((1, LANES)), 1.0, 0.0)
    return first.astype(BF16), (1.0 - first).astype(BF16)


def _na_window_start(step):
    return min(max(NA_ROWS * step - NA_KH // 2, 0), ROWS - NA_WIN_ROWS)


_NA_TYPE_STEPS = (0, 1, NA_GROUP_COUNT - 1)


def _na_tile_plan():
    plan = {}
    for t, step in enumerate(_NA_TYPE_STEPS):
        w0 = _na_window_start(step)
        for dq in range(NA_ROWS):
            r = NA_ROWS * step + dq
            row_start = min(max(r - NA_KH // 2, 0), ROWS - NA_KH)
            for j in range(NA_WIN_ROWS):
                key_row = w0 + j
                inside = row_start <= key_row < row_start + NA_KH
                plan[(t, dq, j)] = key_row - r + (NA_KH - 1) if inside else None
    return plan


def _na_bias_body(rpb_ref, out_ref):
    layer, pair = pl.program_id(0), pl.program_id(1)
    qi = lax.broadcasted_iota(jnp.int32, (GRID_W, GRID_W), 0)
    ki = lax.broadcasted_iota(jnp.int32, (GRID_W, GRID_W), 1)
    lane_entry = jnp.clip(lax.broadcasted_iota(jnp.int32, (8, LANES), 1) - (GRID_W - (NA_KW - 1)), 0, RPB_COLS - 1)
    col_start = jnp.clip(qi - NA_KW // 2, 0, GRID_W - NA_KW)
    col_valid = (ki >= col_start) & (ki < col_start + NA_KW)
    masked = jnp.full((GRID_W, GRID_W), MASK_VALUE, F32)
    plan = _na_tile_plan()
    for hsel in range(2):
        base = ((layer * NA_HEADS + 2 * pair + hsel) * RPB_ROWS) * RPB_COLS
        for ro in range(RPB_ROWS):
            users = [key for key, val in plan.items() if val == ro]
            if not users:
                continue
            pattern = jnp.zeros((8, LANES), F32)
            for c in range(RPB_COLS):
                pattern = jnp.where(lane_entry == c, rpb_ref[base + ro * RPB_COLS + c], pattern)
            rows = jnp.concatenate([pattern] * (GRID_W // 8), axis=0)
            tile = pltpu.roll(rows, GRID_W, 1, stride=1, stride_axis=0)[:, :GRID_W]
            tile = jnp.where(col_valid, tile * LOG2E, MASK_VALUE)
            for (t, dq, j) in users:
                row0 = hsel * NA_TQ + dq * GRID_W
                out_ref[0, t, 0, row0:row0 + GRID_W, j * GRID_W:(j + 1) * GRID_W] = tile
        for (t, dq, j), val in plan.items():
            if val is None:
                row0 = hsel * NA_TQ + dq * GRID_W
                out_ref[0, t, 0, row0:row0 + GRID_W, j * GRID_W:(j + 1) * GRID_W] = masked


def _na_bias(rpb_flat):
    return pl.pallas_call(
        _na_bias_body,
        grid=(DEPTH, NA_HEADS // 2),
        in_specs=[pl.BlockSpec(memory_space=pltpu.SMEM)],
        out_specs=pl.BlockSpec((1, NA_TYPES, 1, 2 * NA_TQ, NA_WIN), lambda l, p: (l, 0, p, 0, 0)),
        out_shape=jax.ShapeDtypeStruct((DEPTH, NA_TYPES, NA_HEADS // 2, 2 * NA_TQ, NA_WIN), F32),
        compiler_params=_cparams(2),
        name="na_bias",
    )(rpb_flat)


def _proj_body(first, *refs):
    if first:
        (x_ref, lng_ref, lnb_ref, w0_ref, w1_ref, w2_ref, b0_ref, b1_ref, b2_ref, gq_ref, gk_ref, bd_ref,
         cos_ref, s1_ref, s2_ref, sgg_ref, sgb_ref,
         h_ref, naq_ref, nak_ref, nav_ref, gqq_ref, gqkt_ref, gqv_ref, vn_ref) = refs
    else:
        (x_ref, w0_ref, w1_ref, w2_ref, b0_ref, b1_ref, b2_ref, gq_ref, gk_ref, bd_ref,
         cos_ref, s1_ref, s2_ref, sgg_ref, sgb_ref,
         naq_ref, nak_ref, nav_ref, gqq_ref, gqkt_ref, gqv_ref, vn_ref) = refs
    w_refs = tuple(r.at[0] for r in (w0_ref, w1_ref, w2_ref))
    b_refs = tuple(r.at[0] for r in (b0_ref, b1_ref, b2_ref))
    for r0 in range(0, PROJ_TM, PROJ_SUB):
        _proj_rows(first, slice(r0, r0 + PROJ_SUB), x_ref, lng_ref if first else None, lnb_ref if first else None,
                   w_refs, b_refs, gq_ref, gk_ref, bd_ref, cos_ref, s1_ref, s2_ref, sgg_ref, sgb_ref,
                   h_ref if first else None, naq_ref, nak_ref, nav_ref, gqq_ref, gqkt_ref, gqv_ref, vn_ref)


def _proj_rows(first, rows, x_ref, lng_ref, lnb_ref, w_refs, b_refs, gq_ref, gk_ref, bd_ref, cos_ref, s1_ref, s2_ref,
               sgg_ref, sgb_ref, h_ref, naq_ref, nak_ref, nav_ref, gqq_ref, gqkt_ref, gqv_ref, vn_ref):
    x = x_ref[rows, :]
    if first:
        x = _layer_norm(x, lng_ref[...], lnb_ref[...])
        h_ref[rows, :] = x
    xb = x.astype(BF16)

    def proj(where, width):
        seg, col = where
        return (jnp.dot(xb, w_refs[seg][:, col:col + width], preferred_element_type=F32)
                + b_refs[seg][:, col:col + width])

    def store_with_ones(dst_ref, val, col, second_copy):
        first_head = _lane_is_first_head(val.shape)
        dst_ref[rows, col:col + LANES] = jnp.where(first_head, val, 1.0).astype(BF16)
        dst_ref[rows, second_copy + col:second_copy + col + LANES] = jnp.where(first_head, 1.0, val).astype(BF16)

    def rms_rope(val, gain):
        sq = val * val
        hi = sq.astype(BF16)
        lo = (sq - hi.astype(F32)).astype(BF16)
        ssum = jnp.dot(jnp.concatenate([hi, lo], axis=1), bd_ref[...], preferred_element_type=F32)
        xn = val * lax.rsqrt(ssum * (1.0 / HEAD_DIM) + RMS_EPS) * gain
        return (xn * cos_ref[rows, :] + pltpu.roll(xn, LANES - 1, 1) * s1_ref[rows, :]
                + pltpu.roll(xn, 1, 1) * s2_ref[rows, :])

    naq_ref[rows, :] = (proj(P_NAQ, NA_WIDTH) * SCORE_SCALE).astype(BF16)
    nak_ref[rows, :] = proj(P_NAK, NA_WIDTH).astype(BF16)
    nav_ref[rows, :] = proj(P_NAV, NA_WIDTH).astype(BF16)

    q = proj(P_GQQ, GQA_WIDTH)
    for p in range(GQA_WIDTH // LANES):
        gqq_ref[rows, p * LANES:(p + 1) * LANES] = rms_rope(q[:, p * LANES:(p + 1) * LANES], gq_ref[...]).astype(BF16)
    k = rms_rope(proj(P_GQK, GQA_KV_WIDTH), gk_ref[...])
    v = proj(P_GQV, GQA_KV_WIDTH)
    first_head = _lane_is_first_head(k.shape)
    k_sw = pltpu.roll(k, HEAD_DIM, 1)
    v_sw = pltpu.roll(v, HEAD_DIM, 1)
    gqkt_ref[0, 0, :, rows] = jnp.where(first_head, k, k_sw).T.astype(BF16)
    gqkt_ref[0, 1, :, rows] = jnp.where(first_head, k_sw, k).T.astype(BF16)
    store_with_ones(gqv_ref, jnp.where(first_head, v, v_sw), 0, LANES)
    store_with_ones(gqv_ref, jnp.where(first_head, v_sw, v), 2 * LANES, LANES)

    vn_ref[rows, :] = _layer_norm(proj(P_SGV, SG_WIDTH), sgg_ref[...], sgb_ref[...]).astype(BF16)


def _proj(layer, first, x, ln_g, ln_b, wb, bb, gq, gk, bd, cos_t, s1_t, s2_t, sgg, sgb):
    tm = PROJ_TM
    tiles_per_seq = SEQ // tm
    row = lambda i: (i, 0)
    pos = lambda i: (i % tiles_per_seq, 0)
    in_specs = [pl.BlockSpec((tm, D_MODEL), row)]
    args = [x]
    if first:
        in_specs += [_fixed_spec((1, D_MODEL), (0, 0)), _fixed_spec((1, D_MODEL), (0, 0))]
        args += [ln_g, ln_b]
    in_specs += [_column_spec(D_MODEL, layer, a, b) for a, b in PROJ_SEGMENTS]
    in_specs += [_column_spec(1, layer, a, b) for a, b in PROJ_SEGMENTS]
    in_specs += [_fixed_spec((None, 1, LANES), (layer, 0, 0)), _fixed_spec((None, 1, LANES), (layer, 0, 0)),
                 _fixed_spec((2 * LANES, LANES), (0, 0)),
                 pl.BlockSpec((tm, LANES), pos), pl.BlockSpec((tm, LANES), pos), pl.BlockSpec((tm, LANES), pos),
                 _fixed_spec((None, 1, SG_WIDTH), (layer, 0, 0)), _fixed_spec((None, 1, SG_WIDTH), (layer, 0, 0))]
    args += [wb] * len(PROJ_SEGMENTS) + [bb] * len(PROJ_SEGMENTS) + [gq, gk, bd, cos_t, s1_t, s2_t, sgg, sgb]
    out_shape, out_specs = [], []
    if first:
        out_shape.append(jax.ShapeDtypeStruct((TOKENS, D_MODEL), F32))
        out_specs.append(pl.BlockSpec((tm, D_MODEL), row))
    out_shape += [
        jax.ShapeDtypeStruct((TOKENS, NA_WIDTH), BF16),
        jax.ShapeDtypeStruct((TOKENS, NA_WIDTH), BF16),
        jax.ShapeDtypeStruct((TOKENS, NA_WIDTH), BF16),
        jax.ShapeDtypeStruct((TOKENS, GQA_WIDTH), BF16),
        jax.ShapeDtypeStruct((BATCH, GQA_KV_HEADS, LANES, SEQ), BF16),
        jax.ShapeDtypeStruct((TOKENS, 2 * GQA_KV_HEADS * LANES), BF16),
        jax.ShapeDtypeStruct((TOKENS, SG_WIDTH), BF16),
    ]
    out_specs += [
        pl.BlockSpec((tm, NA_WIDTH), row),
        pl.BlockSpec((tm, NA_WIDTH), row),
        pl.BlockSpec((tm, NA_WIDTH), row),
        pl.BlockSpec((tm, GQA_WIDTH), row),
        pl.BlockSpec((1, GQA_KV_HEADS, LANES, tm), lambda i: (i // tiles_per_seq, 0, 0, i % tiles_per_seq)),
        pl.BlockSpec((tm, 2 * GQA_KV_HEADS * LANES), row),
        pl.BlockSpec((tm, SG_WIDTH), row),
    ]
    return pl.pallas_call(
        functools.partial(_proj_body, first),
        grid=(TOKENS // tm,),
        in_specs=in_specs,
        out_specs=out_specs,
        out_shape=out_shape,
        compiler_params=_cparams(1),
        name="proj_first" if first else "proj",
    )(*args)


def _silu_gate(x_ref, wz_ref, bz_ref):
    z = jnp.dot(x_ref[...].astype(BF16), wz_ref[0], preferred_element_type=F32) + bz_ref[0]
    return z * jax.nn.sigmoid(z)


def _softmax_numerators(s_ref, e_ref, n_rows, block_rows, l_ref=None):
    for r in range(0, n_rows, block_rows):
        rows = slice(r, r + block_rows)
        blk = s_ref[rows, :]
        e = jnp.exp2(blk - jnp.max(blk, axis=-1, keepdims=True))
        e_ref[rows, :] = e.astype(BF16)
        if l_ref is not None:
            l_ref[rows, :] = jnp.broadcast_to(jnp.sum(e, axis=-1, keepdims=True), (block_rows, LANES))


def _weighted_values(e, v_ones):
    o = jnp.dot(e, v_ones, preferred_element_type=F32)
    return o / pltpu.roll(o, HEAD_DIM, 1)


def _run_stages(n_stages, scores, finish, before_last):
    scores(0)
    for i in range(n_stages):
        if i + 1 < n_stages:
            scores(i + 1)
        else:
            before_last()
        finish(i)


def _na_body(q_ref, k_ref, v_ref, bias0_ref, bias1_ref, x_ref, wz_ref, bz_ref, out_ref,
             s_scr, e_scr, l_scr, y_scr, g_scr):
    step = pl.program_id(0)
    bias_refs = (bias0_ref, bias1_ref)
    first_head = _lane_is_first_head((NA_TQ, LANES))
    keep_first, keep_second = _head_lane_masks()
    stages = [(g, p) for g in range(NA_GROUPS_PER_STEP) for p in range(NA_WIDTH // LANES)]

    def window(g):
        w0 = jnp.clip(NA_ROWS * (NA_GROUPS_PER_STEP * step + g) - NA_KH // 2, 0, ROWS - NA_WIN_ROWS)
        return pl.ds(pl.multiple_of(w0 * GRID_W, 2 * LANES), NA_WIN)

    def scores(i):
        g, p = stages[i]
        rows, col = slice(g * NA_TQ, (g + 1) * NA_TQ), p * LANES
        q = q_ref[rows, col:col + LANES]
        qs = jnp.concatenate([q * keep_first, q * keep_second], axis=0)
        s = lax.dot_general(qs, k_ref[window(g), col:col + LANES], (((1,), (1,)), ((), ())), preferred_element_type=F32)
        s_scr[i % 2] = s + bias_refs[g][0, 0, p]

    def finish(i):
        g, p = stages[i]
        rows, col, slot = slice(g * NA_TQ, (g + 1) * NA_TQ), p * LANES, i % 2
        _softmax_numerators(s_scr.at[slot], e_scr.at[slot], 2 * NA_TQ, NA_SOFTMAX_ROWS, l_scr.at[slot])
        o = jnp.dot(e_scr[slot], v_ref[window(g), col:col + LANES], preferred_element_type=F32) / l_scr[slot]
        y_scr[rows, col:col + LANES] = jnp.where(first_head, o[:NA_TQ], o[NA_TQ:])

    def gate():
        g_scr[...] = _silu_gate(x_ref, wz_ref, bz_ref)

    _run_stages(len(stages), scores, finish, gate)
    out_ref[...] = (y_scr[...] * g_scr[...]).astype(BF16)


def _na(layer, q2, k, v, bias, x, wb, bb):
    steps = NA_GROUP_COUNT // NA_GROUPS_PER_STEP
    tq = NA_GROUPS_PER_STEP * NA_TQ

    def bias_spec(g):
        def index(s, b):
            group = NA_GROUPS_PER_STEP * s + g
            return (layer, jnp.minimum(group, 1) + jnp.maximum(group - (NA_GROUP_COUNT - 2), 0), 0, 0, 0)
        return pl.BlockSpec((1, 1, NA_HEADS // 2, 2 * NA_TQ, NA_WIN), index)

    row = lambda s, b: (b * steps + s, 0)
    return pl.pallas_call(
        _na_body,
        grid=(steps, BATCH),
        in_specs=[
            pl.BlockSpec((tq, NA_WIDTH), row),
            pl.BlockSpec((SEQ, NA_WIDTH), lambda s, b: (b, 0)),
            pl.BlockSpec((SEQ, NA_WIDTH), lambda s, b: (b, 0)),
            bias_spec(0), bias_spec(1),
            pl.BlockSpec((tq, D_MODEL), row),
            _column_spec(D_MODEL, layer, O_NAZ, O_GQQ),
            _column_spec(1, layer, O_NAZ, O_GQQ),
        ],
        out_specs=pl.BlockSpec((tq, NA_WIDTH), row),
        out_shape=jax.ShapeDtypeStruct((TOKENS, NA_WIDTH), BF16),
        scratch_shapes=[pltpu.VMEM((2, 2 * NA_TQ, NA_WIN), F32), pltpu.VMEM((2, 2 * NA_TQ, NA_WIN), BF16),
                        pltpu.VMEM((2, 2 * NA_TQ, LANES), F32), pltpu.VMEM((tq, NA_WIDTH), F32),
                        pltpu.VMEM((tq, NA_WIDTH), F32)],
        compiler_params=_cparams(2),
        name="na",
    )(q2, k, v, bias, bias, x, wb, bb)


def _gqa_body(q_ref, kt_ref, v_ref, x_ref, wz_ref, bz_ref, out_ref, s_scr, e_scr, y_scr, g_scr):
    m = GQA_STAGE_ROWS
    tiles_per_kv = GQA_WIDTH // LANES // GQA_KV_HEADS
    first_head = _lane_is_first_head((m, LANES))
    keep_head = _head_lane_masks()
    stages = [(p, rb, h) for p in range(GQA_WIDTH // LANES) for rb in range(GQA_TQ // m) for h in range(2)]

    def scores(i):
        p, rb, h = stages[i]
        q = q_ref[rb * m:(rb + 1) * m, p * LANES:(p + 1) * LANES] * keep_head[h]
        s_scr[i % 2] = jnp.dot(q, kt_ref[0, p // tiles_per_kv], preferred_element_type=F32)

    def finish(i):
        p, rb, h = stages[i]
        rows, col, slot, kv = slice(rb * m, (rb + 1) * m), p * LANES, i % 2, p // tiles_per_kv
        _softmax_numerators(s_scr.at[slot], e_scr.at[slot], m, GQA_SOFTMAX_ROWS)
        vcol = (2 * kv + h) * LANES
        o = _weighted_values(e_scr[slot], v_ref[:, vcol:vcol + LANES])
        if h == 0:
            y_scr[rows, col:col + LANES] = o
        else:
            y_scr[rows, col:col + LANES] = jnp.where(first_head, y_scr[rows, col:col + LANES], o)

    def gate():
        g_scr[...] = _silu_gate(x_ref, wz_ref, bz_ref)

    _run_stages(len(stages), scores, finish, gate)
    out_ref[...] = (y_scr[...] * g_scr[...]).astype(BF16)


def _gqa(layer, q2, kt, v, x, wb, bb):
    steps = SEQ // GQA_TQ
    row = lambda b, s: (b * steps + s, 0)
    return pl.pallas_call(
        _gqa_body,
        grid=(BATCH, steps),
        in_specs=[
            pl.BlockSpec((GQA_TQ, GQA_WIDTH), row),
            pl.BlockSpec((1, GQA_KV_HEADS, LANES, SEQ), lambda b, s: (b, 0, 0, 0)),
            pl.BlockSpec((SEQ, 2 * GQA_KV_HEADS * LANES), lambda b, s: (b, 0)),
            pl.BlockSpec((GQA_TQ, D_MODEL), row),
            _column_spec(D_MODEL, layer, O_GQZ, O_SGU),
            _column_spec(1, layer, O_GQZ, O_SGU),
        ],
        out_specs=pl.BlockSpec((GQA_TQ, GQA_WIDTH), row),
        out_shape=jax.ShapeDtypeStruct((TOKENS, GQA_WIDTH), BF16),
        scratch_shapes=[pltpu.VMEM((2, GQA_STAGE_ROWS, SEQ), F32), pltpu.VMEM((2, GQA_STAGE_ROWS, SEQ), BF16),
                        pltpu.VMEM((GQA_TQ, GQA_WIDTH), F32), pltpu.VMEM((GQA_TQ, GQA_WIDTH), F32)],
        compiler_params=_cparams(2),
        name="gqa",
    )(q2, kt, v, x, wb, bb)


def _merge_body(x_ref, ya_ref, yb_ref, vn_ref, wu_ref, bu_ref, wz_ref, bz_ref, ws_ref, bs_ref,
                wg_ref, bg_ref, wo_ref, wa_ref, wb_ref, wc_ref, bo_ref, lng_ref, lnb_ref, out_ref, mixed_ref):
    for r0 in range(0, MERGE_TM, MERGE_SUB):
        _merge_rows(slice(r0, r0 + MERGE_SUB), x_ref, ya_ref, yb_ref, vn_ref, wu_ref, bu_ref, wz_ref, bz_ref, ws_ref,
                    bs_ref, wg_ref, bg_ref, wo_ref, wa_ref, wb_ref, wc_ref, bo_ref, lng_ref, lnb_ref, out_ref, mixed_ref)


def _merge_rows(rows, x_ref, ya_ref, yb_ref, vn_ref, wu_ref, bu_ref, wz_ref, bz_ref, ws_ref, bs_ref,
                wg_ref, bg_ref, wo_ref, wa_ref, wb_ref, wc_ref, bo_ref, lng_ref, lnb_ref, out_ref, mixed_ref):
    n_chunks = MERGE_SUB // SG_CHUNK
    x = x_ref[rows, :]
    xb = x.astype(BF16)

    first_head = _lane_is_first_head((SG_CHUNK, LANES))
    chunk_rows = [slice(rows.start + c * SG_CHUNK, rows.start + (c + 1) * SG_CHUNK) for c in range(n_chunks)]
    for pp in range(SG_WIDTH // LANES):
        col = pp * LANES
        rhs = jnp.concatenate([vn_ref[cr, col:col + LANES] for cr in chunk_rows], axis=1)
        res = jnp.dot(ws_ref[pp], rhs, preferred_element_type=F32)
        for c, cr in enumerate(chunk_rows):
            blk = jnp.where(first_head, res[:SG_CHUNK, c * LANES:(c + 1) * LANES], res[SG_CHUNK:, c * LANES:(c + 1) * LANES])
            mixed_ref[cr, col:col + LANES] = blk + bs_ref[:, col:col + LANES]
    u = jnp.dot(xb, wu_ref[0], preferred_element_type=F32) + bu_ref[0]
    z = jnp.dot(xb, wz_ref[0], preferred_element_type=F32) + bz_ref[0]
    tc = (u * mixed_ref[rows, :] * (z * jax.nn.sigmoid(z))).astype(BF16)

    def gate(i):
        lo = i * D_MODEL
        return jax.nn.sigmoid(jnp.dot(xb, wg_ref[0, :, lo:lo + D_MODEL], preferred_element_type=F32)
                              + bg_ref[0, :, lo:lo + D_MODEL])

    merged = gate(0) * jnp.dot(ya_ref[rows, :], wa_ref[...], preferred_element_type=F32)
    merged = merged + gate(1) * jnp.dot(yb_ref[rows, :], wb_ref[...], preferred_element_type=F32)
    merged = merged + gate(2) * jnp.dot(tc, wc_ref[...], preferred_element_type=F32)
    sub = jnp.dot(merged.astype(BF16), wo_ref[...], preferred_element_type=F32) + bo_ref[...]
    out_ref[rows, :] = _layer_norm(DEEPNORM_ALPHA * x + sub, lng_ref[...], lnb_ref[...])


def _merge(layer, x, ya, yb, vn, wb, bb, ws, bs, wo, wa, wbb, wc, bo, lng, lnb):
    tm = MERGE_TM
    row = lambda i: (i, 0)
    vec = lambda width: _fixed_spec((None, 1, width), (layer, 0, 0))
    return pl.pallas_call(
        _merge_body,
        grid=(TOKENS // tm,),
        in_specs=[
            pl.BlockSpec((tm, D_MODEL), row),
            pl.BlockSpec((tm, NA_WIDTH), row),
            pl.BlockSpec((tm, GQA_WIDTH), row),
            pl.BlockSpec((tm, SG_WIDTH), row),
            _column_spec(D_MODEL, layer, O_SGU, O_SGV), _column_spec(1, layer, O_SGU, O_SGV),
            _column_spec(D_MODEL, layer, O_SGZ, O_GATE), _column_spec(1, layer, O_SGZ, O_GATE),
            _fixed_spec((None, SG_GROUPS // 2, 2 * SG_CHUNK, SG_CHUNK), (layer, 0, 0, 0)),
            _fixed_spec((None, SG_CHUNK, SG_WIDTH), (layer, 0, 0)),
            _column_spec(D_MODEL, layer, O_GATE, O_END), _column_spec(1, layer, O_GATE, O_END),
            _fixed_spec((None, D_MODEL, D_MODEL), (layer, 0, 0)),
            _fixed_spec((None, NA_WIDTH, D_MODEL), (layer, 0, 0)),
            _fixed_spec((None, GQA_WIDTH, D_MODEL), (layer, 0, 0)),
            _fixed_spec((None, SG_WIDTH, D_MODEL), (layer, 0, 0)),
            vec(D_MODEL), vec(D_MODEL), vec(D_MODEL),
        ],
        out_specs=pl.BlockSpec((tm, D_MODEL), row),
        out_shape=jax.ShapeDtypeStruct((TOKENS, D_MODEL), F32),
        scratch_shapes=[pltpu.VMEM((tm, SG_WIDTH), F32)],
        compiler_params=_cparams(1),
        name="merge",
    )(x, ya, yb, vn, wb, bb, wb, bb, ws, bs, wb, bb, wo, wa, wbb, wc, bo, lng, lnb)


def _rope_tables():
    t = np.arange(SEQ)
    row = (t // GRID_W).astype(np.float32)
    col = (t % GRID_W).astype(np.float32)
    freqs = np.float32(ROPE_THETA) ** (-np.arange(0, ROPE_AXIS_DIM, 2, dtype=np.float32) / np.float32(ROPE_AXIS_DIM))
    ang = np.concatenate([row[:, None] * freqs, col[:, None] * freqs], axis=-1).astype(np.float32)
    cos = np.cos(ang.astype(np.float64)).astype(np.float32)
    sin = np.sin(ang.astype(np.float64)).astype(np.float32)
    pair = (np.arange(LANES) % HEAD_DIM) // 2
    even = (np.arange(LANES) % 2 == 0)[None, :]
    cos_t = cos[:, pair]
    s1_t = np.where(even, -sin[:, pair], 0.0).astype(np.float32)
    s2_t = np.where(even, 0.0, sin[:, pair]).astype(np.float32)
    return jnp.asarray(cos_t), jnp.asarray(s1_t), jnp.asarray(s2_t)


def kernel(x, ln_in_g, ln_in_b, w_in, b_in, na_rpb, q_norm_g, k_norm_g, sg_ln_g, sg_ln_b, sg_w, sg_b,
           w_br_a, w_br_b, w_br_c, w_out, b_out, ln_post_g, ln_post_b):
    assert x.shape == (BATCH, SEQ, D_MODEL) and w_in.shape == (DEPTH, D_MODEL, O_END)
    cos_t, s1_t, s2_t = _rope_tables()
    bd = jnp.asarray(np.tile(np.kron(np.eye(2), np.ones((HEAD_DIM, HEAD_DIM))), (2, 1)), BF16)
    vec3 = lambda v: v.reshape(DEPTH, 1, -1)

    wb, bb = w_in.astype(BF16), vec3(b_in)
    wo, wa, wbb, wc = (w.astype(BF16) for w in (w_out, w_br_a, w_br_b, w_br_c))
    ws = sg_w.reshape(DEPTH, SG_GROUPS // 2, 2 * SG_CHUNK, SG_CHUNK).astype(BF16)
    bs = jnp.repeat(jnp.swapaxes(sg_b, 1, 2), HEAD_DIM, axis=2)
    gq = vec3(jnp.tile(q_norm_g * SCORE_SCALE, (1, 2)))
    gk = vec3(jnp.tile(k_norm_g, (1, 2)))
    sgg, sgb = vec3(sg_ln_g), vec3(sg_ln_b)
    bo, lng, lnb = vec3(b_out), vec3(ln_post_g), vec3(ln_post_b)
    bias = _na_bias(na_rpb.reshape(-1))

    h = x.reshape(TOKENS, D_MODEL)
    for l in range(DEPTH):
        proj_args = (wb, bb, gq, gk, bd, cos_t, s1_t, s2_t, sgg, sgb)
        if l == 0:
            h, naq, nak, nav, gqq, gqkt, gqv, vn = _proj(l, True, h, ln_in_g.reshape(1, -1), ln_in_b.reshape(1, -1), *proj_args)
        else:
            naq, nak, nav, gqq, gqkt, gqv, vn = _proj(l, False, h, None, None, *proj_args)
        ya = _na(l, naq, nak, nav, bias, h, wb, bb)
        yb = _gqa(l, gqq, gqkt, gqv, h, wb, bb)
        h = _merge(l, h, ya, yb, vn, wb, bb, ws, bs, wo, wa, wbb, wc, bo, lng, lnb)
    return h.reshape(BATCH, SEQ, D_MODEL)
```

```python
import functools

import numpy as np
import jax
import jax.numpy as jnp
from jax import lax
from jax.experimental import pallas as pl
from jax.experimental.pallas import tpu as pltpu

F32 = jnp.float32
BF16 = jnp.bfloat16

D_MODEL = 1024
BATCH = 8
SEQ = 2048
DEPTH = 2
GRID_W = 64
ROWS = SEQ // GRID_W
HEAD_DIM = 64
NA_HEADS = 8
NA_WIDTH = 512
NA_KH = 8
NA_KW = 16
RPB_ROWS = 2 * NA_KH - 1
RPB_COLS = 2 * NA_KW - 1
GQA_HEADS = 8
GQA_KV_HEADS = 2
GQA_WIDTH = 512
GQA_KV_WIDTH = 128
ROPE_THETA = 10000.0
ROPE_AXIS_DIM = HEAD_DIM // 2
SG_WIDTH = 512
SG_GROUPS = 8
SG_CHUNK = 128
N_BRANCH = 3
LN_EPS = 1e-5
RMS_EPS = 1e-6
DEEPNORM_ALPHA = (2.0 * DEPTH) ** 0.25
ATTN_SCALE = HEAD_DIM ** -0.5
LOG2E = 1.4426950408889634
SCORE_SCALE = ATTN_SCALE * LOG2E
MASK_VALUE = -1e30

TOKENS = BATCH * SEQ
LANES = 128
VMEM_LIMIT = 56 * 1024 * 1024

_SPLITS = (NA_WIDTH,) * 4 + (GQA_WIDTH, GQA_KV_WIDTH, GQA_KV_WIDTH, GQA_WIDTH) + (SG_WIDTH,) * 3 + (N_BRANCH * D_MODEL,)
_OFFS = [int(v) for v in np.concatenate([[0], np.cumsum(_SPLITS)])]
(O_NAQ, O_NAK, O_NAV, O_NAZ, O_GQQ, O_GQK, O_GQV, O_GQZ, O_SGU, O_SGV, O_SGZ, O_GATE, O_END) = _OFFS

PROJ_SEGMENTS = ((O_NAQ, O_NAZ), (O_GQQ, O_GQZ), (O_SGV, O_SGZ))
P_NAQ, P_NAK, P_NAV = (0, 0), (0, NA_WIDTH), (0, 2 * NA_WIDTH)
P_GQQ, P_GQK, P_GQV = (1, 0), (1, GQA_WIDTH), (1, GQA_WIDTH + GQA_KV_WIDTH)
P_SGV = (2, 0)

PROJ_TM = 1024
PROJ_SUB = 256
NA_ROWS = 4
NA_TQ = NA_ROWS * GRID_W
NA_WIN_ROWS = 12
NA_WIN = NA_WIN_ROWS * GRID_W
NA_GROUP_COUNT = ROWS // NA_ROWS
NA_GROUPS_PER_STEP = 2
NA_KV_ROWS = NA_WIN_ROWS + NA_ROWS * (NA_GROUPS_PER_STEP - 1)
NA_TYPES = 3
NA_SOFTMAX_ROWS = 16
GQA_TQ = 1024
GQA_STAGE_ROWS = 512
GQA_SOFTMAX_ROWS = 16
MERGE_TM = 1024
MERGE_SUB = 256


def _cparams(n_axes):
    return pltpu.CompilerParams(dimension_semantics=("arbitrary",) * n_axes, vmem_limit_bytes=VMEM_LIMIT)


def _fixed_spec(block_shape, index):
    return pl.BlockSpec(block_shape, lambda *_: index, pipeline_mode=pl.Buffered(1))


def _column_spec(rows, layer, start, stop):
    return _fixed_spec((pl.Element(1), pl.Element(rows), pl.Element(stop - start)), (layer, 0, start))


def _layer_norm(x, g, b):
    mu = jnp.mean(x, axis=-1, keepdims=True)
    xc = x - mu
    var = jnp.mean(xc * xc, axis=-1, keepdims=True)
    return xc * lax.rsqrt(var + LN_EPS) * g + b


def _lane_is_first_head(shape):
    return lax.broadcasted_iota(jnp.int32, shape, len(shape) - 1) % LANES < HEAD_DIM


def _head_lane_masks():
    first = jnp.where(_lane_is_first_head((1, LANES)), 1.0, 0.0)
    return first.astype(BF16), (1.0 - first).astype(BF16)


def _na_window_start(step):
    return min(max(NA_ROWS * step - NA_KH // 2, 0), ROWS - NA_WIN_ROWS)


_NA_TYPE_STEPS = (0, 1, NA_GROUP_COUNT - 1)


def _na_tile_plan():
    plan = {}
    for t, step in enumerate(_NA_TYPE_STEPS):
        w0 = _na_window_start(step)
        for dq in range(NA_ROWS):
            r = NA_ROWS * step + dq
            row_start = min(max(r - NA_KH // 2, 0), ROWS - NA_KH)
            for j in range(NA_WIN_ROWS):
                key_row = w0 + j
                inside = row_start <= key_row < row_start + NA_KH
                plan[(t, dq, j)] = key_row - r + (NA_KH - 1) if inside else None
    return plan


def _na_bias_body(rpb_ref, out_ref):
    layer, pair = pl.program_id(0), pl.program_id(1)
    qi = lax.broadcasted_iota(jnp.int32, (GRID_W, GRID_W), 0)
    ki = lax.broadcasted_iota(jnp.int32, (GRID_W, GRID_W), 1)
    lane_entry = jnp.clip(lax.broadcasted_iota(jnp.int32, (8, LANES), 1) - (GRID_W - (NA_KW - 1)), 0, RPB_COLS - 1)
    col_start = jnp.clip(qi - NA_KW // 2, 0, GRID_W - NA_KW)
    col_valid = (ki >= col_start) & (ki < col_start + NA_KW)
    masked = jnp.full((GRID_W, GRID_W), MASK_VALUE, F32)
    plan = _na_tile_plan()
    for hsel in range(2):
        base = ((layer * NA_HEADS + 2 * pair + hsel) * RPB_ROWS) * RPB_COLS
        for ro in range(RPB_ROWS):
            users = [key for key, val in plan.items() if val == ro]
            if not users:
                continue
            pattern = jnp.zeros((8, LANES), F32)
            for c in range(RPB_COLS):
                pattern = jnp.where(lane_entry == c, rpb_ref[base + ro * RPB_COLS + c], pattern)
            rows = jnp.concatenate([pattern] * (GRID_W // 8), axis=0)
            tile = pltpu.roll(rows, GRID_W, 1, stride=1, stride_axis=0)[:, :GRID_W]
            tile = jnp.where(col_valid, tile * LOG2E, MASK_VALUE)
            for (t, dq, j) in users:
                row0 = hsel * NA_TQ + dq * GRID_W
                out_ref[0, t, 0, row0:row0 + GRID_W, j * GRID_W:(j + 1) * GRID_W] = tile
        for (t, dq, j), val in plan.items():
            if val is None:
                row0 = hsel * NA_TQ + dq * GRID_W
                out_ref[0, t, 0, row0:row0 + GRID_W, j * GRID_W:(j + 1) * GRID_W] = masked


def _na_bias(rpb_flat):
    return pl.pallas_call(
        _na_bias_body,
        grid=(DEPTH, NA_HEADS // 2),
        in_specs=[pl.BlockSpec(memory_space=pltpu.SMEM)],
        out_specs=pl.BlockSpec((1, NA_TYPES, 1, 2 * NA_TQ, NA_WIN), lambda l, p: (l, 0, p, 0, 0)),
        out_shape=jax.ShapeDtypeStruct((DEPTH, NA_TYPES, NA_HEADS // 2, 2 * NA_TQ, NA_WIN), F32),
        compiler_params=_cparams(2),
        name="na_bias",
    )(rpb_flat)


def _proj_body(first, *refs):
    if first:
        (x_ref, lng_ref, lnb_ref, w0_ref, w1_ref, w2_ref, b0_ref, b1_ref, b2_ref, gq_ref, gk_ref, bd_ref,
         cos_ref, s1_ref, s2_ref, sgg_ref, sgb_ref,
         h_ref, naq_ref, nak_ref, nav_ref, gqq_ref, gqkt_ref, gqv_ref, vn_ref) = refs
    else:
        (x_ref, w0_ref, w1_ref, w2_ref, b0_ref, b1_ref, b2_ref, gq_ref, gk_ref, bd_ref,
         cos_ref, s1_ref, s2_ref, sgg_ref, sgb_ref,
         naq_ref, nak_ref, nav_ref, gqq_ref, gqkt_ref, gqv_ref, vn_ref) = refs
    w_refs = tuple(r.at[0] for r in (w0_ref, w1_ref, w2_ref))
    b_refs = tuple(r.at[0] for r in (b0_ref, b1_ref, b2_ref))
    for r0 in range(0, PROJ_TM, PROJ_SUB):
        _proj_rows(first, slice(r0, r0 + PROJ_SUB), x_ref, lng_ref if first else None, lnb_ref if first else None,
                   w_refs, b_refs, gq_ref, gk_ref, bd_ref, cos_ref, s1_ref, s2_ref, sgg_ref, sgb_ref,
                   h_ref if first else None, naq_ref, nak_ref, nav_ref, gqq_ref, gqkt_ref, gqv_ref, vn_ref)


def _proj_rows(first, rows, x_ref, lng_ref, lnb_ref, w_refs, b_refs, gq_ref, gk_ref, bd_ref, cos_ref, s1_ref, s2_ref,
               sgg_ref, sgb_ref, h_ref, naq_ref, nak_ref, nav_ref, gqq_ref, gqkt_ref, gqv_ref, vn_ref):
    x = x_ref[rows, :]
    if first:
        x = _layer_norm(x, lng_ref[...], lnb_ref[...])
        h_ref[rows, :] = x
    xb = x.astype(BF16)

    def proj(where, width):
        seg, col = where
        return (jnp.dot(xb, w_refs[seg][:, col:col + width], preferred_element_type=F32)
                + b_refs[seg][:, col:col + width])

    def store_with_ones(dst_ref, val, col, second_copy):
        first_head = _lane_is_first_head(val.shape)
        dst_ref[rows, col:col + LANES] = jnp.where(first_head, val, 1.0).astype(BF16)
        dst_ref[rows, second_copy + col:second_copy + col + LANES] = jnp.where(first_head, 1.0, val).astype(BF16)

    def rms_rope(val, gain):
        sq = val * val
        hi = sq.astype(BF16)
        lo = (sq - hi.astype(F32)).astype(BF16)
        ssum = jnp.dot(jnp.concatenate([hi, lo], axis=1), bd_ref[...], preferred_element_type=F32)
        xn = val * lax.rsqrt(ssum * (1.0 / HEAD_DIM) + RMS_EPS) * gain
        return (xn * cos_ref[rows, :] + pltpu.roll(xn, LANES - 1, 1) * s1_ref[rows, :]
                + pltpu.roll(xn, 1, 1) * s2_ref[rows, :])

    naq_ref[rows, :] = (proj(P_NAQ, NA_WIDTH) * SCORE_SCALE).astype(BF16)
    nak_ref[rows, :] = proj(P_NAK, NA_WIDTH).astype(BF16)
    nav_ref[rows, :] = proj(P_NAV, NA_WIDTH).astype(BF16)

    q = proj(P_GQQ, GQA_WIDTH)
    for p in range(GQA_WIDTH // LANES):
        gqq_ref[rows, p * LANES:(p + 1) * LANES] = rms_rope(q[:, p * LANES:(p + 1) * LANES], gq_ref[...]).astype(BF16)
    k = rms_rope(proj(P_GQK, GQA_KV_WIDTH), gk_ref[...])
    v = proj(P_GQV, GQA_KV_WIDTH)
    first_head = _lane_is_first_head(k.shape)
    k_sw = pltpu.roll(k, HEAD_DIM, 1)
    v_sw = pltpu.roll(v, HEAD_DIM, 1)
    gqkt_ref[0, 0, :, rows] = jnp.where(first_head, k, k_sw).T.astype(BF16)
    gqkt_ref[0, 1, :, rows] = jnp.where(first_head, k_sw, k).T.astype(BF16)
    store_with_ones(gqv_ref, jnp.where(first_head, v, v_sw), 0, LANES)
    store_with_ones(gqv_ref, jnp.where(first_head, v_sw, v), 2 * LANES, LANES)

    vn_ref[rows, :] = _layer_norm(proj(P_SGV, SG_WIDTH), sgg_ref[...], sgb_ref[...]).astype(BF16)


def _proj(layer, first, x, ln_g, ln_b, wb, bb, gq, gk, bd, cos_t, s1_t, s2_t, sgg, sgb):
    tm = PROJ_TM
    tiles_per_seq = SEQ // tm
    row = lambda i: (i, 0)
    pos = lambda i: (i % tiles_per_seq, 0)
    in_specs = [pl.BlockSpec((tm, D_MODEL), row)]
    args = [x]
    if first:
        in_specs += [_fixed_spec((1, D_MODEL), (0, 0)), _fixed_spec((1, D_MODEL), (0, 0))]
        args += [ln_g, ln_b]
    in_specs += [_column_spec(D_MODEL, layer, a, b) for a, b in PROJ_SEGMENTS]
    in_specs += [_column_spec(1, layer, a, b) for a, b in PROJ_SEGMENTS]
    in_specs += [_fixed_spec((None, 1, LANES), (layer, 0, 0)), _fixed_spec((None, 1, LANES), (layer, 0, 0)),
                 _fixed_spec((2 * LANES, LANES), (0, 0)),
                 pl.BlockSpec((tm, LANES), pos), pl.BlockSpec((tm, LANES), pos), pl.BlockSpec((tm, LANES), pos),
                 _fixed_spec((None, 1, SG_WIDTH), (layer, 0, 0)), _fixed_spec((None, 1, SG_WIDTH), (layer, 0, 0))]
    args += [wb] * len(PROJ_SEGMENTS) + [bb] * len(PROJ_SEGMENTS) + [gq, gk, bd, cos_t, s1_t, s2_t, sgg, sgb]
    out_shape, out_specs = [], []
    if first:
        out_shape.append(jax.ShapeDtypeStruct((TOKENS, D_MODEL), F32))
        out_specs.append(pl.BlockSpec((tm, D_MODEL), row))
    out_shape += [
        jax.ShapeDtypeStruct((TOKENS, NA_WIDTH), BF16),
        jax.ShapeDtypeStruct((TOKENS, NA_WIDTH), BF16),
        jax.ShapeDtypeStruct((TOKENS, NA_WIDTH), BF16),
        jax.ShapeDtypeStruct((TOKENS, GQA_WIDTH), BF16),
        jax.ShapeDtypeStruct((BATCH, GQA_KV_HEADS, LANES, SEQ), BF16),
        jax.ShapeDtypeStruct((TOKENS, 2 * GQA_KV_HEADS * LANES), BF16),
        jax.ShapeDtypeStruct((TOKENS, SG_WIDTH), BF16),
    ]
    out_specs += [
        pl.BlockSpec((tm, NA_WIDTH), row),
        pl.BlockSpec((tm, NA_WIDTH), row),
        pl.BlockSpec((tm, NA_WIDTH), row),
        pl.BlockSpec((tm, GQA_WIDTH), row),
        pl.BlockSpec((1, GQA_KV_HEADS, LANES, tm), lambda i: (i // tiles_per_seq, 0, 0, i % tiles_per_seq)),
        pl.BlockSpec((tm, 2 * GQA_KV_HEADS * LANES), row),
        pl.BlockSpec((tm, SG_WIDTH), row),
    ]
    return pl.pallas_call(
        functools.partial(_proj_body, first),
        grid=(TOKENS // tm,),
        in_specs=in_specs,
        out_specs=out_specs,
        out_shape=out_shape,
        compiler_params=_cparams(1),
        name="proj_first" if first else "proj",
    )(*args)


def _silu_gate(x_ref, wz_ref, bz_ref):
    z = jnp.dot(x_ref[...].astype(BF16), wz_ref[0], preferred_element_type=F32) + bz_ref[0]
    return z * jax.nn.sigmoid(z)


def _softmax_numerators(s_ref, e_ref, n_rows, block_rows, l_ref=None):
    for r in range(0, n_rows, block_rows):
        rows = slice(r, r + block_rows)
        blk = s_ref[rows, :]
        e = jnp.exp2(blk - jnp.max(blk, axis=-1, keepdims=True))
        e_ref[rows, :] = e.astype(BF16)
        if l_ref is not None:
            l_ref[rows, :] = jnp.broadcast_to(jnp.sum(e, axis=-1, keepdims=True), (block_rows, LANES))


def _weighted_values(e, v_ones):
    o = jnp.dot(e, v_ones, preferred_element_type=F32)
    return o / pltpu.roll(o, HEAD_DIM, 1)


def _run_stages(n_stages, scores, finish, before_last):
    scores(0)
    for i in range(n_stages):
        if i + 1 < n_stages:
            scores(i + 1)
        else:
            before_last()
        finish(i)


def _na_kv_start(step):
    first = jnp.clip(NA_ROWS * NA_GROUPS_PER_STEP * step - NA_KH // 2, 0, ROWS - NA_WIN_ROWS)
    return jnp.minimum(first, ROWS - NA_KV_ROWS)


def _na_body(q_ref, k_ref, v_ref, bias0_ref, bias1_ref, x_ref, wz_ref, bz_ref, out_ref,
             s_scr, e_scr, l_scr, y_scr, g_scr):
    step = pl.program_id(0)
    bias_refs = (bias0_ref, bias1_ref)
    first_head = _lane_is_first_head((NA_TQ, LANES))
    keep_first, keep_second = _head_lane_masks()
    stages = [(g, p) for g in range(NA_GROUPS_PER_STEP) for p in range(NA_WIDTH // LANES)]

    def window(g):
        w0 = jnp.clip(NA_ROWS * (NA_GROUPS_PER_STEP * step + g) - NA_KH // 2, 0, ROWS - NA_WIN_ROWS)
        return pl.ds(pl.multiple_of((w0 - _na_kv_start(step)) * GRID_W, 2 * LANES), NA_WIN)

    def scores(i):
        g, p = stages[i]
        rows, col = slice(g * NA_TQ, (g + 1) * NA_TQ), p * LANES
        q = q_ref[rows, col:col + LANES]
        qs = jnp.concatenate([q * keep_first, q * keep_second], axis=0)
        s = lax.dot_general(qs, k_ref[window(g), col:col + LANES], (((1,), (1,)), ((), ())), preferred_element_type=F32)
        s_scr[i % 2] = s + bias_refs[g][0, 0, p]

    def finish(i):
        g, p = stages[i]
        rows, col, slot = slice(g * NA_TQ, (g + 1) * NA_TQ), p * LANES, i % 2
        _softmax_numerators(s_scr.at[slot], e_scr.at[slot], 2 * NA_TQ, NA_SOFTMAX_ROWS, l_scr.at[slot])
        o = jnp.dot(e_scr[slot], v_ref[window(g), col:col + LANES], preferred_element_type=F32) / l_scr[slot]
        y_scr[rows, col:col + LANES] = jnp.where(first_head, o[:NA_TQ], o[NA_TQ:])

    def gate():
        g_scr[...] = _silu_gate(x_ref, wz_ref, bz_ref)

    _run_stages(len(stages), scores, finish, gate)
    out_ref[...] = (y_scr[...] * g_scr[...]).astype(BF16)


def _na(layer, q2, k, v, bias, x, wb, bb):
    steps = NA_GROUP_COUNT // NA_GROUPS_PER_STEP
    tq = NA_GROUPS_PER_STEP * NA_TQ

    def bias_spec(g):
        def index(s, b):
            group = NA_GROUPS_PER_STEP * s + g
            return (layer, jnp.minimum(group, 1) + jnp.maximum(group - (NA_GROUP_COUNT - 2), 0), 0, 0, 0)
        return pl.BlockSpec((1, 1, NA_HEADS // 2, 2 * NA_TQ, NA_WIN), index)

    row = lambda s, b: (b * steps + s, 0)
    kv_spec = pl.BlockSpec((pl.Element(NA_KV_ROWS * GRID_W), pl.Element(NA_WIDTH)),
                           lambda s, b: (pl.multiple_of(b * SEQ + _na_kv_start(s) * GRID_W, 2 * LANES), 0))
    return pl.pallas_call(
        _na_body,
        grid=(steps, BATCH),
        in_specs=[
            pl.BlockSpec((tq, NA_WIDTH), row),
            kv_spec, kv_spec,
            bias_spec(0), bias_spec(1),
            pl.BlockSpec((tq, D_MODEL), row),
            _column_spec(D_MODEL, layer, O_NAZ, O_GQQ),
            _column_spec(1, layer, O_NAZ, O_GQQ),
        ],
        out_specs=pl.BlockSpec((tq, NA_WIDTH), row),
        out_shape=jax.ShapeDtypeStruct((TOKENS, NA_WIDTH), BF16),
        scratch_shapes=[pltpu.VMEM((2, 2 * NA_TQ, NA_WIN), F32), pltpu.VMEM((2, 2 * NA_TQ, NA_WIN), BF16),
                        pltpu.VMEM((2, 2 * NA_TQ, LANES), F32), pltpu.VMEM((tq, NA_WIDTH), F32),
                        pltpu.VMEM((tq, NA_WIDTH), F32)],
        compiler_params=_cparams(2),
        name="na",
    )(q2, k, v, bias, bias, x, wb, bb)


def _gqa_body(q_ref, kt_ref, v_ref, x_ref, wz_ref, bz_ref, out_ref, s_scr, e_scr, y_scr, g_scr):
    m = GQA_STAGE_ROWS
    tiles_per_kv = GQA_WIDTH // LANES // GQA_KV_HEADS
    first_head = _lane_is_first_head((m, LANES))
    keep_head = _head_lane_masks()
    stages = [(p, rb, h) for p in range(GQA_WIDTH // LANES) for rb in range(GQA_TQ // m) for h in range(2)]

    def scores(i):
        p, rb, h = stages[i]
        q = q_ref[rb * m:(rb + 1) * m, p * LANES:(p + 1) * LANES] * keep_head[h]
        s_scr[i % 2] = jnp.dot(q, kt_ref[0, p // tiles_per_kv], preferred_element_type=F32)

    def finish(i):
        p, rb, h = stages[i]
        rows, col, slot, kv = slice(rb * m, (rb + 1) * m), p * LANES, i % 2, p // tiles_per_kv
        _softmax_numerators(s_scr.at[slot], e_scr.at[slot], m, GQA_SOFTMAX_ROWS)
        vcol = (2 * kv + h) * LANES
        o = _weighted_values(e_scr[slot], v_ref[:, vcol:vcol + LANES])
        if h == 0:
            y_scr[rows, col:col + LANES] = o
        else:
            y_scr[rows, col:col + LANES] = jnp.where(first_head, y_scr[rows, col:col + LANES], o)

    def gate():
        g_scr[...] = _silu_gate(x_ref, wz_ref, bz_ref)

    _run_stages(len(stages), scores, finish, gate)
    out_ref[...] = (y_scr[...] * g_scr[...]).astype(BF16)


def _gqa(layer, q2, kt, v, x, wb, bb):
    steps = SEQ // GQA_TQ
    row = lambda b, s: (b * steps + s, 0)
    return pl.pallas_call(
        _gqa_body,
        grid=(BATCH, steps),
        in_specs=[
            pl.BlockSpec((GQA_TQ, GQA_WIDTH), row),
            pl.BlockSpec((1, GQA_KV_HEADS, LANES, SEQ), lambda b, s: (b, 0, 0, 0)),
            pl.BlockSpec((SEQ, 2 * GQA_KV_HEADS * LANES), lambda b, s: (b, 0)),
            pl.BlockSpec((GQA_TQ, D_MODEL), row),
            _column_spec(D_MODEL, layer, O_GQZ, O_SGU),
            _column_spec(1, layer, O_GQZ, O_SGU),
        ],
        out_specs=pl.BlockSpec((GQA_TQ, GQA_WIDTH), row),
        out_shape=jax.ShapeDtypeStruct((TOKENS, GQA_WIDTH), BF16),
        scratch_shapes=[pltpu.VMEM((2, GQA_STAGE_ROWS, SEQ), F32), pltpu.VMEM((2, GQA_STAGE_ROWS, SEQ), BF16),
                        pltpu.VMEM((GQA_TQ, GQA_WIDTH), F32), pltpu.VMEM((GQA_TQ, GQA_WIDTH), F32)],
        compiler_params=_cparams(2),
        name="gqa",
    )(q2, kt, v, x, wb, bb)


def _merge_body(x_ref, ya_ref, yb_ref, vn_ref, wu_ref, bu_ref, wz_ref, bz_ref, ws_ref, bs_ref,
                wg_ref, bg_ref, wo_ref, wa_ref, wb_ref, wc_ref, bo_ref, lng_ref, lnb_ref, out_ref, mixed_ref):
    for r0 in range(0, MERGE_TM, MERGE_SUB):
        _merge_rows(slice(r0, r0 + MERGE_SUB), x_ref, ya_ref, yb_ref, vn_ref, wu_ref, bu_ref, wz_ref, bz_ref, ws_ref,
                    bs_ref, wg_ref, bg_ref, wo_ref, wa_ref, wb_ref, wc_ref, bo_ref, lng_ref, lnb_ref, out_ref, mixed_ref)


def _merge_rows(rows, x_ref, ya_ref, yb_ref, vn_ref, wu_ref, bu_ref, wz_ref, bz_ref, ws_ref, bs_ref,
                wg_ref, bg_ref, wo_ref, wa_ref, wb_ref, wc_ref, bo_ref, lng_ref, lnb_ref, out_ref, mixed_ref):
    n_chunks = MERGE_SUB // SG_CHUNK
    x = x_ref[rows, :]
    xb = x.astype(BF16)

    first_head = _lane_is_first_head((SG_CHUNK, LANES))
    chunk_rows = [slice(rows.start + c * SG_CHUNK, rows.start + (c + 1) * SG_CHUNK) for c in range(n_chunks)]
    for pp in range(SG_WIDTH // LANES):
        col = pp * LANES
        rhs = jnp.concatenate([vn_ref[cr, col:col + LANES] for cr in chunk_rows], axis=1)
        res = jnp.dot(ws_ref[pp], rhs, preferred_element_type=F32)
        for c, cr in enumerate(chunk_rows):
            blk = jnp.where(first_head, res[:SG_CHUNK, c * LANES:(c + 1) * LANES], res[SG_CHUNK:, c * LANES:(c + 1) * LANES])
            mixed_ref[cr, col:col + LANES] = blk + bs_ref[:, col:col + LANES]
    u = jnp.dot(xb, wu_ref[0], preferred_element_type=F32) + bu_ref[0]
    z = jnp.dot(xb, wz_ref[0], preferred_element_type=F32) + bz_ref[0]
    tc = (u * mixed_ref[rows, :] * (z * jax.nn.sigmoid(z))).astype(BF16)

    def gate(i):
        lo = i * D_MODEL
        return jax.nn.sigmoid(jnp.dot(xb, wg_ref[0, :, lo:lo + D_MODEL], preferred_element_type=F32)
                              + bg_ref[0, :, lo:lo + D_MODEL])

    merged = gate(0) * jnp.dot(ya_ref[rows, :], wa_ref[...], preferred_element_type=F32)
    merged = merged + gate(1) * jnp.dot(yb_ref[rows, :], wb_ref[...], preferred_element_type=F32)
    merged = merged + gate(2) * jnp.dot(tc, wc_ref[...], preferred_element_type=F32)
    sub = jnp.dot(merged.astype(BF16), wo_ref[...], preferred_element_type=F32) + bo_ref[...]
    out_ref[rows, :] = _layer_norm(DEEPNORM_ALPHA * x + sub, lng_ref[...], lnb_ref[...])


def _merge(layer, x, ya, yb, vn, wb, bb, ws, bs, wo, wa, wbb, wc, bo, lng, lnb):
    tm = MERGE_TM
    row = lambda i: (i, 0)
    vec = lambda width: _fixed_spec((None, 1, width), (layer, 0, 0))
    return pl.pallas_call(
        _merge_body,
        grid=(TOKENS // tm,),
        in_specs=[
            pl.BlockSpec((tm, D_MODEL), row),
            pl.BlockSpec((tm, NA_WIDTH), row),
            pl.BlockSpec((tm, GQA_WIDTH), row),
            pl.BlockSpec((tm, SG_WIDTH), row),
            _column_spec(D_MODEL, layer, O_SGU, O_SGV), _column_spec(1, layer, O_SGU, O_SGV),
            _column_spec(D_MODEL, layer, O_SGZ, O_GATE), _column_spec(1, layer, O_SGZ, O_GATE),
            _fixed_spec((None, SG_GROUPS // 2, 2 * SG_CHUNK, SG_CHUNK), (layer, 0, 0, 0)),
            _fixed_spec((None, SG_CHUNK, SG_WIDTH), (layer, 0, 0)),
            _column_spec(D_MODEL, layer, O_GATE, O_END), _column_spec(1, layer, O_GATE, O_END),
            _fixed_spec((None, D_MODEL, D_MODEL), (layer, 0, 0)),
            _fixed_spec((None, NA_WIDTH, D_MODEL), (layer, 0, 0)),
            _fixed_spec((None, GQA_WIDTH, D_MODEL), (layer, 0, 0)),
            _fixed_spec((None, SG_WIDTH, D_MODEL), (layer, 0, 0)),
            vec(D_MODEL), vec(D_MODEL), vec(D_MODEL),
        ],
        out_specs=pl.BlockSpec((tm, D_MODEL), row),
        out_shape=jax.ShapeDtypeStruct((TOKENS, D_MODEL), F32),
        scratch_shapes=[pltpu.VMEM((tm, SG_WIDTH), F32)],
        compiler_params=_cparams(1),
        name="merge",
    )(x, ya, yb, vn, wb, bb, wb, bb, ws, bs, wb, bb, wo, wa, wbb, wc, bo, lng, lnb)


def _rope_tables():
    t = np.arange(SEQ)
    row = (t // GRID_W).astype(np.float32)
    col = (t % GRID_W).astype(np.float32)
    freqs = np.float32(ROPE_THETA) ** (-np.arange(0, ROPE_AXIS_DIM, 2, dtype=np.float32) / np.float32(ROPE_AXIS_DIM))
    ang = np.concatenate([row[:, None] * freqs, col[:, None] * freqs], axis=-1).astype(np.float32)
    cos = np.cos(ang.astype(np.float64)).astype(np.float32)
    sin = np.sin(ang.astype(np.float64)).astype(np.float32)
    pair = (np.arange(LANES) % HEAD_DIM) // 2
    even = (np.arange(LANES) % 2 == 0)[None, :]
    cos_t = cos[:, pair]
    s1_t = np.where(even, -sin[:, pair], 0.0).astype(np.float32)
    s2_t = np.where(even, 0.0, sin[:, pair]).astype(np.float32)
    return jnp.asarray(cos_t), jnp.asarray(s1_t), jnp.asarray(s2_t)


def kernel(x, ln_in_g, ln_in_b, w_in, b_in, na_rpb, q_norm_g, k_norm_g, sg_ln_g, sg_ln_b, sg_w, sg_b,
           w_br_a, w_br_b, w_br_c, w_out, b_out, ln_post_g, ln_post_b):
    assert x.shape == (BATCH, SEQ, D_MODEL) and w_in.shape == (DEPTH, D_MODEL, O_END)
    cos_t, s1_t, s2_t = _rope_tables()
    bd = jnp.asarray(np.tile(np.kron(np.eye(2), np.ones((HEAD_DIM, HEAD_DIM))), (2, 1)), BF16)
    vec3 = lambda v: v.reshape(DEPTH, 1, -1)

    wb, bb = w_in.astype(BF16), vec3(b_in)
    wo, wa, wbb, wc = (w.astype(BF16) for w in (w_out, w_br_a, w_br_b, w_br_c))
    ws = sg_w.reshape(DEPTH, SG_GROUPS // 2, 2 * SG_CHUNK, SG_CHUNK).astype(BF16)
    bs = jnp.repeat(jnp.swapaxes(sg_b, 1, 2), HEAD_DIM, axis=2)
    gq = vec3(jnp.tile(q_norm_g * SCORE_SCALE, (1, 2)))
    gk = vec3(jnp.tile(k_norm_g, (1, 2)))
    sgg, sgb = vec3(sg_ln_g), vec3(sg_ln_b)
    bo, lng, lnb = vec3(b_out), vec3(ln_post_g), vec3(ln_post_b)
    bias = _na_bias(na_rpb.reshape(-1))

    h = x.reshape(TOKENS, D_MODEL)
    for l in range(DEPTH):
        proj_args = (wb, bb, gq, gk, bd, cos_t, s1_t, s2_t, sgg, sgb)
        if l == 0:
            h, naq, nak, nav, gqq, gqkt, gqv, vn = _proj(l, True, h, ln_in_g.reshape(1, -1), ln_in_b.reshape(1, -1), *proj_args)
        else:
            naq, nak, nav, gqq, gqkt, gqv, vn = _proj(l, False, h, None, None, *proj_args)
        ya = _na(l, naq, nak, nav, bias, h, wb, bb)
        yb = _gqa(l, gqq, gqkt, gqv, h, wb, bb)
        h = _merge(l, h, ya, yb, vn, wb, bb, ws, bs, wo, wa, wbb, wc, bo, lng, lnb)
    return h.reshape(BATCH, SEQ, D_MODEL)
```

```python
import functools

import numpy as np
import jax
import jax.numpy as jnp
from jax import lax
from jax.experimental import pallas as pl
from jax.experimental.pallas import tpu as pltpu

F32 = jnp.float32
BF16 = jnp.bfloat16

D_MODEL = 1024
BATCH = 8
SEQ = 2048
DEPTH = 2
GRID_W = 64
ROWS = SEQ // GRID_W
HEAD_DIM = 64
NA_HEADS = 8
NA_WIDTH = 512
NA_KH = 8
NA_KW = 16
RPB_ROWS = 2 * NA_KH - 1
RPB_COLS = 2 * NA_KW - 1
GQA_HEADS = 8
GQA_KV_HEADS = 2
GQA_WIDTH = 512
GQA_KV_WIDTH = 128
ROPE_THETA = 10000.0
ROPE_AXIS_DIM = HEAD_DIM // 2
SG_WIDTH = 512
SG_GROUPS = 8
SG_CHUNK = 128
N_BRANCH = 3
LN_EPS = 1e-5
RMS_EPS = 1e-6
DEEPNORM_ALPHA = (2.0 * DEPTH) ** 0.25
ATTN_SCALE = HEAD_DIM ** -0.5
LOG2E = 1.4426950408889634
SCORE_SCALE = ATTN_SCALE * LOG2E
MASK_VALUE = -1e30

TOKENS = BATCH * SEQ
LANES = 128
VMEM_LIMIT = 56 * 1024 * 1024

_SPLITS = (NA_WIDTH,) * 4 + (GQA_WIDTH, GQA_KV_WIDTH, GQA_KV_WIDTH, GQA_WIDTH) + (SG_WIDTH,) * 3 + (N_BRANCH * D_MODEL,)
_OFFS = [int(v) for v in np.concatenate([[0], np.cumsum(_SPLITS)])]
(O_NAQ, O_NAK, O_NAV, O_NAZ, O_GQQ, O_GQK, O_GQV, O_GQZ, O_SGU, O_SGV, O_SGZ, O_GATE, O_END) = _OFFS

PROJ_SEGMENTS = ((O_NAQ, O_NAZ), (O_GQQ, O_GQZ), (O_SGV, O_SGZ))
P_NAQ, P_NAK, P_NAV = (0, 0), (0, NA_WIDTH), (0, 2 * NA_WIDTH)
P_GQQ, P_GQK, P_GQV = (1, 0), (1, GQA_WIDTH), (1, GQA_WIDTH + GQA_KV_WIDTH)
P_SGV = (2, 0)

PROJ_TM = 1024
PROJ_SUB = 256
NA_ROWS = 4
NA_TQ = NA_ROWS * GRID_W
NA_WIN_ROWS = 12
NA_WIN = NA_WIN_ROWS * GRID_W
NA_GROUP_COUNT = ROWS // NA_ROWS
NA_GROUPS_PER_STEP = 2
NA_TYPES = 3
NA_SOFTMAX_ROWS = 16
GQA_TQ = 1024
GQA_STAGE_ROWS = 512
GQA_SOFTMAX_ROWS = 16
MERGE_TM = 1024
MERGE_SUB = 256


def _cparams(n_axes):
    return pltpu.CompilerParams(dimension_semantics=("arbitrary",) * n_axes, vmem_limit_bytes=VMEM_LIMIT)


def _fixed_spec(block_shape, index):
    return pl.BlockSpec(block_shape, lambda *_: index, pipeline_mode=pl.Buffered(1))


def _column_spec(rows, layer, start, stop):
    return _fixed_spec((pl.Element(1), pl.Element(rows), pl.Element(stop - start)), (layer, 0, start))


def _layer_norm(x, g, b):
    mu = jnp.mean(x, axis=-1, keepdims=True)
    xc = x - mu
    var = jnp.mean(xc * xc, axis=-1, keepdims=True)
    return xc * lax.rsqrt(var + LN_EPS) * g + b


def _lane_is_first_head(shape):
    return lax.broadcasted_iota(jnp.int32, shape, len(shape) - 1) % LANES < HEAD_DIM


def _head_lane_masks():
    first = jnp.where(_lane_is_first_head((1, LANES)), 1.0, 0.0)
    return first.astype(BF16), (1.0 - first).astype(BF16)


def _na_window_start(step):
    return min(max(NA_ROWS * step - NA_KH // 2, 0), ROWS - NA_WIN_ROWS)


_NA_TYPE_STEPS = (0, 1, NA_GROUP_COUNT - 1)


def _na_tile_plan():
    plan = {}
    for t, step in enumerate(_NA_TYPE_STEPS):
        w0 = _na_window_start(step)
        for dq in range(NA_ROWS):
            r = NA_ROWS * step + dq
            row_start = min(max(r - NA_KH // 2, 0), ROWS - NA_KH)
            for j in range(NA_WIN_ROWS):
                key_row = w0 + j
                inside = row_start <= key_row < row_start + NA_KH
                plan[(t, dq, j)] = key_row - r + (NA_KH - 1) if inside else None
    return plan


def _na_bias_body(rpb_ref, out_ref):
    layer, pair = pl.program_id(0), pl.program_id(1)
    qi = lax.broadcasted_iota(jnp.int32, (GRID_W, GRID_W), 0)
    ki = lax.broadcasted_iota(jnp.int32, (GRID_W, GRID_W), 1)
    lane_entry = jnp.clip(lax.broadcasted_iota(jnp.int32, (8, LANES), 1) - (GRID_W - (NA_KW - 1)), 0, RPB_COLS - 1)
    col_start = jnp.clip(qi - NA_KW // 2, 0, GRID_W - NA_KW)
    col_valid = (ki >= col_start) & (ki < col_start + NA_KW)
    masked = jnp.full((GRID_W, GRID_W), MASK_VALUE, F32)
    plan = _na_tile_plan()
    for hsel in range(2):
        base = ((layer * NA_HEADS + 2 * pair + hsel) * RPB_ROWS) * RPB_COLS
        for ro in range(RPB_ROWS):
            users = [key for key, val in plan.items() if val == ro]
            if not users:
                continue
            pattern = jnp.zeros((8, LANES), F32)
            for c in range(RPB_COLS):
                pattern = jnp.where(lane_entry == c, rpb_ref[base + ro * RPB_COLS + c], pattern)
            rows = jnp.concatenate([pattern] * (GRID_W // 8), axis=0)
            tile = pltpu.roll(rows, GRID_W, 1, stride=1, stride_axis=0)[:, :GRID_W]
            tile = jnp.where(col_valid, tile * LOG2E, MASK_VALUE)
            for (t, dq, j) in users:
                row0 = hsel * NA_TQ + dq * GRID_W
                out_ref[0, t, 0, row0:row0 + GRID_W, j * GRID_W:(j + 1) * GRID_W] = tile
        for (t, dq, j), val in plan.items():
            if val is None:
                row0 = hsel * NA_TQ + dq * GRID_W
                out_ref[0, t, 0, row0:row0 + GRID_W, j * GRID_W:(j + 1) * GRID_W] = masked


def _na_bias(rpb_flat):
    return pl.pallas_call(
        _na_bias_body,
        grid=(DEPTH, NA_HEADS // 2),
        in_specs=[pl.BlockSpec(memory_space=pltpu.SMEM)],
        out_specs=pl.BlockSpec((1, NA_TYPES, 1, 2 * NA_TQ, NA_WIN), lambda l, p: (l, 0, p, 0, 0)),
        out_shape=jax.ShapeDtypeStruct((DEPTH, NA_TYPES, NA_HEADS // 2, 2 * NA_TQ, NA_WIN), F32),
        compiler_params=_cparams(2),
        name="na_bias",
    )(rpb_flat)


def _proj_body(first, *refs):
    if first:
        (x_ref, lng_ref, lnb_ref, w0_ref, w1_ref, w2_ref, b0_ref, b1_ref, b2_ref, gq_ref, gk_ref, bd_ref,
         cos_ref, s1_ref, s2_ref, sgg_ref, sgb_ref,
         h_ref, hb_ref, naq_ref, nak_ref, nav_ref, gqq_ref, gqkt_ref, gqv_ref, vn_ref) = refs
    else:
        (x_ref, w0_ref, w1_ref, w2_ref, b0_ref, b1_ref, b2_ref, gq_ref, gk_ref, bd_ref,
         cos_ref, s1_ref, s2_ref, sgg_ref, sgb_ref,
         naq_ref, nak_ref, nav_ref, gqq_ref, gqkt_ref, gqv_ref, vn_ref) = refs
    w_refs = tuple(r.at[0] for r in (w0_ref, w1_ref, w2_ref))
    b_refs = tuple(r.at[0] for r in (b0_ref, b1_ref, b2_ref))
    for r0 in range(0, PROJ_TM, PROJ_SUB):
        _proj_rows(first, slice(r0, r0 + PROJ_SUB), x_ref, lng_ref if first else None, lnb_ref if first else None,
                   w_refs, b_refs, gq_ref, gk_ref, bd_ref, cos_ref, s1_ref, s2_ref, sgg_ref, sgb_ref,
                   h_ref if first else None, hb_ref if first else None, naq_ref, nak_ref, nav_ref, gqq_ref, gqkt_ref, gqv_ref, vn_ref)


def _proj_rows(first, rows, x_ref, lng_ref, lnb_ref, w_refs, b_refs, gq_ref, gk_ref, bd_ref, cos_ref, s1_ref, s2_ref,
               sgg_ref, sgb_ref, h_ref, hb_ref, naq_ref, nak_ref, nav_ref, gqq_ref, gqkt_ref, gqv_ref, vn_ref):
    x = x_ref[rows, :]
    if first:
        x = _layer_norm(x, lng_ref[...], lnb_ref[...])
        h_ref[rows, :] = x
        hb_ref[rows, :] = x.astype(BF16)
    xb = x.astype(BF16)

    def proj(where, width):
        seg, col = where
        return (jnp.dot(xb, w_refs[seg][:, col:col + width], preferred_element_type=F32)
                + b_refs[seg][:, col:col + width])

    def store_with_ones(dst_ref, val, col, second_copy):
        first_head = _lane_is_first_head(val.shape)
        dst_ref[rows, col:col + LANES] = jnp.where(first_head, val, 1.0).astype(BF16)
        dst_ref[rows, second_copy + col:second_copy + col + LANES] = jnp.where(first_head, 1.0, val).astype(BF16)

    def rms_rope(val, gain):
        sq = val * val
        hi = sq.astype(BF16)
        lo = (sq - hi.astype(F32)).astype(BF16)
        ssum = jnp.dot(jnp.concatenate([hi, lo], axis=1), bd_ref[...], preferred_element_type=F32)
        xn = val * lax.rsqrt(ssum * (1.0 / HEAD_DIM) + RMS_EPS) * gain
        return (xn * cos_ref[rows, :] + pltpu.roll(xn, LANES - 1, 1) * s1_ref[rows, :]
                + pltpu.roll(xn, 1, 1) * s2_ref[rows, :])

    naq_ref[rows, :] = (proj(P_NAQ, NA_WIDTH) * SCORE_SCALE).astype(BF16)
    nak_ref[rows, :] = proj(P_NAK, NA_WIDTH).astype(BF16)
    nav_ref[rows, :] = proj(P_NAV, NA_WIDTH).astype(BF16)

    q = proj(P_GQQ, GQA_WIDTH)
    for p in range(GQA_WIDTH // LANES):
        gqq_ref[rows, p * LANES:(p + 1) * LANES] = rms_rope(q[:, p * LANES:(p + 1) * LANES], gq_ref[...]).astype(BF16)
    k = rms_rope(proj(P_GQK, GQA_KV_WIDTH), gk_ref[...])
    v = proj(P_GQV, GQA_KV_WIDTH)
    first_head = _lane_is_first_head(k.shape)
    k_sw = pltpu.roll(k, HEAD_DIM, 1)
    v_sw = pltpu.roll(v, HEAD_DIM, 1)
    gqkt_ref[0, 0, :, rows] = jnp.where(first_head, k, k_sw).T.astype(BF16)
    gqkt_ref[0, 1, :, rows] = jnp.where(first_head, k_sw, k).T.astype(BF16)
    store_with_ones(gqv_ref, jnp.where(first_head, v, v_sw), 0, LANES)
    store_with_ones(gqv_ref, jnp.where(first_head, v_sw, v), 2 * LANES, LANES)

    vn_ref[rows, :] = _layer_norm(proj(P_SGV, SG_WIDTH), sgg_ref[...], sgb_ref[...]).astype(BF16)


def _proj(layer, first, x, ln_g, ln_b, wb, bb, gq, gk, bd, cos_t, s1_t, s2_t, sgg, sgb):
    tm = PROJ_TM
    tiles_per_seq = SEQ // tm
    row = lambda i: (i, 0)
    pos = lambda i: (i % tiles_per_seq, 0)
    in_specs = [pl.BlockSpec((tm, D_MODEL), row)]
    args = [x]
    if first:
        in_specs += [_fixed_spec((1, D_MODEL), (0, 0)), _fixed_spec((1, D_MODEL), (0, 0))]
        args += [ln_g, ln_b]
    in_specs += [_column_spec(D_MODEL, layer, a, b) for a, b in PROJ_SEGMENTS]
    in_specs += [_column_spec(1, layer, a, b) for a, b in PROJ_SEGMENTS]
    in_specs += [_fixed_spec((None, 1, LANES), (layer, 0, 0)), _fixed_spec((None, 1, LANES), (layer, 0, 0)),
                 _fixed_spec((2 * LANES, LANES), (0, 0)),
                 pl.BlockSpec((tm, LANES), pos), pl.BlockSpec((tm, LANES), pos), pl.BlockSpec((tm, LANES), pos),
                 _fixed_spec((None, 1, SG_WIDTH), (layer, 0, 0)), _fixed_spec((None, 1, SG_WIDTH), (layer, 0, 0))]
    args += [wb] * len(PROJ_SEGMENTS) + [bb] * len(PROJ_SEGMENTS) + [gq, gk, bd, cos_t, s1_t, s2_t, sgg, sgb]
    out_shape, out_specs = [], []
    if first:
        out_shape += [jax.ShapeDtypeStruct((TOKENS, D_MODEL), F32), jax.ShapeDtypeStruct((TOKENS, D_MODEL), BF16)]
        out_specs += [pl.BlockSpec((tm, D_MODEL), row), pl.BlockSpec((tm, D_MODEL), row)]
    out_shape += [
        jax.ShapeDtypeStruct((TOKENS, NA_WIDTH), BF16),
        jax.ShapeDtypeStruct((TOKENS, NA_WIDTH), BF16),
        jax.ShapeDtypeStruct((TOKENS, NA_WIDTH), BF16),
        jax.ShapeDtypeStruct((TOKENS, GQA_WIDTH), BF16),
        jax.ShapeDtypeStruct((BATCH, GQA_KV_HEADS, LANES, SEQ), BF16),
        jax.ShapeDtypeStruct((TOKENS, 2 * GQA_KV_HEADS * LANES), BF16),
        jax.ShapeDtypeStruct((TOKENS, SG_WIDTH), BF16),
    ]
    out_specs += [
        pl.BlockSpec((tm, NA_WIDTH), row),
        pl.BlockSpec((tm, NA_WIDTH), row),
        pl.BlockSpec((tm, NA_WIDTH), row),
        pl.BlockSpec((tm, GQA_WIDTH), row),
        pl.BlockSpec((1, GQA_KV_HEADS, LANES, tm), lambda i: (i // tiles_per_seq, 0, 0, i % tiles_per_seq)),
        pl.BlockSpec((tm, 2 * GQA_KV_HEADS * LANES), row),
        pl.BlockSpec((tm, SG_WIDTH), row),
    ]
    return pl.pallas_call(
        functools.partial(_proj_body, first),
        grid=(TOKENS // tm,),
        in_specs=in_specs,
        out_specs=out_specs,
        out_shape=out_shape,
        compiler_params=_cparams(1),
        name="proj_first" if first else "proj",
    )(*args)


def _silu_gate(x_ref, wz_ref, bz_ref):
    z = jnp.dot(x_ref[...].astype(BF16), wz_ref[0], preferred_element_type=F32) + bz_ref[0]
    return z * jax.nn.sigmoid(z)


def _softmax_numerators(s_ref, e_ref, n_rows, block_rows, l_ref=None):
    for r in range(0, n_rows, block_rows):
        rows = slice(r, r + block_rows)
        blk = s_ref[rows, :]
        e = jnp.exp2(blk - jnp.max(blk, axis=-1, keepdims=True))
        e_ref[rows, :] = e.astype(BF16)
        if l_ref is not None:
            l_ref[rows, :] = jnp.broadcast_to(jnp.sum(e, axis=-1, keepdims=True), (block_rows, LANES))


def _weighted_values(e, v_ones):
    o = jnp.dot(e, v_ones, preferred_element_type=F32)
    return o / pltpu.roll(o, HEAD_DIM, 1)


def _run_stages(n_stages, scores, finish, before_last):
    scores(0)
    for i in range(n_stages):
        if i + 1 < n_stages:
            scores(i + 1)
        else:
            before_last()
        finish(i)


def _na_body(q_ref, k_ref, v_ref, bias0_ref, bias1_ref, x_ref, wz_ref, bz_ref, out_ref,
             s_scr, e_scr, l_scr, y_scr, g_scr):
    step = pl.program_id(0)
    bias_refs = (bias0_ref, bias1_ref)
    first_head = _lane_is_first_head((NA_TQ, LANES))
    keep_first, keep_second = _head_lane_masks()
    stages = [(g, p) for g in range(NA_GROUPS_PER_STEP) for p in range(NA_WIDTH // LANES)]

    def window(g):
        w0 = jnp.clip(NA_ROWS * (NA_GROUPS_PER_STEP * step + g) - NA_KH // 2, 0, ROWS - NA_WIN_ROWS)
        return pl.ds(pl.multiple_of(w0 * GRID_W, 2 * LANES), NA_WIN)

    def scores(i):
        g, p = stages[i]
        rows, col = slice(g * NA_TQ, (g + 1) * NA_TQ), p * LANES
        q = q_ref[rows, col:col + LANES]
        qs = jnp.concatenate([q * keep_first, q * keep_second], axis=0)
        s = lax.dot_general(qs, k_ref[window(g), col:col + LANES], (((1,), (1,)), ((), ())), preferred_element_type=F32)
        s_scr[i % 2] = s + bias_refs[g][0, 0, p]

    def finish(i):
        g, p = stages[i]
        rows, col, slot = slice(g * NA_TQ, (g + 1) * NA_TQ), p * LANES, i % 2
        _softmax_numerators(s_scr.at[slot], e_scr.at[slot], 2 * NA_TQ, NA_SOFTMAX_ROWS, l_scr.at[slot])
        o = jnp.dot(e_scr[slot], v_ref[window(g), col:col + LANES], preferred_element_type=F32) / l_scr[slot]
        y_scr[rows, col:col + LANES] = jnp.where(first_head, o[:NA_TQ], o[NA_TQ:])

    def gate():
        g_scr[...] = _silu_gate(x_ref, wz_ref, bz_ref)

    _run_stages(len(stages), scores, finish, gate)
    out_ref[...] = (y_scr[...] * g_scr[...]).astype(BF16)


def _na(layer, q2, k, v, bias, x, wb, bb):
    steps = NA_GROUP_COUNT // NA_GROUPS_PER_STEP
    tq = NA_GROUPS_PER_STEP * NA_TQ

    def bias_spec(g):
        def index(s, b):
            group = NA_GROUPS_PER_STEP * s + g
            return (layer, jnp.minimum(group, 1) + jnp.maximum(group - (NA_GROUP_COUNT - 2), 0), 0, 0, 0)
        return pl.BlockSpec((1, 1, NA_HEADS // 2, 2 * NA_TQ, NA_WIN), index)

    row = lambda s, b: (b * steps + s, 0)
    return pl.pallas_call(
        _na_body,
        grid=(steps, BATCH),
        in_specs=[
            pl.BlockSpec((tq, NA_WIDTH), row),
            pl.BlockSpec((SEQ, NA_WIDTH), lambda s, b: (b, 0)),
            pl.BlockSpec((SEQ, NA_WIDTH), lambda s, b: (b, 0)),
            bias_spec(0), bias_spec(1),
            pl.BlockSpec((tq, D_MODEL), row),
            _column_spec(D_MODEL, layer, O_NAZ, O_GQQ),
            _column_spec(1, layer, O_NAZ, O_GQQ),
        ],
        out_specs=pl.BlockSpec((tq, NA_WIDTH), row),
        out_shape=jax.ShapeDtypeStruct((TOKENS, NA_WIDTH), BF16),
        scratch_shapes=[pltpu.VMEM((2, 2 * NA_TQ, NA_WIN), F32), pltpu.VMEM((2, 2 * NA_TQ, NA_WIN), BF16),
                        pltpu.VMEM((2, 2 * NA_TQ, LANES), F32), pltpu.VMEM((tq, NA_WIDTH), F32),
                        pltpu.VMEM((tq, NA_WIDTH), F32)],
        compiler_params=_cparams(2),
        name="na",
    )(q2, k, v, bias, bias, x, wb, bb)


def _gqa_body(q_ref, kt_ref, v_ref, x_ref, wz_ref, bz_ref, out_ref, s_scr, e_scr, y_scr, g_scr):
    m = GQA_STAGE_ROWS
    tiles_per_kv = GQA_WIDTH // LANES // GQA_KV_HEADS
    first_head = _lane_is_first_head((m, LANES))
    keep_head = _head_lane_masks()
    stages = [(p, rb, h) for p in range(GQA_WIDTH // LANES) for rb in range(GQA_TQ // m) for h in range(2)]

    def scores(i):
        p, rb, h = stages[i]
        q = q_ref[rb * m:(rb + 1) * m, p * LANES:(p + 1) * LANES] * keep_head[h]
        s_scr[i % 2] = jnp.dot(q, kt_ref[0, p // tiles_per_kv], preferred_element_type=F32)

    def finish(i):
        p, rb, h = stages[i]
        rows, col, slot, kv = slice(rb * m, (rb + 1) * m), p * LANES, i % 2, p // tiles_per_kv
        _softmax_numerators(s_scr.at[slot], e_scr.at[slot], m, GQA_SOFTMAX_ROWS)
        vcol = (2 * kv + h) * LANES
        o = _weighted_values(e_scr[slot], v_ref[:, vcol:vcol + LANES])
        if h == 0:
            y_scr[rows, col:col + LANES] = o
        else:
            y_scr[rows, col:col + LANES] = jnp.where(first_head, y_scr[rows, col:col + LANES], o)

    def gate():
        g_scr[...] = _silu_gate(x_ref, wz_ref, bz_ref)

    _run_stages(len(stages), scores, finish, gate)
    out_ref[...] = (y_scr[...] * g_scr[...]).astype(BF16)


def _gqa(layer, q2, kt, v, x, wb, bb):
    steps = SEQ // GQA_TQ
    row = lambda b, s: (b * steps + s, 0)
    return pl.pallas_call(
        _gqa_body,
        grid=(BATCH, steps),
        in_specs=[
            pl.BlockSpec((GQA_TQ, GQA_WIDTH), row),
            pl.BlockSpec((1, GQA_KV_HEADS, LANES, SEQ), lambda b, s: (b, 0, 0, 0)),
            pl.BlockSpec((SEQ, 2 * GQA_KV_HEADS * LANES), lambda b, s: (b, 0)),
            pl.BlockSpec((GQA_TQ, D_MODEL), row),
            _column_spec(D_MODEL, layer, O_GQZ, O_SGU),
            _column_spec(1, layer, O_GQZ, O_SGU),
        ],
        out_specs=pl.BlockSpec((GQA_TQ, GQA_WIDTH), row),
        out_shape=jax.ShapeDtypeStruct((TOKENS, GQA_WIDTH), BF16),
        scratch_shapes=[pltpu.VMEM((2, GQA_STAGE_ROWS, SEQ), F32), pltpu.VMEM((2, GQA_STAGE_ROWS, SEQ), BF16),
                        pltpu.VMEM((GQA_TQ, GQA_WIDTH), F32), pltpu.VMEM((GQA_TQ, GQA_WIDTH), F32)],
        compiler_params=_cparams(2),
        name="gqa",
    )(q2, kt, v, x, wb, bb)


def _merge_body(x_ref, ya_ref, yb_ref, vn_ref, wu_ref, bu_ref, wz_ref, bz_ref, ws_ref, bs_ref,
                wg_ref, bg_ref, wo_ref, wa_ref, wb_ref, wc_ref, bo_ref, lng_ref, lnb_ref, out_ref, *rest):
    outb_ref, mixed_ref = rest if len(rest) == 2 else (None, rest[0])
    for r0 in range(0, MERGE_TM, MERGE_SUB):
        _merge_rows(slice(r0, r0 + MERGE_SUB), x_ref, ya_ref, yb_ref, vn_ref, wu_ref, bu_ref, wz_ref, bz_ref, ws_ref,
                    bs_ref, wg_ref, bg_ref, wo_ref, wa_ref, wb_ref, wc_ref, bo_ref, lng_ref, lnb_ref, out_ref, outb_ref,
                    mixed_ref)


def _merge_rows(rows, x_ref, ya_ref, yb_ref, vn_ref, wu_ref, bu_ref, wz_ref, bz_ref, ws_ref, bs_ref,
                wg_ref, bg_ref, wo_ref, wa_ref, wb_ref, wc_ref, bo_ref, lng_ref, lnb_ref, out_ref, outb_ref, mixed_ref):
    n_chunks = MERGE_SUB // SG_CHUNK
    x = x_ref[rows, :]
    xb = x.astype(BF16)

    first_head = _lane_is_first_head((SG_CHUNK, LANES))
    chunk_rows = [slice(rows.start + c * SG_CHUNK, rows.start + (c + 1) * SG_CHUNK) for c in range(n_chunks)]
    for pp in range(SG_WIDTH // LANES):
        col = pp * LANES
        rhs = jnp.concatenate([vn_ref[cr, col:col + LANES] for cr in chunk_rows], axis=1)
        res = jnp.dot(ws_ref[pp], rhs, preferred_element_type=F32)
        for c, cr in enumerate(chunk_rows):
            blk = jnp.where(first_head, res[:SG_CHUNK, c * LANES:(c + 1) * LANES], res[SG_CHUNK:, c * LANES:(c + 1) * LANES])
            mixed_ref[cr, col:col + LANES] = blk + bs_ref[:, col:col + LANES]
    u = jnp.dot(xb, wu_ref[0], preferred_element_type=F32) + bu_ref[0]
    z = jnp.dot(xb, wz_ref[0], preferred_element_type=F32) + bz_ref[0]
    tc = (u * mixed_ref[rows, :] * (z * jax.nn.sigmoid(z))).astype(BF16)

    def gate(i):
        lo = i * D_MODEL
        return jax.nn.sigmoid(jnp.dot(xb, wg_ref[0, :, lo:lo + D_MODEL], preferred_element_type=F32)
                              + bg_ref[0, :, lo:lo + D_MODEL])

    merged = gate(0) * jnp.dot(ya_ref[rows, :], wa_ref[...], preferred_element_type=F32)
    merged = merged + gate(1) * jnp.dot(yb_ref[rows, :], wb_ref[...], preferred_element_type=F32)
    merged = merged + gate(2) * jnp.dot(tc, wc_ref[...], preferred_element_type=F32)
    sub = jnp.dot(merged.astype(BF16), wo_ref[...], preferred_element_type=F32) + bo_ref[...]
    y = _layer_norm(DEEPNORM_ALPHA * x + sub, lng_ref[...], lnb_ref[...])
    out_ref[rows, :] = y
    if outb_ref is not None:
        outb_ref[rows, :] = y.astype(BF16)


def _merge(layer, emit_bf16, x, ya, yb, vn, wb, bb, ws, bs, wo, wa, wbb, wc, bo, lng, lnb):
    tm = MERGE_TM
    row = lambda i: (i, 0)
    out_shape = [jax.ShapeDtypeStruct((TOKENS, D_MODEL), F32)]
    if emit_bf16:
        out_shape.append(jax.ShapeDtypeStruct((TOKENS, D_MODEL), BF16))
    vec = lambda width: _fixed_spec((None, 1, width), (layer, 0, 0))
    return pl.pallas_call(
        _merge_body,
        grid=(TOKENS // tm,),
        in_specs=[
            pl.BlockSpec((tm, D_MODEL), row),
            pl.BlockSpec((tm, NA_WIDTH), row),
            pl.BlockSpec((tm, GQA_WIDTH), row),
            pl.BlockSpec((tm, SG_WIDTH), row),
            _column_spec(D_MODEL, layer, O_SGU, O_SGV), _column_spec(1, layer, O_SGU, O_SGV),
            _column_spec(D_MODEL, layer, O_SGZ, O_GATE), _column_spec(1, layer, O_SGZ, O_GATE),
            _fixed_spec((None, SG_GROUPS // 2, 2 * SG_CHUNK, SG_CHUNK), (layer, 0, 0, 0)),
            _fixed_spec((None, SG_CHUNK, SG_WIDTH), (layer, 0, 0)),
            _column_spec(D_MODEL, layer, O_GATE, O_END), _column_spec(1, layer, O_GATE, O_END),
            _fixed_spec((None, D_MODEL, D_MODEL), (layer, 0, 0)),
            _fixed_spec((None, NA_WIDTH, D_MODEL), (layer, 0, 0)),
            _fixed_spec((None, GQA_WIDTH, D_MODEL), (layer, 0, 0)),
            _fixed_spec((None, SG_WIDTH, D_MODEL), (layer, 0, 0)),
            vec(D_MODEL), vec(D_MODEL), vec(D_MODEL),
        ],
        out_specs=[pl.BlockSpec((tm, D_MODEL), row)] * len(out_shape),
        out_shape=out_shape,
        scratch_shapes=[pltpu.VMEM((tm, SG_WIDTH), F32)],
        compiler_params=_cparams(1),
        name="merge",
    )(x, ya, yb, vn, wb, bb, wb, bb, ws, bs, wb, bb, wo, wa, wbb, wc, bo, lng, lnb)


def _rope_tables():
    t = np.arange(SEQ)
    row = (t // GRID_W).astype(np.float32)
    col = (t % GRID_W).astype(np.float32)
    freqs = np.float32(ROPE_THETA) ** (-np.arange(0, ROPE_AXIS_DIM, 2, dtype=np.float32) / np.float32(ROPE_AXIS_DIM))
    ang = np.concatenate([row[:, None] * freqs, col[:, None] * freqs], axis=-1).astype(np.float32)
    cos = np.cos(ang.astype(np.float64)).astype(np.float32)
    sin = np.sin(ang.astype(np.float64)).astype(np.float32)
    pair = (np.arange(LANES) % HEAD_DIM) // 2
    even = (np.arange(LANES) % 2 == 0)[None, :]
    cos_t = cos[:, pair]
    s1_t = np.where(even, -sin[:, pair], 0.0).astype(np.float32)
    s2_t = np.where(even, 0.0, sin[:, pair]).astype(np.float32)
    return jnp.asarray(cos_t), jnp.asarray(s1_t), jnp.asarray(s2_t)


def kernel(x, ln_in_g, ln_in_b, w_in, b_in, na_rpb, q_norm_g, k_norm_g, sg_ln_g, sg_ln_b, sg_w, sg_b,
           w_br_a, w_br_b, w_br_c, w_out, b_out, ln_post_g, ln_post_b):
    assert x.shape == (BATCH, SEQ, D_MODEL) and w_in.shape == (DEPTH, D_MODEL, O_END)
    cos_t, s1_t, s2_t = _rope_tables()
    bd = jnp.asarray(np.tile(np.kron(np.eye(2), np.ones((HEAD_DIM, HEAD_DIM))), (2, 1)), BF16)
    vec3 = lambda v: v.reshape(DEPTH, 1, -1)

    wb, bb = w_in.astype(BF16), vec3(b_in)
    wo, wa, wbb, wc = (w.astype(BF16) for w in (w_out, w_br_a, w_br_b, w_br_c))
    ws = sg_w.reshape(DEPTH, SG_GROUPS // 2, 2 * SG_CHUNK, SG_CHUNK).astype(BF16)
    bs = jnp.repeat(jnp.swapaxes(sg_b, 1, 2), HEAD_DIM, axis=2)
    gq = vec3(jnp.tile(q_norm_g * SCORE_SCALE, (1, 2)))
    gk = vec3(jnp.tile(k_norm_g, (1, 2)))
    sgg, sgb = vec3(sg_ln_g), vec3(sg_ln_b)
    bo, lng, lnb = vec3(b_out), vec3(ln_post_g), vec3(ln_post_b)
    bias = _na_bias(na_rpb.reshape(-1))

    h = x.reshape(TOKENS, D_MODEL)
    for l in range(DEPTH):
        proj_args = (wb, bb, gq, gk, bd, cos_t, s1_t, s2_t, sgg, sgb)
        if l == 0:
            h, hb, naq, nak, nav, gqq, gqkt, gqv, vn = _proj(l, True, h, ln_in_g.reshape(1, -1), ln_in_b.reshape(1, -1),
                                                             *proj_args)
        else:
            naq, nak, nav, gqq, gqkt, gqv, vn = _proj(l, False, hb, None, None, *proj_args)
        ya = _na(l, naq, nak, nav, bias, hb, wb, bb)
        yb = _gqa(l, gqq, gqkt, gqv, hb, wb, bb)
        outs = _merge(l, l + 1 < DEPTH, h, ya, yb, vn, wb, bb, ws, bs, wo, wa, wbb, wc, bo, lng, lnb)
        h, hb = (outs[0], outs[1]) if l + 1 < DEPTH else (outs[0], None)
    return h.reshape(BATCH, SEQ, D_MODEL)
```
